```python
import math
import jax, jax.numpy as jnp
from jax import lax
import numpy as np

D_MODEL = 2048
BATCH = 2
SEQ = 16384
DEPTH = 2

CHUNK = 64
EPS = 1e-5
ROPE_THETA = 10000.0
FF_DIM = 256 * ((8 * D_MODEL // 3 + 255) // 256)
SSD_HEAD_DIM = 64
SSD_HEADS = D_MODEL // SSD_HEAD_DIM
SSD_WIDTH = SSD_HEADS * SSD_HEAD_DIM
SSD_GROUPS = 4
SSD_STATE = 128
SSD_CONV = 4
SSD_BLOCK = CHUNK
SSD_XBC = SSD_WIDTH + 2 * SSD_GROUPS * SSD_STATE
DSA_HEAD_DIM = 64
DSA_HEADS = D_MODEL // 128
DSA_WIDTH = DSA_HEADS * DSA_HEAD_DIM
IDX_HEADS = D_MODEL // 128
IDX_DIM = 64
DSA_TOPK_MAX = 256
DSA_QBLOCK = 128
AB_SIZES = (SSD_WIDTH, SSD_XBC, SSD_HEADS, DSA_WIDTH, DSA_WIDTH, DSA_WIDTH, IDX_HEADS * IDX_DIM, IDX_DIM, IDX_HEADS)
AB_IN = sum(AB_SIZES)
AB_OUT = SSD_WIDTH + DSA_WIDTH
BAND_HEAD_DIM = 64
BAND_HEADS = D_MODEL // BAND_HEAD_DIM
BAND_WIDTH = BAND_HEADS * BAND_HEAD_DIM
BAND_PREV = 8
BAND_KEYS = (BAND_PREV + 1) * CHUNK
REL_PAST = 256
REL_SIZE = REL_PAST + CHUNK

kernel_name = 'hybrid_ssd_dsa_chunkband_macaron'


def rms_norm(x, gain):
    x32 = x.astype(jnp.float32)
    y = x32 * lax.rsqrt(jnp.mean(jnp.square(x32), axis=-1, keepdims=True) + EPS)
    return (y * gain.astype(jnp.float32)).astype(x.dtype)


def group_rms_norm(y32, gain, groups):
    b, s, w = y32.shape
    g = y32.reshape(b, s, groups, w // groups)
    g = g * lax.rsqrt(jnp.mean(jnp.square(g), axis=-1, keepdims=True) + EPS)
    return g.reshape(b, s, w) * gain.astype(jnp.float32)


def swiglu(h, w_in, w_out):
    g, u = jnp.split(h @ w_in, 2, axis=-1)
    return (jax.nn.silu(g) * u) @ w_out


def rotary(t, pos):
    half = t.shape[-1] // 2
    inv = ROPE_THETA ** (-jnp.arange(half, dtype=jnp.float32) / half)
    ang = pos.astype(jnp.float32)[:, None] * inv[None, :]
    cos = jnp.cos(ang)[:, None, :]
    sin = jnp.sin(ang)[:, None, :]
    t32 = t.astype(jnp.float32)
    t1, t2 = t32[..., :half], t32[..., half:]
    return jnp.concatenate([t1 * cos - t2 * sin, t1 * sin + t2 * cos], axis=-1).astype(t.dtype)


def causal_depthwise_conv(t, w, bias):
    ch = t.shape[-1]
    out = lax.conv_general_dilated(t, w[:, None, :].astype(t.dtype), window_strides=(1,),
                                   padding=[(w.shape[0] - 1, 0)],
                                   dimension_numbers=('NWC', 'WIO', 'NWC'),
                                   feature_group_count=ch)
    return out + bias.astype(t.dtype)


def ssd_chunked_scan(x, dt, a, bm, cm):
    b, S, H, P = x.shape
    G, N = bm.shape[2], bm.shape[3]
    E = H // G
    nc = S // SSD_BLOCK
    l = SSD_BLOCK

    def chunks(t):
        return jnp.moveaxis(t.reshape((b, nc, l) + t.shape[2:]), 1, 0)

    xg = chunks((x * dt[..., None]).reshape(b, S, G, E, P))
    ag = chunks((dt * a).reshape(b, S, G, E))
    bg = chunks(bm)
    cg = chunks(cm)
    tril = jnp.tril(jnp.ones((l, l), dtype=bool))

    def step(state, inp):
        xc, ac, bc, cc = inp
        a_cum = jnp.cumsum(ac, axis=1)
        a_t = jnp.moveaxis(a_cum, 1, -1)
        seg = a_t[..., :, None] - a_t[..., None, :]
        decay = jnp.exp(jnp.where(tril, seg, -jnp.inf))
        cb = jnp.einsum('btgn,bsgn->bgts', cc, bc)
        y_diag = jnp.einsum('bgts,bgets,bsgep->btgep', cb, decay, xc)
        y_off = jnp.einsum('btgn,bgepn,btge->btgep', cc, state, jnp.exp(a_cum))
        to_end = jnp.exp(a_cum[:, -1:] - a_cum)
        new_state = state * jnp.exp(a_cum[:, -1])[..., None, None] + \
            jnp.einsum('bsgn,bsge,bsgep->bgepn', bc, to_end, xc)
        return new_state, y_diag + y_off

    init = jnp.zeros((b, G, E, P, N), jnp.float32)
    _, y = lax.scan(step, init, (xg, ag, bg, cg))
    return jnp.moveaxis(y, 0, 1).reshape(b, S, H, P)


def dsa_sparse_attention(q, k, v, q_idx, k_idx, w_idx):
    b, S, H, Dh = q.shape
    nq = S // DSA_QBLOCK
    top = min(DSA_TOPK_MAX, S // 4)
    key_pos = jnp.arange(S)
    k_flat = k.reshape(b, S, H * Dh)
    v_flat = v.reshape(b, S, H * Dh)
    gather = jax.vmap(lambda arr, idx: arr[idx])

    def blocks(t):
        return jnp.moveaxis(t.reshape((b, nq, DSA_QBLOCK) + t.shape[2:]), 1, 0)

    def one_block(args):
        qb, qib, wb, start = args
        qpos = start + jnp.arange(DSA_QBLOCK)
        visible_end = (qpos // CHUNK + 1) * CHUNK
        admissible = key_pos[None, :] < visible_end[:, None]
        rel = jax.nn.relu(jnp.einsum('bqhd,bsd->bqhs', qib, k_idx, preferred_element_type=jnp.float32))
        score = jnp.einsum('bqhs,bqh->bqs', rel, wb.astype(jnp.float32))
        score = jnp.where(admissible[None], score, -jnp.inf)
        vals, sel = lax.top_k(score, top)
        valid = vals > -jnp.inf
        flat = sel.reshape(b, DSA_QBLOCK * top)
        kg = gather(k_flat, flat).reshape(b, DSA_QBLOCK, top, H, Dh)
        vg = gather(v_flat, flat).reshape(b, DSA_QBLOCK, top, H, Dh)
        logits = jnp.einsum('bqhd,bqkhd->bhqk', qb, kg, preferred_element_type=jnp.float32) * (Dh ** -0.5)
        logits = jnp.where(valid[:, None], logits, -jnp.inf)
        p = jax.nn.softmax(logits, axis=-1).astype(v.dtype)
        return jnp.einsum('bhqk,bqkhd->bqhd', p, vg)

    starts = jnp.arange(nq, dtype=jnp.int32) * DSA_QBLOCK
    out = lax.map(one_block, (blocks(q), blocks(q_idx), blocks(w_idx), starts))
    return jnp.moveaxis(out, 0, 1).reshape(b, S, H * Dh)


def chunk_band_attention(q, k, v, rel_bias):
    b, S, H, Dh = q.shape
    nc = S // CHUNK
    pad = BAND_PREV * CHUNK
    k_pad = jnp.pad(k, ((0, 0), (pad, 0), (0, 0), (0, 0)))
    v_pad = jnp.pad(v, ((0, 0), (pad, 0), (0, 0), (0, 0)))
    qi = jnp.arange(CHUNK)
    kj = jnp.arange(BAND_KEYS)
    dist = kj[None, :] - qi[:, None] - pad
    bias = rel_bias.astype(jnp.float32)[:, jnp.clip(dist, -REL_PAST, CHUNK - 1) + REL_PAST]
    q_chunks = jnp.moveaxis(q.reshape(b, nc, CHUNK, H, Dh), 1, 0)

    def one_chunk(args):
        qc, c = args
        start = c * CHUNK
        kb = lax.dynamic_slice_in_dim(k_pad, start, BAND_KEYS, axis=1)
        vb = lax.dynamic_slice_in_dim(v_pad, start, BAND_KEYS, axis=1)
        valid = (start - pad + kj) >= 0
        logits = jnp.einsum('bqhd,bkhd->bhqk', qc, kb, preferred_element_type=jnp.float32) * (Dh ** -0.5) + bias
        logits = jnp.where(valid, logits, -jnp.inf)
        p = jax.nn.softmax(logits, axis=-1).astype(v.dtype)
        return jnp.einsum('bhqk,bkhd->bqhd', p, vb)

    out = lax.map(one_chunk, (q_chunks, jnp.arange(nc, dtype=jnp.int32)))
    return jnp.moveaxis(out, 0, 1).reshape(b, S, H * Dh)


def ssd_dsa_mixer(h, w_in, conv_w, conv_b, dt_bias, a_log, d_skip, out_norm, w_out, pos):
    b, S, _ = h.shape
    split_at = np.cumsum(AB_SIZES)[:-1].tolist()
    z, xbc, dt_raw, q, k, v, q_idx, k_idx, w_idx = jnp.split(h @ w_in, split_at, axis=-1)
    xbc = jax.nn.silu(causal_depthwise_conv(xbc, conv_w, conv_b))
    xs, bm, cm = jnp.split(xbc, [SSD_WIDTH, SSD_WIDTH + SSD_GROUPS * SSD_STATE], axis=-1)
    xs = xs.reshape(b, S, SSD_HEADS, SSD_HEAD_DIM).astype(jnp.float32)
    dt = jax.nn.softplus(dt_raw.astype(jnp.float32) + dt_bias.astype(jnp.float32))
    a = -jnp.exp(a_log.astype(jnp.float32))
    y = ssd_chunked_scan(xs, dt, a,
                         bm.reshape(b, S, SSD_GROUPS, SSD_STATE).astype(jnp.float32),
                         cm.reshape(b, S, SSD_GROUPS, SSD_STATE).astype(jnp.float32))
    y = y + xs * d_skip.astype(jnp.float32)[:, None]
    y = y.reshape(b, S, SSD_WIDTH) * jax.nn.silu(z.astype(jnp.float32))
    y = group_rms_norm(y, out_norm, SSD_GROUPS).astype(h.dtype)
    q = rotary(q.reshape(b, S, DSA_HEADS, DSA_HEAD_DIM), pos)
    k = rotary(k.reshape(b, S, DSA_HEADS, DSA_HEAD_DIM), pos)
    v = v.reshape(b, S, DSA_HEADS, DSA_HEAD_DIM)
    q_idx = rotary(q_idx.reshape(b, S, IDX_HEADS, IDX_DIM), pos)
    k_idx = rotary(k_idx.reshape(b, S, 1, IDX_DIM), pos)[:, :, 0]
    w_idx = w_idx * (IDX_HEADS ** -0.5 * IDX_DIM ** -0.5)
    o = dsa_sparse_attention(q, k, v, q_idx, k_idx, w_idx)
    return jnp.concatenate([y, o.astype(h.dtype)], axis=-1) @ w_out


def band_mixer(h, w_qkv, rel_bias, w_out):
    b, S, _ = h.shape
    q, k, v = jnp.split(h @ w_qkv, 3, axis=-1)
    shp = (b, S, BAND_HEADS, BAND_HEAD_DIM)
    o = chunk_band_attention(q.reshape(shp), k.reshape(shp), v.reshape(shp), rel_bias)
    return o @ w_out


def setup_inputs(seed: int = 0) -> dict:
    key = jax.random.key(seed)
    ks = iter(jax.random.split(key, 40))
    f32 = jnp.float32
    n_even = (DEPTH + 1) // 2
    n_odd = DEPTH // 2

    def dense(shape, fan_in):
        return jax.random.normal(next(ks), shape, f32) * (fan_in ** -0.5)

    def gain(shape):
        return 1.0 + 0.02 * jax.random.normal(next(ks), shape, f32)

    def small(shape, scale):
        return scale * jax.random.normal(next(ks), shape, f32)

    x = jax.random.normal(next(ks), (BATCH, SEQ, D_MODEL), f32)
    ffn1_norm = gain((DEPTH, D_MODEL))
    ffn1_w_in = dense((DEPTH, D_MODEL, 2 * FF_DIM), D_MODEL)
    ffn1_w_out = dense((DEPTH, FF_DIM, D_MODEL), FF_DIM)
    mix_norm = gain((DEPTH, D_MODEL))
    ab_w_in = dense((n_even, D_MODEL, AB_IN), D_MODEL)
    ssd_conv_w = dense((n_even, SSD_CONV, SSD_XBC), SSD_CONV)
    ssd_conv_b = small((n_even, SSD_XBC), 0.02)
    u = jax.random.uniform(next(ks), (n_even, SSD_HEADS), f32)
    dt0 = jnp.exp(u * (math.log(0.1) - math.log(0.001)) + math.log(0.001))
    ssd_dt_bias = dt0 + jnp.log(-jnp.expm1(-dt0))
    ssd_a_log = jnp.log(jax.random.uniform(next(ks), (n_even, SSD_HEADS), f32, minval=1.0, maxval=16.0))
    ssd_d_skip = 1.0 + 0.1 * jax.random.normal(next(ks), (n_even, SSD_HEADS), f32)
    ssd_out_norm = gain((n_even, SSD_WIDTH))
    ab_w_out = dense((n_even, AB_OUT, D_MODEL), AB_OUT)
    c_w_qkv = dense((n_odd, D_MODEL, 3 * BAND_WIDTH), D_MODEL)
    c_rel_bias = small((n_odd, BAND_HEADS, REL_SIZE), 0.1)
    c_w_out = dense((n_odd, BAND_WIDTH, D_MODEL), BAND_WIDTH)
    ffn2_norm = gain((DEPTH, D_MODEL))
    ffn2_w_in = dense((DEPTH, D_MODEL, 2 * FF_DIM), D_MODEL)
    ffn2_w_out = dense((DEPTH, FF_DIM, D_MODEL), FF_DIM)
    final_norm = gain((D_MODEL,))
    return {'x': x, 'ffn1_norm': ffn1_norm, 'ffn1_w_in': ffn1_w_in, 'ffn1_w_out': ffn1_w_out,
            'mix_norm': mix_norm, 'ab_w_in': ab_w_in, 'ssd_conv_w': ssd_conv_w, 'ssd_conv_b': ssd_conv_b,
            'ssd_dt_bias': ssd_dt_bias, 'ssd_a_log': ssd_a_log, 'ssd_d_skip': ssd_d_skip,
            'ssd_out_norm': ssd_out_norm, 'ab_w_out': ab_w_out, 'c_w_qkv': c_w_qkv,
            'c_rel_bias': c_rel_bias, 'c_w_out': c_w_out, 'ffn2_norm': ffn2_norm,
            'ffn2_w_in': ffn2_w_in, 'ffn2_w_out': ffn2_w_out, 'final_norm': final_norm}


def reference(x, ffn1_norm, ffn1_w_in, ffn1_w_out, mix_norm, ab_w_in, ssd_conv_w, ssd_conv_b,
              ssd_dt_bias, ssd_a_log, ssd_d_skip, ssd_out_norm, ab_w_out, c_w_qkv, c_rel_bias,
              c_w_out, ffn2_norm, ffn2_w_in, ffn2_w_out, final_norm):
    pos = jnp.arange(x.shape[1], dtype=jnp.int32)
    for layer in range(DEPTH):
        x = x + 0.5 * swiglu(rms_norm(x, ffn1_norm[layer]), ffn1_w_in[layer], ffn1_w_out[layer])
        h = rms_norm(x, mix_norm[layer])
        i = layer // 2
        if layer % 2 == 0:
            x = x + ssd_dsa_mixer(h, ab_w_in[i], ssd_conv_w[i], ssd_conv_b[i], ssd_dt_bias[i],
                                  ssd_a_log[i], ssd_d_skip[i], ssd_out_norm[i], ab_w_out[i], pos)
        else:
            x = x + band_mixer(h, c_w_qkv[i], c_rel_bias[i], c_w_out[i])
        x = x + 0.5 * swiglu(rms_norm(x, ffn2_norm[layer]), ffn2_w_in[layer], ffn2_w_out[layer])
    return rms_norm(x, final_norm)
```

```python
import functools

import jax
import jax.numpy as jnp
import numpy as np
from jax import lax
from jax.experimental import pallas as pl
from jax.experimental.pallas import tpu as pltpu

F32 = jnp.float32
BF16 = jnp.bfloat16
I32 = jnp.int32

EPS = 1e-5
ROPE_THETA = 10000.0
CHUNK = 64
HEAD_DIM = 64
SSD_GROUPS = 4
SSD_STATE = 128
SSD_CONV = 4
DSA_TOPK_MAX = 256
DSA_QBLOCK = 128
BAND_PREV = 8
REL_PAST = 256

LANES = 128
SUBLANES = 8
VMEM_LIMIT_BYTES = 56 * 1024 * 1024

KIDX_LANE = 0
DT_LANE = 64
WIDX_LANE = 96

NEG_BIG = -1e30
HIGHEST = lax.Precision.HIGHEST


def _cparams(*sem):
    return pltpu.CompilerParams(dimension_semantics=sem, vmem_limit_bytes=VMEM_LIMIT_BYTES)


def _nt(a, b, **kw):
    return lax.dot_general(a, b, (((1,), (1,)), ((), ())), preferred_element_type=F32, **kw)


def _tn(a, b, **kw):
    return lax.dot_general(a, b, (((0,), (0,)), ((), ())), preferred_element_type=F32, **kw)


def _dot(a, b, **kw):
    return jnp.dot(a, b, preferred_element_type=F32, **kw)


def _rms(x, gain):
    ms = jnp.mean(x * x, axis=-1, keepdims=True)
    return x * lax.rsqrt(ms + EPS) * gain


def _silu(x):
    return x * jax.nn.sigmoid(x)


def _norm_matmul_kernel(x_ref, g_ref, w_ref, o_ref, xn_ref):
    @pl.when(pl.program_id(1) == 0)
    def _():
        xn_ref[...] = _rms(x_ref[...], g_ref[...]).astype(BF16)

    o_ref[...] = _dot(xn_ref[...], w_ref[...]).astype(o_ref.dtype)


def _norm_matmul(x, gain, w, tn, out_dtype):
    t, d = x.shape
    n = w.shape[1]
    tm = min(512, t)
    return pl.pallas_call(
        _norm_matmul_kernel,
        grid=(t // tm, n // tn),
        in_specs=[
            pl.BlockSpec((tm, d), lambda i, j: (i, 0)),
            pl.BlockSpec((1, d), lambda i, j: (0, 0)),
            pl.BlockSpec((d, tn), lambda i, j: (0, j)),
        ],
        out_specs=pl.BlockSpec((tm, tn), lambda i, j: (i, j)),
        out_shape=jax.ShapeDtypeStruct((t, n), out_dtype),
        scratch_shapes=[pltpu.VMEM((tm, d), BF16)],
        compiler_params=_cparams("parallel", "arbitrary"),
        name="norm_matmul",
    )(x, gain.reshape(1, d), w)


def _ffn_kernel(x_ref, g_ref, wg_ref, wu_ref, wo_ref, fg_ref, o_ref, xn_ref, acc_ref, *, final):
    j = pl.program_id(1)

    @pl.when(j == 0)
    def _():
        xn_ref[...] = _rms(x_ref[...], g_ref[...]).astype(BF16)
        acc_ref[...] = jnp.zeros_like(acc_ref)

    xn = xn_ref[...]
    g = _dot(xn, wg_ref[...])
    u = _dot(xn, wu_ref[...])
    a = (_silu(g) * u).astype(BF16)
    acc_ref[...] += _dot(a, wo_ref[...])

    @pl.when(j == pl.num_programs(1) - 1)
    def _():
        y = x_ref[...] + 0.5 * acc_ref[...]
        if final:
            y = _rms(y, fg_ref[...])
        o_ref[...] = y


def _ffn(x, gain, w_in, w_out, final_gain=None):
    t, d = x.shape
    ff = w_out.shape[0]
    tm = min(512, t)
    tf = 512
    nf = ff // tf
    final = final_gain is not None
    fg = (final_gain if final else gain).reshape(1, d)
    return pl.pallas_call(
        functools.partial(_ffn_kernel, final=final),
        grid=(t // tm, nf),
        in_specs=[
            pl.BlockSpec((tm, d), lambda i, j: (i, 0)),
            pl.BlockSpec((1, d), lambda i, j: (0, 0)),
            pl.BlockSpec((d, tf), lambda i, j: (0, j)),
            pl.BlockSpec((d, tf), lambda i, j: (0, j + nf)),
            pl.BlockSpec((tf, d), lambda i, j: (j, 0)),
            pl.BlockSpec((1, d), lambda i, j: (0, 0)),
        ],
        out_specs=pl.BlockSpec((tm, d), lambda i, j: (i, 0)),
        out_shape=jax.ShapeDtypeStruct((t, d), F32),
        scratch_shapes=[pltpu.VMEM((tm, d), BF16), pltpu.VMEM((tm, d), F32)],
        compiler_params=_cparams("parallel", "arbitrary"),
        name="ffn",
    )(x, gain.reshape(1, d), w_in, w_in, w_out, fg)


def _proj_res_kernel(*refs, n_lhs):
    x_ref = refs[0]
    a_refs = refs[1:1 + n_lhs]
    w_refs = refs[1 + n_lhs:1 + 2 * n_lhs]
    o_ref = refs[1 + 2 * n_lhs]
    y = x_ref[...]
    for a_ref, w_ref in zip(a_refs, w_refs):
        y = y + _dot(a_ref[...], w_ref[...])
    o_ref[...] = y


def _proj_res(x, lhs, ws):
    t, d = x.shape
    tm = min(512, t)
    tn = d // 2
    in_specs = [pl.BlockSpec((tm, tn), lambda i, j: (i, j))]
    in_specs += [pl.BlockSpec((tm, a.shape[1]), lambda i, j: (i, 0)) for a in lhs]
    in_specs += [pl.BlockSpec((w.shape[0], tn), lambda i, j: (0, j)) for w in ws]
    return pl.pallas_call(
        functools.partial(_proj_res_kernel, n_lhs=len(lhs)),
        grid=(t // tm, d // tn),
        in_specs=in_specs,
        out_specs=pl.BlockSpec((tm, tn), lambda i, j: (i, j)),
        out_shape=jax.ShapeDtypeStruct((t, d), F32),
        compiler_params=_cparams("parallel", "arbitrary"),
        name="proj_res",
    )(x, *lhs, *ws)


def _rotate_slab(x, cos, sin_signed, first_half):
    fwd = pltpu.roll(x, HEAD_DIM // 2, 1)
    bwd = pltpu.roll(x, LANES - HEAD_DIM // 2, 1)
    return x * cos + jnp.where(first_half, bwd, fwd) * sin_signed


def _dsa_prep_kernel(q_ref, k_ref, v_ref, qi_ref, ps_ref, cos_ref, sin_ref,
                     qo_ref, ko_ref, vo_ref, qio_ref, kio_ref, *, q_scale):
    cos = cos_ref[...]
    sin = sin_ref[...]
    lane = lax.broadcasted_iota(I32, cos.shape, 1)
    first_half = (lane % HEAD_DIM) < HEAD_DIM // 2
    n_slabs = q_ref.shape[1] // LANES
    for c in range(n_slabs):
        sl = slice(c * LANES, (c + 1) * LANES)
        qo_ref[:, sl] = (_rotate_slab(q_ref[:, sl], cos, sin, first_half) * q_scale).astype(BF16)
        ko_ref[:, sl] = _rotate_slab(k_ref[:, sl], cos, sin, first_half).astype(BF16)
        qio_ref[:, sl] = _rotate_slab(qi_ref[:, sl], cos, sin, first_half).astype(BF16)
    vo_ref[...] = v_ref[...].astype(BF16)
    ki = _rotate_slab(ps_ref[...], cos, sin, first_half)
    ki_dup = jnp.where(lane < HEAD_DIM, ki, pltpu.roll(ki, HEAD_DIM, 1))
    kio_ref[...] = ki_dup.astype(BF16)


def _dsa_prep(pm, ps, cos, sin, seq, hw, col_q):
    t = pm.shape[0]
    tm = min(512, seq)
    npos = seq // tm

    def col(c):
        return pl.BlockSpec((tm, hw), lambda i: (i, c))

    pos_spec = pl.BlockSpec((tm, LANES), lambda i: (i % npos, 0))
    row_hw = pl.BlockSpec((tm, hw), lambda i: (i, 0))
    row_l = pl.BlockSpec((tm, LANES), lambda i: (i, 0))
    shp = jax.ShapeDtypeStruct((t, hw), BF16)
    return pl.pallas_call(
        functools.partial(_dsa_prep_kernel, q_scale=HEAD_DIM ** -0.5),
        grid=(t // tm,),
        in_specs=[col(col_q), col(col_q + 1), col(col_q + 2), col(col_q + 3), row_l, pos_spec, pos_spec],
        out_specs=[row_hw, row_hw, row_hw, row_hw, row_l],
        out_shape=[shp, shp, shp, shp, jax.ShapeDtypeStruct((t, LANES), BF16)],
        compiler_params=_cparams("parallel"),
        name="dsa_prep",
    )(pm, pm, pm, pm, ps, cos, sin)


def _ssd_kernel(z_ref, xr_ref, br_ref, cr_ref, ps_ref,
                wx_ref, wb_ref, wc_ref, bx_ref, bb_ref, bc_ref,
                dtb_ref, alog_ref, dsk_ref, gn_ref,
                ex_ref, lt_ref, sel0_ref, sel1_ref, up0_ref, up1_ref,
                o_ref,
                extx_ref, extb_ref, extc_ref, xs_ref, bm_ref, cm_ref, dt_ref, y_ref, st_ref,
                *, rows, n_pairs_per_group):
    r = pl.program_id(1)
    d = z_ref.shape[1]
    gw = SSD_GROUPS * SSD_STATE
    hist = SUBLANES

    @pl.when(r == 0)
    def _():
        extx_ref[0:hist, :] = jnp.zeros((hist, d), F32)
        extb_ref[0:hist, :] = jnp.zeros((hist, gw), F32)
        extc_ref[0:hist, :] = jnp.zeros((hist, gw), F32)
        st_ref[...] = jnp.zeros_like(st_ref)

    @pl.when(r > 0)
    def _():
        extx_ref[0:hist, :] = extx_ref[rows:rows + hist, :]
        extb_ref[0:hist, :] = extb_ref[rows:rows + hist, :]
        extc_ref[0:hist, :] = extc_ref[rows:rows + hist, :]

    def conv_silu(raw_ref, ext_ref, w_ref, b_ref, dst_ref):
        ext_ref[hist:hist + rows, :] = raw_ref[...]
        acc = b_ref[...] + w_ref[SSD_CONV - 1:SSD_CONV, :] * ext_ref[hist:hist + rows, :]
        for back in range(1, SSD_CONV):
            tap = SSD_CONV - 1 - back
            acc = acc + w_ref[tap:tap + 1, :] * ext_ref[hist - back:hist - back + rows, :]
        dst_ref[...] = _silu(acc)

    conv_silu(xr_ref, extx_ref, wx_ref, bx_ref, xs_ref)
    conv_silu(br_ref, extb_ref, wb_ref, bb_ref, bm_ref)
    conv_silu(cr_ref, extc_ref, wc_ref, bc_ref, cm_ref)

    lane = lax.broadcasted_iota(I32, (1, LANES), 1)
    n_heads = d // HEAD_DIM
    head_lane = (lane >= DT_LANE) & (lane < DT_LANE + n_heads)
    dt_all = jax.nn.softplus(ps_ref[...] + dtb_ref[...])
    dt_ref[...] = jnp.where(head_lane, dt_all, 0.0)
    a_vec = jnp.where(head_lane, -jnp.exp(alog_ref[...]), 0.0)

    ex = ex_ref[...]
    lt = lt_ref[...]
    up0 = up0_ref[...]
    up1 = up1_ref[...]
    t_idx = lax.broadcasted_iota(I32, (CHUNK, LANES), 0)
    s_idx = lax.broadcasted_iota(I32, (CHUNK, LANES), 1)
    tril2 = t_idx >= (s_idx % HEAD_DIM)
    low_half = s_idx < HEAD_DIM
    low_half2 = lax.broadcasted_iota(I32, (LANES, LANES), 1) < HEAD_DIM
    top_rows = lax.broadcasted_iota(I32, (LANES, LANES), 0) < HEAD_DIM
    blockdiag = low_half2 == top_rows

    def chunk_body(c, carry):
        rs = pl.ds(pl.multiple_of(c * CHUNK, CHUNK), CHUNK)
        dt = dt_ref[rs, :]
        ac = dt * a_vec
        a_cum = _dot(lt, ac, precision=HIGHEST)
        act0 = _nt(sel0_ref[...], ac, precision=HIGHEST)
        act1 = _nt(sel1_ref[...], ac, precision=HIGHEST)
        a_cum_t = _dot(act0, up0, precision=HIGHEST) + _dot(act1, up1, precision=HIGHEST)
        col_all = _dot(a_cum, ex, precision=HIGHEST)
        dt_e = _dot(dt, ex, precision=HIGHEST)
        last_e = col_all[CHUNK - 1:CHUNK, :]
        exp_a = jnp.exp(col_all)
        to_end = jnp.exp(last_e - col_all)
        exp_end = jnp.exp(last_e)
        xdt = xs_ref[rs, :] * dt_e
        xdt_b = xdt.astype(BF16)
        xw_b = (xdt * to_end).astype(BF16)
        gwid = d // SSD_GROUPS
        for g in range(SSD_GROUPS):
            gs = slice(g * SSD_STATE, (g + 1) * SSD_STATE)
            gd = slice(g * gwid, (g + 1) * gwid)
            bm_b = bm_ref[rs, gs].astype(BF16)
            cm_b = cm_ref[rs, gs].astype(BF16)
            cb2 = _nt(cm_b, jnp.concatenate([bm_b, bm_b], axis=0))
            st = st_ref[g]
            y_off = _dot(cm_b, st.astype(BF16)) * exp_a[:, gd]
            for jj in range(n_pairs_per_group):
                pidx = g * n_pairs_per_group + jj
                sl = slice(pidx * LANES, (pidx + 1) * LANES)
                seg = col_all[:, sl] - a_cum_t[pidx:pidx + 1, :]
                lmat = (cb2 * jnp.exp(jnp.where(tril2, seg, -jnp.inf))).astype(BF16)
                xp = xdt_b[:, sl]
                rhs = jnp.where(blockdiag, jnp.concatenate([xp, xp], axis=0), jnp.zeros((), BF16))
                y_ref[rs, sl] = _dot(lmat, rhs) + y_off[:, jj * LANES:(jj + 1) * LANES]
            st_ref[g] = st * exp_end[:, gd] + _tn(bm_b, xw_b[:, gd])
        return carry

    lax.fori_loop(0, rows // CHUNK, chunk_body, 0)

    y = (y_ref[...] + xs_ref[...] * dsk_ref[...]) * _silu(z_ref[...])
    gwid = d // SSD_GROUPS
    for g in range(SSD_GROUPS):
        gd = slice(g * gwid, (g + 1) * gwid)
        o_ref[:, gd] = _rms(y[:, gd], gn_ref[:, gd]).astype(o_ref.dtype)


def _ssd_constants(d):
    n_heads = d // HEAD_DIM
    n_pairs = n_heads // 2
    pr = max(SUBLANES, n_pairs)
    ex = np.zeros((LANES, d), np.float32)
    for h in range(n_heads):
        ex[DT_LANE + h, h * HEAD_DIM:(h + 1) * HEAD_DIM] = 1.0
    lt = np.tril(np.ones((CHUNK, CHUNK), np.float32))
    up = np.triu(np.ones((CHUNK, CHUNK), np.float32))
    up0 = np.concatenate([up, np.zeros_like(up)], axis=1)
    up1 = np.concatenate([np.zeros_like(up), up], axis=1)
    sel0 = np.zeros((pr, LANES), np.float32)
    sel1 = np.zeros((pr, LANES), np.float32)
    for j in range(n_pairs):
        sel0[j, DT_LANE + 2 * j] = 1.0
        sel1[j, DT_LANE + 2 * j + 1] = 1.0
    return [jnp.asarray(a) for a in (ex, lt, sel0, sel1, up0, up1)]


def _ssd(pm, ps, conv_w, conv_b, dt_bias, a_log, d_skip, out_norm, batch, seq, d):
    t = pm.shape[0]
    rows = min(256, seq)
    nr = seq // rows
    gw = SSD_GROUPS * SSD_STATE
    n_heads = d // HEAD_DIM
    gwid = d // SSD_GROUPS

    def lane_vec(v):
        return jnp.zeros((1, LANES), F32).at[0, DT_LANE:DT_LANE + n_heads].set(v)

    consts = _ssd_constants(d)
    wx, wb, wc = conv_w[:, :d], conv_w[:, d:d + gw], conv_w[:, d + gw:]
    cb = conv_b.reshape(1, -1)
    bx, bb, bc = cb[:, :d], cb[:, d:d + gw], cb[:, d + gw:]
    dsk = jnp.repeat(d_skip, HEAD_DIM).reshape(1, d)

    def rowblk(width, colblk):
        return pl.BlockSpec((rows, width), lambda b, r: (b * nr + r, colblk))

    def full(a):
        return pl.BlockSpec(a.shape, lambda b, r: (0,) * a.ndim)

    small = [wx, wb, wc, bx, bb, bc, lane_vec(dt_bias), lane_vec(a_log), dsk, out_norm.reshape(1, d)] + consts
    return pl.pallas_call(
        functools.partial(_ssd_kernel, rows=rows, n_pairs_per_group=n_heads // SSD_GROUPS // 2),
        grid=(batch, nr),
        in_specs=[rowblk(d, 0), rowblk(d, 1), rowblk(gw, 2 * d // gw), rowblk(gw, 2 * d // gw + 1),
                  rowblk(LANES, 0)] + [full(a) for a in small],
        out_specs=rowblk(d, 0),
        out_shape=jax.ShapeDtypeStruct((t, d), BF16),
        scratch_shapes=[
            pltpu.VMEM((rows + SUBLANES, d), F32), pltpu.VMEM((rows + SUBLANES, gw), F32),
            pltpu.VMEM((rows + SUBLANES, gw), F32),
            pltpu.VMEM((rows, d), F32), pltpu.VMEM((rows, gw), F32), pltpu.VMEM((rows, gw), F32),
            pltpu.VMEM((rows, LANES), F32), pltpu.VMEM((rows, d), F32),
            pltpu.VMEM((SSD_GROUPS, SSD_STATE, gwid), F32),
        ],
        compiler_params=_cparams("arbitrary", "arbitrary"),
        name="ssd",
    )(pm, pm, pm, pm, ps, *small)


def _order_key(score):
    bits = lax.bitcast_convert_type(score, I32)
    return bits ^ ((bits >> 31) & jnp.int32(0x7FFFFFFF))


_NEG_INF_KEY = int(np.array(-np.inf, np.float32).view(np.int32)) ^ 0x7FFFFFFF


def _dsa_kernel(qi_s, kj_s, q_ref, qi_ref, ps_ref, ki_ref, k_ref, v_ref, o_ref,
                keys_ref, thr_ref, jcut_ref, qa_ref, qb_ref, qis_ref, wb_ref, m_ref, l_ref, acc_ref,
                *, topk, w_scale, tk, ts, idx_bits):
    step = pl.program_id(1)
    qblk = qi_s[step]
    kj = kj_s[step]
    start = qblk * DSA_QBLOCK
    hw = q_ref.shape[1]
    n_heads = hw // HEAD_DIM
    n_pairs = n_heads // 2
    lane_q = lax.broadcasted_iota(I32, (DSA_QBLOCK, LANES), 1)
    low = lane_q < HEAD_DIM

    @pl.when(kj == 0)
    def _():
        zero_b = jnp.zeros((), BF16)
        for p in range(n_pairs):
            sl = slice(p * LANES, (p + 1) * LANES)
            qs = q_ref[:, sl]
            qa_ref[:, sl] = jnp.where(low, qs, zero_b)
            qb_ref[:, sl] = jnp.where(low, zero_b, qs)
            qis = qi_ref[:, sl]
            qis_ref[(2 * p) * DSA_QBLOCK:(2 * p + 1) * DSA_QBLOCK, :] = jnp.where(low, qis, zero_b)
            qis_ref[(2 * p + 1) * DSA_QBLOCK:(2 * p + 2) * DSA_QBLOCK, :] = jnp.where(low, zero_b, qis)
        w = ps_ref[...] * w_scale
        for h in range(n_heads):
            wb_ref[h] = jnp.broadcast_to(w[:, WIDX_LANE + h:WIDX_LANE + h + 1], (DSA_QBLOCK, LANES))
        m_ref[...] = jnp.full(m_ref.shape, NEG_BIG, F32)
        l_ref[...] = jnp.zeros_like(l_ref)
        acc_ref[...] = jnp.zeros_like(acc_ref)

        n_tiles = (start + DSA_QBLOCK + ts - 1) // ts
        row = lax.broadcasted_iota(I32, (DSA_QBLOCK, ts), 0)
        kcol = lax.broadcasted_iota(I32, (DSA_QBLOCK, ts), 1)
        vis_end = start + CHUNK + CHUNK * (row // CHUNK)

        def score_body(t, carry):
            base = pl.multiple_of(t * ts, ts)
            kt = ki_ref[pl.ds(base, ts), :]
            slabs = [jnp.zeros((DSA_QBLOCK, LANES), F32) for _ in range(ts // LANES)]
            for h in range(n_heads):
                rel = _nt(qis_ref[h * DSA_QBLOCK:(h + 1) * DSA_QBLOCK, :], kt)
                wbh = wb_ref[h]
                for c in range(ts // LANES):
                    slabs[c] = slabs[c] + jnp.maximum(rel[:, c * LANES:(c + 1) * LANES], 0.0) * wbh
            sc = jnp.concatenate(slabs, axis=1)
            sc = jnp.where(sc == 0.0, 0.0, sc)
            sc = jnp.where(kcol + base < vis_end, sc, -jnp.inf)
            keys_ref[:, pl.ds(base, ts)] = _order_key(sc)
            return carry

        lax.fori_loop(0, n_tiles, score_body, 0)

        sign_bit = jnp.int32(-2 ** 31)

        def bisect_body(it, res):
            cand = res | jnp.left_shift(jnp.int32(1), 31 - it)
            cand_s = cand ^ sign_bit

            def count_body(t, cnt):
                base = pl.multiple_of(t * ts, ts)
                kt = keys_ref[:, pl.ds(base, ts)]
                for c in range(ts // LANES):
                    cnt = cnt + jnp.where(kt[:, c * LANES:(c + 1) * LANES] >= cand_s, 1.0, 0.0)
                return cnt

            cnt = lax.fori_loop(0, n_tiles, count_body, jnp.zeros((DSA_QBLOCK, LANES), F32))
            total = jnp.sum(cnt, axis=1, keepdims=True)
            return jnp.where(total >= float(topk), cand, res)

        res = lax.fori_loop(0, 32, bisect_body, jnp.zeros((DSA_QBLOCK, LANES), I32))
        thr = res ^ sign_bit
        thr_ref[...] = thr

        def tally_body(t, carry):
            gt, ge = carry
            base = pl.multiple_of(t * ts, ts)
            kt = keys_ref[:, pl.ds(base, ts)]
            for c in range(ts // LANES):
                ks = kt[:, c * LANES:(c + 1) * LANES]
                gt = gt + jnp.where(ks > thr, 1.0, 0.0)
                ge = ge + jnp.where(ks >= thr, 1.0, 0.0)
            return gt, ge

        zeros_f = jnp.zeros((DSA_QBLOCK, LANES), F32)
        gt, ge = lax.fori_loop(0, n_tiles, tally_body, (zeros_f, zeros_f))
        need = float(topk) - jnp.sum(gt, axis=1, keepdims=True)
        n_ge = jnp.sum(ge, axis=1, keepdims=True)
        jcut_ref[...] = jnp.full((DSA_QBLOCK, LANES), 2 ** idx_bits - 1, I32)

        @pl.when(jnp.max(n_ge) > float(topk))
        def _():
            def cut_body(it, cut):
                cand = cut | jnp.left_shift(jnp.int32(1), idx_bits - 1 - it)

                def count_body(t, cnt):
                    base = pl.multiple_of(t * ts, ts)
                    kt = keys_ref[:, pl.ds(base, ts)]
                    for c in range(ts // LANES):
                        idx = lane_q + (base + c * LANES)
                        hit = (kt[:, c * LANES:(c + 1) * LANES] == thr) & (idx < cand)
                        cnt = cnt + jnp.where(hit, 1.0, 0.0)
                    return cnt

                cnt = lax.fori_loop(0, n_tiles, count_body, zeros_f)
                return jnp.where(jnp.sum(cnt, axis=1, keepdims=True) <= need, cand, cut)

            jcut_ref[...] = lax.fori_loop(0, idx_bits, cut_body, jnp.zeros((DSA_QBLOCK, LANES), I32))

    kbase = pl.multiple_of(kj * tk, tk)
    key_t = keys_ref[:, pl.ds(kbase, tk)]
    thr = thr_ref[...]
    reps = tk // LANES
    thr_t = jnp.concatenate([thr] * reps, axis=1)
    cut_t = jnp.concatenate([jcut_ref[...]] * reps, axis=1)
    kidx = lax.broadcasted_iota(I32, (DSA_QBLOCK, tk), 1) + kbase
    mask = ((key_t > thr_t) | ((key_t == thr_t) & (kidx < cut_t))) & (key_t > jnp.int32(_NEG_INF_KEY))

    for p in range(n_pairs):
        sl = slice(p * LANES, (p + 1) * LANES)
        kp = k_ref[:, sl]
        vp = v_ref[:, sl]
        upd = []
        for i, qm_ref in enumerate((qa_ref, qb_ref)):
            h = 2 * p + i
            s = jnp.where(mask, _nt(qm_ref[:, sl], kp), NEG_BIG)
            m_old = m_ref[h][:, 0:1]
            m_new = jnp.maximum(m_old, jnp.max(s, axis=1, keepdims=True))
            alpha = jnp.exp(m_old - m_new)
            pe = jnp.exp(s - m_new)
            l_new = alpha * l_ref[h][:, 0:1] + jnp.sum(pe, axis=1, keepdims=True)
            m_ref[h] = jnp.broadcast_to(m_new, (DSA_QBLOCK, LANES))
            l_ref[h] = jnp.broadcast_to(l_new, (DSA_QBLOCK, LANES))
            upd.append(alpha * acc_ref[:, sl] + _dot(pe.astype(BF16), vp))
        acc_ref[:, sl] = jnp.where(low, upd[0], upd[1])

    last = (start + DSA_QBLOCK - 1) // tk

    @pl.when(kj == last)
    def _():
        for p in range(n_pairs):
            sl = slice(p * LANES, (p + 1) * LANES)
            denom = jnp.where(low, l_ref[2 * p], l_ref[2 * p + 1])
            o_ref[:, sl] = (acc_ref[:, sl] / denom).astype(o_ref.dtype)


def _dsa(q, k, v, qi, ki, ps, batch, seq, n_idx_heads):
    t, hw = q.shape
    nq = seq // DSA_QBLOCK
    tk = min(512, seq)
    ts = min(512, seq)
    nkt = seq // tk
    topk = min(DSA_TOPK_MAX, seq // 4)
    n_heads = hw // HEAD_DIM
    qi_list, kj_list = [], []
    for i in range(nq):
        for j in range((i * DSA_QBLOCK + DSA_QBLOCK - 1) // tk + 1):
            qi_list.append(i)
            kj_list.append(j)
    qi_arr = jnp.asarray(np.array(qi_list, np.int32))
    kj_arr = jnp.asarray(np.array(kj_list, np.int32))
    nsteps = len(qi_list)

    qspec = pl.BlockSpec((DSA_QBLOCK, hw), lambda b, s, qs, ks: (b * nq + qs[s], 0))
    kspec = pl.BlockSpec((tk, hw), lambda b, s, qs, ks: (b * nkt + ks[s], 0))
    grid_spec = pltpu.PrefetchScalarGridSpec(
        num_scalar_prefetch=2,
        grid=(batch, nsteps),
        in_specs=[
            qspec, qspec,
            pl.BlockSpec((DSA_QBLOCK, LANES), lambda b, s, qs, ks: (b * nq + qs[s], 0)),
            pl.BlockSpec((seq, LANES), lambda b, s, qs, ks: (b, 0)),
            kspec, kspec,
        ],
        out_specs=qspec,
        scratch_shapes=[
            pltpu.VMEM((DSA_QBLOCK, seq), I32),
            pltpu.VMEM((DSA_QBLOCK, LANES), I32),
            pltpu.VMEM((DSA_QBLOCK, LANES), I32),
            pltpu.VMEM((DSA_QBLOCK, hw), BF16),
            pltpu.VMEM((DSA_QBLOCK, hw), BF16),
            pltpu.VMEM((n_idx_heads * DSA_QBLOCK, LANES), BF16),
            pltpu.VMEM((n_idx_heads, DSA_QBLOCK, LANES), F32),
            pltpu.VMEM((n_heads, DSA_QBLOCK, LANES), F32),
            pltpu.VMEM((n_heads, DSA_QBLOCK, LANES), F32),
            pltpu.VMEM((DSA_QBLOCK, hw), F32),
        ],
    )
    w_scale = float(n_idx_heads) ** -0.5 * float(HEAD_DIM) ** -0.5
    return pl.pallas_call(
        functools.partial(_dsa_kernel, topk=topk, w_scale=w_scale, tk=tk, ts=ts, idx_bits=seq.bit_length()),
        grid_spec=grid_spec,
        out_shape=jax.ShapeDtypeStruct((t, hw), BF16),
        compiler_params=_cparams("arbitrary", "arbitrary"),
        name="dsa",
    )(qi_arr, kj_arr, q, qi, ps, ki, k, v)


BAND_QROWS = 2 * CHUNK
BAND_KBLOCKS = (BAND_PREV * CHUNK) // BAND_QROWS + 1


def _band_kernel(*refs, q_scale):
    q_ref = refs[0]
    k_refs = refs[1:1 + BAND_KBLOCKS]
    v_refs = refs[1 + BAND_KBLOCKS:1 + 2 * BAND_KBLOCKS]
    bias_ref = refs[1 + 2 * BAND_KBLOCKS]
    o_ref = refs[2 + 2 * BAND_KBLOCKS]
    kbuf_ref, vbuf_ref = refs[3 + 2 * BAND_KBLOCKS:]
    i = pl.program_id(1)
    d = q_ref.shape[1]
    nkeys = BAND_KBLOCKS * BAND_QROWS
    for blk in range(BAND_KBLOCKS):
        kbuf_ref[blk * BAND_QROWS:(blk + 1) * BAND_QROWS, :] = k_refs[blk][...]
        vbuf_ref[blk * BAND_QROWS:(blk + 1) * BAND_QROWS, :] = v_refs[blk][...]
    kcol = lax.broadcasted_iota(I32, (BAND_QROWS, nkeys), 1)
    valid = kcol >= (BAND_KBLOCKS - 1 - i) * BAND_QROWS
    low = lax.broadcasted_iota(I32, (BAND_QROWS, LANES), 1) < HEAD_DIM
    zero_b = jnp.zeros((), BF16)
    for p in range(d // LANES):
        sl = slice(p * LANES, (p + 1) * LANES)
        qs = q_ref[:, sl] * q_scale
        kp = kbuf_ref[:, sl]
        vp = vbuf_ref[:, sl]
        outs = []
        for half in range(2):
            qm = jnp.where(low, qs, zero_b) if half == 0 else jnp.where(low, zero_b, qs)
            s = _nt(qm, kp) + bias_ref[2 * p + half]
            s = jnp.where(valid, s, -jnp.inf)
            e = jnp.exp(s - jnp.max(s, axis=1, keepdims=True))
            pn = (e / jnp.sum(e, axis=1, keepdims=True)).astype(BF16)
            outs.append(_dot(pn, vp))
        o_ref[:, sl] = jnp.where(low, outs[0], outs[1]).astype(o_ref.dtype)


def _band_bias(rel_bias):
    nkeys = BAND_KBLOCKS * BAND_QROWS
    tq = np.arange(BAND_QROWS)[:, None]
    kj = np.arange(nkeys)[None, :]
    dist = kj - (BAND_KBLOCKS - 1) * BAND_QROWS - tq
    idx = np.clip(dist, -REL_PAST, CHUNK - 1) + REL_PAST
    qc = tq // CHUNK
    kc = kj // CHUNK
    visible = (kc >= qc) & (kc <= qc + BAND_PREV)
    bias = rel_bias.astype(F32)[:, idx]
    return jnp.where(jnp.asarray(visible)[None], bias, -jnp.inf)


def _band(qkv, rel_bias, batch, seq, d):
    t = qkv.shape[0]
    nq = seq // BAND_QROWS
    n_heads = d // HEAD_DIM
    nkeys = BAND_KBLOCKS * BAND_QROWS
    bias = _band_bias(rel_bias)

    def kv_spec(blk, col):
        back = BAND_KBLOCKS - 1 - blk
        return pl.BlockSpec((BAND_QROWS, d), lambda b, i: (b * nq + jnp.maximum(i - back, 0), col))

    in_specs = [pl.BlockSpec((BAND_QROWS, d), lambda b, i: (b * nq + i, 0))]
    in_specs += [kv_spec(blk, 1) for blk in range(BAND_KBLOCKS)]
    in_specs += [kv_spec(blk, 2) for blk in range(BAND_KBLOCKS)]
    in_specs += [pl.BlockSpec((n_heads, BAND_QROWS, nkeys), lambda b, i: (0, 0, 0))]
    return pl.pallas_call(
        functools.partial(_band_kernel, q_scale=HEAD_DIM ** -0.5),
        grid=(batch, nq),
        in_specs=in_specs,
        out_specs=pl.BlockSpec((BAND_QROWS, d), lambda b, i: (b * nq + i, 0)),
        out_shape=jax.ShapeDtypeStruct((t, d), BF16),
        scratch_shapes=[pltpu.VMEM((nkeys, d), BF16), pltpu.VMEM((nkeys, d), BF16)],
        compiler_params=_cparams("parallel", "arbitrary"),
        name="band",
    )(*([qkv] * (1 + 2 * BAND_KBLOCKS)), bias)


def _rope_tables(seq):
    half = HEAD_DIM // 2
    inv = ROPE_THETA ** (-jnp.arange(half, dtype=F32) / half)
    ang = jnp.arange(seq, dtype=F32)[:, None] * inv[None, :]
    cos = jnp.cos(ang)
    sin = jnp.sin(ang)
    reps = LANES // HEAD_DIM
    cos_t = jnp.tile(jnp.concatenate([cos, cos], axis=1), (1, reps))
    sin_t = jnp.tile(jnp.concatenate([-sin, sin], axis=1), (1, reps))
    return cos_t, sin_t


def _ssd_dsa_mixer(x, norm, w_in, conv_w, conv_b, dt_bias, a_log, d_skip, out_norm, w_out, batch, seq):
    t, d = x.shape
    hw = d // 2
    n_ssd_heads = d // HEAD_DIM
    n_idx_heads = d // 128
    xbc = d + 2 * SSD_GROUPS * SSD_STATE
    sizes = (d, xbc, n_ssd_heads, hw, hw, hw, n_idx_heads * HEAD_DIM, HEAD_DIM, n_idx_heads)
    offs = np.concatenate([[0], np.cumsum(sizes)])
    seg = [w_in[:, offs[i]:offs[i + 1]] for i in range(len(sizes))]
    w_z, w_xbc, w_dt, w_q, w_k, w_v, w_qi, w_ki, w_wi = seg
    w_main = jnp.concatenate([w_z, w_xbc, w_q, w_k, w_v, w_qi], axis=1).astype(BF16)
    w_small = jnp.zeros((d, LANES), F32)
    w_small = w_small.at[:, KIDX_LANE:KIDX_LANE + HEAD_DIM].set(w_ki)
    w_small = w_small.at[:, DT_LANE:DT_LANE + n_ssd_heads].set(w_dt)
    w_small = w_small.at[:, WIDX_LANE:WIDX_LANE + n_idx_heads].set(w_wi)
    pm = _norm_matmul(x, norm, w_main, hw, F32)
    ps = _norm_matmul(x, norm, w_small.astype(BF16), LANES, F32)
    y = _ssd(pm, ps, conv_w, conv_b, dt_bias, a_log, d_skip, out_norm, batch, seq, d)
    cos_t, sin_t = _rope_tables(seq)
    col_q = (d + xbc) // hw
    q, k, v, qi, ki = _dsa_prep(pm, ps, cos_t, sin_t, seq, hw, col_q)
    o = _dsa(q, k, v, qi, ki, ps, batch, seq, n_idx_heads)
    w_out_b = w_out.astype(BF16)
    return _proj_res(x, [y, o], [w_out_b[:d], w_out_b[d:]])


def _band_mixer(x, norm, w_qkv, rel_bias, w_out, batch, seq):
    t, d = x.shape
    qkv = _norm_matmul(x, norm, w_qkv.astype(BF16), d // 2, BF16)
    o = _band(qkv, rel_bias, batch, seq, d)
    return _proj_res(x, [o], [w_out.astype(BF16)])


def kernel(x, ffn1_norm, ffn1_w_in, ffn1_w_out, mix_norm, ab_w_in, ssd_conv_w, ssd_conv_b, ssd_dt_bias, ssd_a_log, ssd_d_skip, ssd_out_norm, ab_w_out, c_w_qkv, c_rel_bias, c_w_out, ffn2_norm, ffn2_w_in, ffn2_w_out, final_norm):
    batch, seq, d = x.shape
    depth = ffn1_norm.shape[0]
    xf = x.reshape(batch * seq, d)
    for layer in range(depth):
        xf = _ffn(xf, ffn1_norm[layer], ffn1_w_in[layer].astype(BF16), ffn1_w_out[layer].astype(BF16))
        i = layer // 2
        if layer % 2 == 0:
            xf = _ssd_dsa_mixer(xf, mix_norm[layer], ab_w_in[i], ssd_conv_w[i], ssd_conv_b[i], ssd_dt_bias[i],
                                ssd_a_log[i], ssd_d_skip[i], ssd_out_norm[i], ab_w_out[i], batch, seq)
        else:
            xf = _band_mixer(xf, mix_norm[layer], c_w_qkv[i], c_rel_bias[i], c_w_out[i], batch, seq)
        fin = final_norm if layer == depth - 1 else None
        xf = _ffn(xf, ffn2_norm[layer], ffn2_w_in[layer].astype(BF16), ffn2_w_out[layer].astype(BF16), fin)
    return xf.reshape(batch, seq, d)
```

```python
import functools

import jax
import jax.numpy as jnp
import numpy as np
from jax import lax
from jax.experimental import pallas as pl
from jax.experimental.pallas import tpu as pltpu

F32 = jnp.float32
BF16 = jnp.bfloat16
I32 = jnp.int32

EPS = 1e-5
ROPE_THETA = 10000.0
CHUNK = 64
HEAD_DIM = 64
SSD_GROUPS = 4
SSD_STATE = 128
SSD_CONV = 4
DSA_TOPK_MAX = 256
DSA_QBLOCK = 128
BAND_PREV = 8
REL_PAST = 256

LANES = 128
SUBLANES = 8
VMEM_LIMIT_BYTES = 56 * 1024 * 1024

KIDX_LANE = 0
DT_LANE = 64
WIDX_LANE = 96

NEG_BIG = -1e30
HIGHEST = lax.Precision.HIGHEST


def _cparams(*sem):
    return pltpu.CompilerParams(dimension_semantics=sem, vmem_limit_bytes=VMEM_LIMIT_BYTES)


def _nt(a, b, **kw):
    return lax.dot_general(a, b, (((1,), (1,)), ((), ())), preferred_element_type=F32, **kw)


def _tn(a, b, **kw):
    return lax.dot_general(a, b, (((0,), (0,)), ((), ())), preferred_element_type=F32, **kw)


def _dot(a, b, **kw):
    return jnp.dot(a, b, preferred_element_type=F32, **kw)


def _rms(x, gain):
    ms = jnp.mean(x * x, axis=-1, keepdims=True)
    return x * lax.rsqrt(ms + EPS) * gain


def _silu(x):
    return x * jax.nn.sigmoid(x)


def _norm_matmul_kernel(x_ref, g_ref, w_ref, o_ref, xn_ref):
    @pl.when(pl.program_id(1) == 0)
    def _():
        xn_ref[...] = _rms(x_ref[...], g_ref[...]).astype(BF16)

    o_ref[...] = _dot(xn_ref[...], w_ref[...]).astype(o_ref.dtype)


def _norm_matmul(x, gain, w, tn, out_dtype):
    t, d = x.shape
    n = w.shape[1]
    tm = min(512, t)
    return pl.pallas_call(
        _norm_matmul_kernel,
        grid=(t // tm, n // tn),
        in_specs=[
            pl.BlockSpec((tm, d), lambda i, j: (i, 0)),
            pl.BlockSpec((1, d), lambda i, j: (0, 0)),
            pl.BlockSpec((d, tn), lambda i, j: (0, j)),
        ],
        out_specs=pl.BlockSpec((tm, tn), lambda i, j: (i, j)),
        out_shape=jax.ShapeDtypeStruct((t, n), out_dtype),
        scratch_shapes=[pltpu.VMEM((tm, d), BF16)],
        compiler_params=_cparams("parallel", "arbitrary"),
        name="norm_matmul",
    )(x, gain.reshape(1, d), w)


def _ffn_kernel(x_ref, g_ref, wg_ref, wu_ref, wo_ref, fg_ref, o_ref, xn_ref, acc_ref, *, final):
    j = pl.program_id(1)

    @pl.when(j == 0)
    def _():
        xn_ref[...] = _rms(x_ref[...], g_ref[...]).astype(BF16)
        acc_ref[...] = jnp.zeros_like(acc_ref)

    xn = xn_ref[...]
    g = _dot(xn, wg_ref[...])
    u = _dot(xn, wu_ref[...])
    a = (_silu(g) * u).astype(BF16)
    acc_ref[...] += _dot(a, wo_ref[...])

    @pl.when(j == pl.num_programs(1) - 1)
    def _():
        y = x_ref[...] + 0.5 * acc_ref[...]
        if final:
            y = _rms(y, fg_ref[...])
        o_ref[...] = y


def _ffn(x, gain, w_in, w_out, final_gain=None):
    t, d = x.shape
    ff = w_out.shape[0]
    tm = min(512, t)
    tf = 512
    nf = ff // tf
    final = final_gain is not None
    fg = (final_gain if final else gain).reshape(1, d)
    return pl.pallas_call(
        functools.partial(_ffn_kernel, final=final),
        grid=(t // tm, nf),
        in_specs=[
            pl.BlockSpec((tm, d), lambda i, j: (i, 0)),
            pl.BlockSpec((1, d), lambda i, j: (0, 0)),
            pl.BlockSpec((d, tf), lambda i, j: (0, j)),
            pl.BlockSpec((d, tf), lambda i, j: (0, j + nf)),
            pl.BlockSpec((tf, d), lambda i, j: (j, 0)),
            pl.BlockSpec((1, d), lambda i, j: (0, 0)),
        ],
        out_specs=pl.BlockSpec((tm, d), lambda i, j: (i, 0)),
        out_shape=jax.ShapeDtypeStruct((t, d), F32),
        scratch_shapes=[pltpu.VMEM((tm, d), BF16), pltpu.VMEM((tm, d), F32)],
        compiler_params=_cparams("parallel", "arbitrary"),
        name="ffn",
    )(x, gain.reshape(1, d), w_in, w_in, w_out, fg)


def _proj_res_kernel(*refs, n_lhs):
    x_ref = refs[0]
    a_refs = refs[1:1 + n_lhs]
    w_refs = refs[1 + n_lhs:1 + 2 * n_lhs]
    o_ref = refs[1 + 2 * n_lhs]
    y = x_ref[...]
    for a_ref, w_ref in zip(a_refs, w_refs):
        y = y + _dot(a_ref[...], w_ref[...])
    o_ref[...] = y


def _proj_res(x, lhs, ws):
    t, d = x.shape
    tm = min(512, t)
    tn = d // 2
    in_specs = [pl.BlockSpec((tm, tn), lambda i, j: (i, j))]
    in_specs += [pl.BlockSpec((tm, a.shape[1]), lambda i, j: (i, 0)) for a in lhs]
    in_specs += [pl.BlockSpec((w.shape[0], tn), lambda i, j: (0, j)) for w in ws]
    return pl.pallas_call(
        functools.partial(_proj_res_kernel, n_lhs=len(lhs)),
        grid=(t // tm, d // tn),
        in_specs=in_specs,
        out_specs=pl.BlockSpec((tm, tn), lambda i, j: (i, j)),
        out_shape=jax.ShapeDtypeStruct((t, d), F32),
        compiler_params=_cparams("parallel", "arbitrary"),
        name="proj_res",
    )(x, *lhs, *ws)


def _rotate_slab(x, cos, sin_signed, first_half):
    fwd = pltpu.roll(x, HEAD_DIM // 2, 1)
    bwd = pltpu.roll(x, LANES - HEAD_DIM // 2, 1)
    return x * cos + jnp.where(first_half, bwd, fwd) * sin_signed


def _dsa_prep_kernel(q_ref, k_ref, v_ref, qi_ref, ps_ref, cos_ref, sin_ref,
                     qo_ref, ko_ref, vo_ref, qio_ref, kio_ref, *, q_scale):
    cos = cos_ref[...]
    sin = sin_ref[...]
    lane = lax.broadcasted_iota(I32, cos.shape, 1)
    first_half = (lane % HEAD_DIM) < HEAD_DIM // 2
    n_slabs = q_ref.shape[1] // LANES
    for c in range(n_slabs):
        sl = slice(c * LANES, (c + 1) * LANES)
        qo_ref[:, sl] = (_rotate_slab(q_ref[:, sl], cos, sin, first_half) * q_scale).astype(BF16)
        ko_ref[:, sl] = _rotate_slab(k_ref[:, sl], cos, sin, first_half).astype(BF16)
        qio_ref[:, sl] = _rotate_slab(qi_ref[:, sl], cos, sin, first_half).astype(BF16)
    vo_ref[...] = v_ref[...].astype(BF16)
    ki = _rotate_slab(ps_ref[...], cos, sin, first_half)
    ki_dup = jnp.where(lane < HEAD_DIM, ki, pltpu.roll(ki, HEAD_DIM, 1))
    kio_ref[...] = ki_dup.astype(BF16)


def _dsa_prep(pm, ps, cos, sin, seq, hw, col_q):
    t = pm.shape[0]
    tm = min(512, seq)
    npos = seq // tm

    def col(c):
        return pl.BlockSpec((tm, hw), lambda i: (i, c))

    pos_spec = pl.BlockSpec((tm, LANES), lambda i: (i % npos, 0))
    row_hw = pl.BlockSpec((tm, hw), lambda i: (i, 0))
    row_l = pl.BlockSpec((tm, LANES), lambda i: (i, 0))
    shp = jax.ShapeDtypeStruct((t, hw), BF16)
    return pl.pallas_call(
        functools.partial(_dsa_prep_kernel, q_scale=HEAD_DIM ** -0.5),
        grid=(t // tm,),
        in_specs=[col(col_q), col(col_q + 1), col(col_q + 2), col(col_q + 3), row_l, pos_spec, pos_spec],
        out_specs=[row_hw, row_hw, row_hw, row_hw, row_l],
        out_shape=[shp, shp, shp, shp, jax.ShapeDtypeStruct((t, LANES), BF16)],
        compiler_params=_cparams("parallel"),
        name="dsa_prep",
    )(pm, pm, pm, pm, ps, cos, sin)


def _ssd_kernel(z_ref, xr_ref, br_ref, cr_ref, ps_ref,
                wx_ref, wb_ref, wc_ref, bx_ref, bb_ref, bc_ref,
                dtb_ref, alog_ref, dsk_ref, gn_ref,
                ex_ref, lt_ref, sel0_ref, sel1_ref, up0_ref, up1_ref,
                o_ref,
                extx_ref, extb_ref, extc_ref, xs_ref, bm_ref, cm_ref, dt_ref, y_ref, st_ref,
                *, rows, n_pairs_per_group):
    r = pl.program_id(1)
    d = z_ref.shape[1]
    gw = SSD_GROUPS * SSD_STATE
    hist = SUBLANES

    @pl.when(r == 0)
    def _():
        extx_ref[0:hist, :] = jnp.zeros((hist, d), F32)
        extb_ref[0:hist, :] = jnp.zeros((hist, gw), F32)
        extc_ref[0:hist, :] = jnp.zeros((hist, gw), F32)
        st_ref[...] = jnp.zeros_like(st_ref)

    @pl.when(r > 0)
    def _():
        extx_ref[0:hist, :] = extx_ref[rows:rows + hist, :]
        extb_ref[0:hist, :] = extb_ref[rows:rows + hist, :]
        extc_ref[0:hist, :] = extc_ref[rows:rows + hist, :]

    def conv_silu(raw_ref, ext_ref, w_ref, b_ref, dst_ref):
        ext_ref[hist:hist + rows, :] = raw_ref[...]
        acc = b_ref[...] + w_ref[SSD_CONV - 1:SSD_CONV, :] * ext_ref[hist:hist + rows, :]
        for back in range(1, SSD_CONV):
            tap = SSD_CONV - 1 - back
            acc = acc + w_ref[tap:tap + 1, :] * ext_ref[hist - back:hist - back + rows, :]
        dst_ref[...] = _silu(acc)

    conv_silu(xr_ref, extx_ref, wx_ref, bx_ref, xs_ref)
    conv_silu(br_ref, extb_ref, wb_ref, bb_ref, bm_ref)
    conv_silu(cr_ref, extc_ref, wc_ref, bc_ref, cm_ref)

    lane = lax.broadcasted_iota(I32, (1, LANES), 1)
    n_heads = d // HEAD_DIM
    head_lane = (lane >= DT_LANE) & (lane < DT_LANE + n_heads)
    dt_all = jax.nn.softplus(ps_ref[...] + dtb_ref[...])
    dt_ref[...] = jnp.where(head_lane, dt_all, 0.0)
    a_vec = jnp.where(head_lane, -jnp.exp(alog_ref[...]), 0.0)

    ex = ex_ref[...]
    lt = lt_ref[...]
    up0 = up0_ref[...]
    up1 = up1_ref[...]
    t_idx = lax.broadcasted_iota(I32, (CHUNK, LANES), 0)
    s_idx = lax.broadcasted_iota(I32, (CHUNK, LANES), 1)
    tril2 = t_idx >= (s_idx % HEAD_DIM)
    low_half = s_idx < HEAD_DIM
    low_half2 = lax.broadcasted_iota(I32, (LANES, LANES), 1) < HEAD_DIM
    top_rows = lax.broadcasted_iota(I32, (LANES, LANES), 0) < HEAD_DIM
    blockdiag = low_half2 == top_rows

    def chunk_body(c, carry):
        rs = pl.ds(pl.multiple_of(c * CHUNK, CHUNK), CHUNK)
        dt = dt_ref[rs, :]
        ac = dt * a_vec
        a_cum = _dot(lt, ac, precision=HIGHEST)
        act0 = _nt(sel0_ref[...], ac, precision=HIGHEST)
        act1 = _nt(sel1_ref[...], ac, precision=HIGHEST)
        a_cum_t = _dot(act0, up0, precision=HIGHEST) + _dot(act1, up1, precision=HIGHEST)
        col_all = _dot(a_cum, ex, precision=HIGHEST)
        dt_e = _dot(dt, ex, precision=HIGHEST)
        last_e = col_all[CHUNK - 1:CHUNK, :]
        exp_a = jnp.exp(col_all)
        to_end = jnp.exp(last_e - col_all)
        exp_end = jnp.exp(last_e)
        xdt = xs_ref[rs, :] * dt_e
        xdt_b = xdt.astype(BF16)
        xw_b = (xdt * to_end).astype(BF16)
        gwid = d // SSD_GROUPS
        for g in range(SSD_GROUPS):
            gs = slice(g * SSD_STATE, (g + 1) * SSD_STATE)
            gd = slice(g * gwid, (g + 1) * gwid)
            bm_b = bm_ref[rs, gs].astype(BF16)
            cm_b = cm_ref[rs, gs].astype(BF16)
            cb2 = _nt(cm_b, jnp.concatenate([bm_b, bm_b], axis=0))
            st = st_ref[g]
            y_off = _dot(cm_b, st.astype(BF16)) * exp_a[:, gd]
            for jj in range(n_pairs_per_group):
                pidx = g * n_pairs_per_group + jj
                sl = slice(pidx * LANES, (pidx + 1) * LANES)
                seg = col_all[:, sl] - a_cum_t[pidx:pidx + 1, :]
                lmat = (cb2 * jnp.exp(jnp.where(tril2, seg, -jnp.inf))).astype(BF16)
                xp = xdt_b[:, sl]
                rhs = jnp.where(blockdiag, jnp.concatenate([xp, xp], axis=0), jnp.zeros((), BF16))
                y_ref[rs, sl] = _dot(lmat, rhs) + y_off[:, jj * LANES:(jj + 1) * LANES]
            st_ref[g] = st * exp_end[:, gd] + _tn(bm_b, xw_b[:, gd])
        return carry

    lax.fori_loop(0, rows // CHUNK, chunk_body, 0)

    y = (y_ref[...] + xs_ref[...] * dsk_ref[...]) * _silu(z_ref[...])
    gwid = d // SSD_GROUPS
    for g in range(SSD_GROUPS):
        gd = slice(g * gwid, (g + 1) * gwid)
        o_ref[:, gd] = _rms(y[:, gd], gn_ref[:, gd]).astype(o_ref.dtype)


def _ssd_constants(d):
    n_heads = d // HEAD_DIM
    n_pairs = n_heads // 2
    pr = max(SUBLANES, n_pairs)
    ex = np.zeros((LANES, d), np.float32)
    for h in range(n_heads):
        ex[DT_LANE + h, h * HEAD_DIM:(h + 1) * HEAD_DIM] = 1.0
    lt = np.tril(np.ones((CHUNK, CHUNK), np.float32))
    up = np.triu(np.ones((CHUNK, CHUNK), np.float32))
    up0 = np.concatenate([up, np.zeros_like(up)], axis=1)
    up1 = np.concatenate([np.zeros_like(up), up], axis=1)
    sel0 = np.zeros((pr, LANES), np.float32)
    sel1 = np.zeros((pr, LANES), np.float32)
    for j in range(n_pairs):
        sel0[j, DT_LANE + 2 * j] = 1.0
        sel1[j, DT_LANE + 2 * j + 1] = 1.0
    return [jnp.asarray(a) for a in (ex, lt, sel0, sel1, up0, up1)]


def _ssd(pm, ps, conv_w, conv_b, dt_bias, a_log, d_skip, out_norm, batch, seq, d):
    t = pm.shape[0]
    rows = min(256, seq)
    nr = seq // rows
    gw = SSD_GROUPS * SSD_STATE
    n_heads = d // HEAD_DIM
    gwid = d // SSD_GROUPS

    def lane_vec(v):
        return jnp.zeros((1, LANES), F32).at[0, DT_LANE:DT_LANE + n_heads].set(v)

    consts = _ssd_constants(d)
    wx, wb, wc = conv_w[:, :d], conv_w[:, d:d + gw], conv_w[:, d + gw:]
    cb = conv_b.reshape(1, -1)
    bx, bb, bc = cb[:, :d], cb[:, d:d + gw], cb[:, d + gw:]
    dsk = jnp.repeat(d_skip, HEAD_DIM).reshape(1, d)

    def rowblk(width, colblk):
        return pl.BlockSpec((rows, width), lambda b, r: (b * nr + r, colblk))

    def full(a):
        return pl.BlockSpec(a.shape, lambda b, r: (0,) * a.ndim)

    small = [wx, wb, wc, bx, bb, bc, lane_vec(dt_bias), lane_vec(a_log), dsk, out_norm.reshape(1, d)] + consts
    return pl.pallas_call(
        functools.partial(_ssd_kernel, rows=rows, n_pairs_per_group=n_heads // SSD_GROUPS // 2),
        grid=(batch, nr),
        in_specs=[rowblk(d, 0), rowblk(d, 1), rowblk(gw, 2 * d // gw), rowblk(gw, 2 * d // gw + 1),
                  rowblk(LANES, 0)] + [full(a) for a in small],
        out_specs=rowblk(d, 0),
        out_shape=jax.ShapeDtypeStruct((t, d), BF16),
        scratch_shapes=[
            pltpu.VMEM((rows + SUBLANES, d), F32), pltpu.VMEM((rows + SUBLANES, gw), F32),
            pltpu.VMEM((rows + SUBLANES, gw), F32),
            pltpu.VMEM((rows, d), F32), pltpu.VMEM((rows, gw), F32), pltpu.VMEM((rows, gw), F32),
            pltpu.VMEM((rows, LANES), F32), pltpu.VMEM((rows, d), F32),
            pltpu.VMEM((SSD_GROUPS, SSD_STATE, gwid), F32),
        ],
        compiler_params=_cparams("arbitrary", "arbitrary"),
        name="ssd",
    )(pm, pm, pm, pm, ps, *small)


def _order_key(score):
    bits = lax.bitcast_convert_type(score, I32)
    return bits ^ ((bits >> 31) & jnp.int32(0x7FFFFFFF))


_NEG_INF_KEY = int(np.array(-np.inf, np.float32).view(np.int32)) ^ 0x7FFFFFFF


def _dsa_kernel(qi_s, kj_s, q_ref, qi_ref, ps_ref, ki_ref, k_ref, v_ref, o_ref,
                keys_ref, thr_ref, jcut_ref, qst_ref, qis_ref, wb_ref, s_ref, m_ref, l_ref, acc_ref,
                *, topk, w_scale, tk, ts, idx_bits):
    step = pl.program_id(1)
    qblk = qi_s[step]
    kj = kj_s[step]
    start = qblk * DSA_QBLOCK
    hw = q_ref.shape[1]
    n_heads = hw // HEAD_DIM
    n_pairs = n_heads // 2
    lane_q = lax.broadcasted_iota(I32, (DSA_QBLOCK, LANES), 1)
    low = lane_q < HEAD_DIM

    @pl.when(kj == 0)
    def _():
        zero_b = jnp.zeros((), BF16)
        for p in range(n_pairs):
            sl = slice(p * LANES, (p + 1) * LANES)
            qs = q_ref[:, sl]
            qst_ref[p, 0:DSA_QBLOCK, :] = jnp.where(low, qs, zero_b)
            qst_ref[p, DSA_QBLOCK:2 * DSA_QBLOCK, :] = jnp.where(low, zero_b, qs)
            qis = qi_ref[:, sl]
            qis_ref[(2 * p) * DSA_QBLOCK:(2 * p + 1) * DSA_QBLOCK, :] = jnp.where(low, qis, zero_b)
            qis_ref[(2 * p + 1) * DSA_QBLOCK:(2 * p + 2) * DSA_QBLOCK, :] = jnp.where(low, zero_b, qis)
        w = ps_ref[...] * w_scale
        for h in range(n_heads):
            wb_ref[h] = jnp.broadcast_to(w[:, WIDX_LANE + h:WIDX_LANE + h + 1], (DSA_QBLOCK, LANES))
        m_ref[...] = jnp.full(m_ref.shape, NEG_BIG, F32)
        l_ref[...] = jnp.zeros_like(l_ref)
        acc_ref[...] = jnp.zeros_like(acc_ref)

        n_tiles = (start + DSA_QBLOCK + ts - 1) // ts
        row = lax.broadcasted_iota(I32, (DSA_QBLOCK, ts), 0)
        kcol = lax.broadcasted_iota(I32, (DSA_QBLOCK, ts), 1)
        vis_end = start + CHUNK + CHUNK * (row // CHUNK)

        def score_body(t, carry):
            base = pl.multiple_of(t * ts, ts)
            kt = ki_ref[pl.ds(base, ts), :]
            slabs = [jnp.zeros((DSA_QBLOCK, LANES), F32) for _ in range(ts // LANES)]
            for h in range(n_heads):
                rel = _nt(qis_ref[h * DSA_QBLOCK:(h + 1) * DSA_QBLOCK, :], kt)
                wbh = wb_ref[h]
                for c in range(ts // LANES):
                    slabs[c] = slabs[c] + jnp.maximum(rel[:, c * LANES:(c + 1) * LANES], 0.0) * wbh
            sc = jnp.concatenate(slabs, axis=1)
            sc = jnp.where(sc == 0.0, 0.0, sc)
            sc = jnp.where(kcol + base < vis_end, sc, -jnp.inf)
            keys_ref[:, pl.ds(base, ts)] = _order_key(sc)
            return carry

        lax.fori_loop(0, n_tiles, score_body, 0)

        sign_bit = jnp.int32(-2 ** 31)

        def bisect_body(it, res):
            cand = res | jnp.left_shift(jnp.int32(1), 31 - it)
            cand_s = cand ^ sign_bit

            def count_body(t, cnt):
                base = pl.multiple_of(t * ts, ts)
                kt = keys_ref[:, pl.ds(base, ts)]
                for c in range(ts // LANES):
                    cnt = cnt + jnp.where(kt[:, c * LANES:(c + 1) * LANES] >= cand_s, 1.0, 0.0)
                return cnt

            cnt = lax.fori_loop(0, n_tiles, count_body, jnp.zeros((DSA_QBLOCK, LANES), F32))
            total = jnp.sum(cnt, axis=1, keepdims=True)
            return jnp.where(total >= float(topk), cand, res)

        res = lax.fori_loop(0, 32, bisect_body, jnp.zeros((DSA_QBLOCK, LANES), I32))
        thr = res ^ sign_bit
        thr_ref[...] = thr

        def tally_body(t, carry):
            gt, ge = carry
            base = pl.multiple_of(t * ts, ts)
            kt = keys_ref[:, pl.ds(base, ts)]
            for c in range(ts // LANES):
                ks = kt[:, c * LANES:(c + 1) * LANES]
                gt = gt + jnp.where(ks > thr, 1.0, 0.0)
                ge = ge + jnp.where(ks >= thr, 1.0, 0.0)
            return gt, ge

        zeros_f = jnp.zeros((DSA_QBLOCK, LANES), F32)
        gt, ge = lax.fori_loop(0, n_tiles, tally_body, (zeros_f, zeros_f))
        need = float(topk) - jnp.sum(gt, axis=1, keepdims=True)
        n_ge = jnp.sum(ge, axis=1, keepdims=True)
        jcut_ref[...] = jnp.full((DSA_QBLOCK, LANES), 2 ** idx_bits - 1, I32)

        @pl.when(jnp.max(n_ge) > float(topk))
        def _():
            def cut_body(it, cut):
                cand = cut | jnp.left_shift(jnp.int32(1), idx_bits - 1 - it)

                def count_body(t, cnt):
                    base = pl.multiple_of(t * ts, ts)
                    kt = keys_ref[:, pl.ds(base, ts)]
                    for c in range(ts // LANES):
                        idx = lane_q + (base + c * LANES)
                        hit = (kt[:, c * LANES:(c + 1) * LANES] == thr) & (idx < cand)
                        cnt = cnt + jnp.where(hit, 1.0, 0.0)
                    return cnt

                cnt = lax.fori_loop(0, n_tiles, count_body, zeros_f)
                return jnp.where(jnp.sum(cnt, axis=1, keepdims=True) <= need, cand, cut)

            jcut_ref[...] = lax.fori_loop(0, idx_bits, cut_body, jnp.zeros((DSA_QBLOCK, LANES), I32))

    kbase = pl.multiple_of(kj * tk, tk)
    key_t = keys_ref[:, pl.ds(kbase, tk)]
    thr = thr_ref[...]
    reps = tk // LANES
    thr_t = jnp.concatenate([thr] * reps, axis=1)
    cut_t = jnp.concatenate([jcut_ref[...]] * reps, axis=1)
    kidx = lax.broadcasted_iota(I32, (DSA_QBLOCK, tk), 1) + kbase
    mask = ((key_t > thr_t) | ((key_t == thr_t) & (kidx < cut_t))) & (key_t > jnp.int32(_NEG_INF_KEY))

    for p in range(n_pairs):
        s_ref[p] = _nt(qst_ref[p], k_ref[:, p * LANES:(p + 1) * LANES])

    for p in range(n_pairs):
        pes, alphas = [], []
        for half in range(2):
            rows = slice(half * DSA_QBLOCK, (half + 1) * DSA_QBLOCK)
            s = jnp.where(mask, s_ref[p, rows, :], NEG_BIG)
            m_old = m_ref[p, rows, :]
            m_new = jnp.maximum(m_old, jnp.max(s, axis=1, keepdims=True))
            alpha = jnp.exp(m_old - m_new)
            pe = jnp.exp(s - jnp.concatenate([m_new] * reps, axis=1))
            m_ref[p, rows, :] = m_new
            l_ref[p, rows, :] = alpha * l_ref[p, rows, :] + jnp.sum(pe, axis=1, keepdims=True)
            pes.append(pe.astype(BF16))
            alphas.append(alpha)
        pv = _dot(jnp.concatenate(pes, axis=0), v_ref[:, p * LANES:(p + 1) * LANES])
        acc_ref[p] = jnp.concatenate(alphas, axis=0) * acc_ref[p] + pv

    last = (start + DSA_QBLOCK - 1) // tk

    @pl.when(kj == last)
    def _():
        for p in range(n_pairs):
            oa = acc_ref[p, 0:DSA_QBLOCK, :] / l_ref[p, 0:DSA_QBLOCK, :]
            ob = acc_ref[p, DSA_QBLOCK:2 * DSA_QBLOCK, :] / l_ref[p, DSA_QBLOCK:2 * DSA_QBLOCK, :]
            o_ref[:, p * LANES:(p + 1) * LANES] = jnp.where(low, oa, ob).astype(o_ref.dtype)


def _dsa(q, k, v, qi, ki, ps, batch, seq, n_idx_heads):
    t, hw = q.shape
    nq = seq // DSA_QBLOCK
    tk = min(512, seq)
    ts = min(512, seq)
    nkt = seq // tk
    topk = min(DSA_TOPK_MAX, seq // 4)
    n_heads = hw // HEAD_DIM
    qi_list, kj_list = [], []
    for i in range(nq):
        for j in range((i * DSA_QBLOCK + DSA_QBLOCK - 1) // tk + 1):
            qi_list.append(i)
            kj_list.append(j)
    qi_arr = jnp.asarray(np.array(qi_list, np.int32))
    kj_arr = jnp.asarray(np.array(kj_list, np.int32))
    nsteps = len(qi_list)

    qspec = pl.BlockSpec((DSA_QBLOCK, hw), lambda b, s, qs, ks: (b * nq + qs[s], 0))
    kspec = pl.BlockSpec((tk, hw), lambda b, s, qs, ks: (b * nkt + ks[s], 0))
    grid_spec = pltpu.PrefetchScalarGridSpec(
        num_scalar_prefetch=2,
        grid=(batch, nsteps),
        in_specs=[
            qspec, qspec,
            pl.BlockSpec((DSA_QBLOCK, LANES), lambda b, s, qs, ks: (b * nq + qs[s], 0)),
            pl.BlockSpec((seq, LANES), lambda b, s, qs, ks: (b, 0)),
            kspec, kspec,
        ],
        out_specs=qspec,
        scratch_shapes=[
            pltpu.VMEM((DSA_QBLOCK, seq), I32),
            pltpu.VMEM((DSA_QBLOCK, LANES), I32),
            pltpu.VMEM((DSA_QBLOCK, LANES), I32),
            pltpu.VMEM((n_heads // 2, 2 * DSA_QBLOCK, LANES), BF16),
            pltpu.VMEM((n_idx_heads * DSA_QBLOCK, LANES), BF16),
            pltpu.VMEM((n_idx_heads, DSA_QBLOCK, LANES), F32),
            pltpu.VMEM((n_heads // 2, 2 * DSA_QBLOCK, tk), F32),
            pltpu.VMEM((n_heads // 2, 2 * DSA_QBLOCK, LANES), F32),
            pltpu.VMEM((n_heads // 2, 2 * DSA_QBLOCK, LANES), F32),
            pltpu.VMEM((n_heads // 2, 2 * DSA_QBLOCK, LANES), F32),
        ],
    )
    w_scale = float(n_idx_heads) ** -0.5 * float(HEAD_DIM) ** -0.5
    return pl.pallas_call(
        functools.partial(_dsa_kernel, topk=topk, w_scale=w_scale, tk=tk, ts=ts, idx_bits=seq.bit_length()),
        grid_spec=grid_spec,
        out_shape=jax.ShapeDtypeStruct((t, hw), BF16),
        compiler_params=_cparams("arbitrary", "arbitrary"),
        name="dsa",
    )(qi_arr, kj_arr, q, qi, ps, ki, k, v)


BAND_QROWS = 2 * CHUNK
BAND_KBLOCKS = (BAND_PREV * CHUNK) // BAND_QROWS + 1


def _band_kernel(*refs, q_scale):
    q_ref = refs[0]
    k_refs = refs[1:1 + BAND_KBLOCKS]
    v_refs = refs[1 + BAND_KBLOCKS:1 + 2 * BAND_KBLOCKS]
    bias_ref = refs[1 + 2 * BAND_KBLOCKS]
    o_ref = refs[2 + 2 * BAND_KBLOCKS]
    kbuf_ref, vbuf_ref = refs[3 + 2 * BAND_KBLOCKS:]
    i = pl.program_id(1)
    d = q_ref.shape[1]
    nkeys = BAND_KBLOCKS * BAND_QROWS
    for blk in range(BAND_KBLOCKS):
        kbuf_ref[blk * BAND_QROWS:(blk + 1) * BAND_QROWS, :] = k_refs[blk][...]
        vbuf_ref[blk * BAND_QROWS:(blk + 1) * BAND_QROWS, :] = v_refs[blk][...]
    kcol = lax.broadcasted_iota(I32, (2 * BAND_QROWS, nkeys), 1)
    valid = kcol >= (BAND_KBLOCKS - 1 - i) * BAND_QROWS
    low = lax.broadcasted_iota(I32, (BAND_QROWS, LANES), 1) < HEAD_DIM
    zero_b = jnp.zeros((), BF16)
    for p in range(d // LANES):
        sl = slice(p * LANES, (p + 1) * LANES)
        qs = q_ref[:, sl] * q_scale
        qst = jnp.concatenate([jnp.where(low, qs, zero_b), jnp.where(low, zero_b, qs)], axis=0)
        s = _nt(qst, kbuf_ref[:, sl]) + bias_ref[p]
        s = jnp.where(valid, s, -jnp.inf)
        e = jnp.exp(s - jnp.max(s, axis=1, keepdims=True))
        denom = jnp.sum(e, axis=1, keepdims=True)
        pv = _dot(e.astype(BF16), vbuf_ref[:, sl]) / denom
        o_ref[:, sl] = jnp.where(low, pv[0:BAND_QROWS], pv[BAND_QROWS:2 * BAND_QROWS]).astype(o_ref.dtype)


def _band_bias(rel_bias):
    nkeys = BAND_KBLOCKS * BAND_QROWS
    tq = np.arange(BAND_QROWS)[:, None]
    kj = np.arange(nkeys)[None, :]
    dist = kj - (BAND_KBLOCKS - 1) * BAND_QROWS - tq
    idx = np.clip(dist, -REL_PAST, CHUNK - 1) + REL_PAST
    qc = tq // CHUNK
    kc = kj // CHUNK
    visible = (kc >= qc) & (kc <= qc + BAND_PREV)
    bias = rel_bias.astype(F32)[:, idx]
    return jnp.where(jnp.asarray(visible)[None], bias, -jnp.inf)


def _band(qkv, rel_bias, batch, seq, d):
    t = qkv.shape[0]
    nq = seq // BAND_QROWS
    n_heads = d // HEAD_DIM
    nkeys = BAND_KBLOCKS * BAND_QROWS
    bias = _band_bias(rel_bias).reshape(n_heads // 2, 2 * BAND_QROWS, nkeys)

    def kv_spec(blk, col):
        back = BAND_KBLOCKS - 1 - blk
        return pl.BlockSpec((BAND_QROWS, d), lambda b, i: (b * nq + jnp.maximum(i - back, 0), col))

    in_specs = [pl.BlockSpec((BAND_QROWS, d), lambda b, i: (b * nq + i, 0))]
    in_specs += [kv_spec(blk, 1) for blk in range(BAND_KBLOCKS)]
    in_specs += [kv_spec(blk, 2) for blk in range(BAND_KBLOCKS)]
    in_specs += [pl.BlockSpec((n_heads // 2, 2 * BAND_QROWS, nkeys), lambda b, i: (0, 0, 0))]
    return pl.pallas_call(
        functools.partial(_band_kernel, q_scale=HEAD_DIM ** -0.5),
        grid=(batch, nq),
        in_specs=in_specs,
        out_specs=pl.BlockSpec((BAND_QROWS, d), lambda b, i: (b * nq + i, 0)),
        out_shape=jax.ShapeDtypeStruct((t, d), BF16),
        scratch_shapes=[pltpu.VMEM((nkeys, d), BF16), pltpu.VMEM((nkeys, d), BF16)],
        compiler_params=_cparams("parallel", "arbitrary"),
        name="band",
    )(*([qkv] * (1 + 2 * BAND_KBLOCKS)), bias)


def _rope_tables(seq):
    half = HEAD_DIM // 2
    inv = ROPE_THETA ** (-jnp.arange(half, dtype=F32) / half)
    ang = jnp.arange(seq, dtype=F32)[:, None] * inv[None, :]
    cos = jnp.cos(ang)
    sin = jnp.sin(ang)
    reps = LANES // HEAD_DIM
    cos_t = jnp.tile(jnp.concatenate([cos, cos], axis=1), (1, reps))
    sin_t = jnp.tile(jnp.concatenate([-sin, sin], axis=1), (1, reps))
    return cos_t, sin_t


def _ssd_dsa_mixer(x, norm, w_in, conv_w, conv_b, dt_bias, a_log, d_skip, out_norm, w_out, batch, seq):
    t, d = x.shape
    hw = d // 2
    n_ssd_heads = d // HEAD_DIM
    n_idx_heads = d // 128
    xbc = d + 2 * SSD_GROUPS * SSD_STATE
    sizes = (d, xbc, n_ssd_heads, hw, hw, hw, n_idx_heads * HEAD_DIM, HEAD_DIM, n_idx_heads)
    offs = np.concatenate([[0], np.cumsum(sizes)])
    seg = [w_in[:, offs[i]:offs[i + 1]] for i in range(len(sizes))]
    w_z, w_xbc, w_dt, w_q, w_k, w_v, w_qi, w_ki, w_wi = seg
    w_main = jnp.concatenate([w_z, w_xbc, w_q, w_k, w_v, w_qi], axis=1).astype(BF16)
    w_small = jnp.zeros((d, LANES), F32)
    w_small = w_small.at[:, KIDX_LANE:KIDX_LANE + HEAD_DIM].set(w_ki)
    w_small = w_small.at[:, DT_LANE:DT_LANE + n_ssd_heads].set(w_dt)
    w_small = w_small.at[:, WIDX_LANE:WIDX_LANE + n_idx_heads].set(w_wi)
    pm = _norm_matmul(x, norm, w_main, hw, F32)
    ps = _norm_matmul(x, norm, w_small.astype(BF16), LANES, F32)
    y = _ssd(pm, ps, conv_w, conv_b, dt_bias, a_log, d_skip, out_norm, batch, seq, d)
    cos_t, sin_t = _rope_tables(seq)
    col_q = (d + xbc) // hw
    q, k, v, qi, ki = _dsa_prep(pm, ps, cos_t, sin_t, seq, hw, col_q)
    o = _dsa(q, k, v, qi, ki, ps, batch, seq, n_idx_heads)
    w_out_b = w_out.astype(BF16)
    return _proj_res(x, [y, o], [w_out_b[:d], w_out_b[d:]])


def _band_mixer(x, norm, w_qkv, rel_bias, w_out, batch, seq):
    t, d = x.shape
    qkv = _norm_matmul(x, norm, w_qkv.astype(BF16), d // 2, BF16)
    o = _band(qkv, rel_bias, batch, seq, d)
    return _proj_res(x, [o], [w_out.astype(BF16)])


def kernel(x, ffn1_norm, ffn1_w_in, ffn1_w_out, mix_norm, ab_w_in, ssd_conv_w, ssd_conv_b, ssd_dt_bias, ssd_a_log, ssd_d_skip, ssd_out_norm, ab_w_out, c_w_qkv, c_rel_bias, c_w_out, ffn2_norm, ffn2_w_in, ffn2_w_out, final_norm):
    batch, seq, d = x.shape
    depth = ffn1_norm.shape[0]
    xf = x.reshape(batch * seq, d)
    for layer in range(depth):
        xf = _ffn(xf, ffn1_norm[layer], ffn1_w_in[layer].astype(BF16), ffn1_w_out[layer].astype(BF16))
        i = layer // 2
        if layer % 2 == 0:
            xf = _ssd_dsa_mixer(xf, mix_norm[layer], ab_w_in[i], ssd_conv_w[i], ssd_conv_b[i], ssd_dt_bias[i],
                                ssd_a_log[i], ssd_d_skip[i], ssd_out_norm[i], ab_w_out[i], batch, seq)
        else:
            xf = _band_mixer(xf, mix_norm[layer], c_w_qkv[i], c_rel_bias[i], c_w_out[i], batch, seq)
        fin = final_norm if layer == depth - 1 else None
        xf = _ffn(xf, ffn2_norm[layer], ffn2_w_in[layer].astype(BF16), ffn2_w_out[layer].astype(BF16), fin)
    return xf.reshape(batch, seq, d)
```

```python
import functools

import jax
import jax.numpy as jnp
import numpy as np
from jax import lax
from jax.experimental import pallas as pl
from jax.experimental.pallas import tpu as pltpu

F32 = jnp.float32
BF16 = jnp.bfloat16
I32 = jnp.int32

EPS = 1e-5
ROPE_THETA = 10000.0
CHUNK = 64
HEAD_DIM = 64
SSD_GROUPS = 4
SSD_STATE = 128
SSD_CONV = 4
DSA_TOPK_MAX = 256
DSA_QBLOCK = 128
BAND_PREV = 8
REL_PAST = 256

LANES = 128
SUBLANES = 8
VMEM_LIMIT_BYTES = 56 * 1024 * 1024

KIDX_LANE = 0
DT_LANE = 64
WIDX_LANE = 96

NEG_BIG = -1e30
HIGHEST = lax.Precision.HIGHEST


def _cparams(*sem):
    return pltpu.CompilerParams(dimension_semantics=sem, vmem_limit_bytes=VMEM_LIMIT_BYTES)


def _nt(a, b, **kw):
    return lax.dot_general(a, b, (((1,), (1,)), ((), ())), preferred_element_type=F32, **kw)


def _tn(a, b, **kw):
    return lax.dot_general(a, b, (((0,), (0,)), ((), ())), preferred_element_type=F32, **kw)


def _dot(a, b, **kw):
    return jnp.dot(a, b, preferred_element_type=F32, **kw)


def _rms(x, gain):
    ms = jnp.mean(x * x, axis=-1, keepdims=True)
    return x * lax.rsqrt(ms + EPS) * gain


def _silu(x):
    return x * jax.nn.sigmoid(x)


def _norm_matmul_kernel(x_ref, g_ref, w_ref, o_ref, xn_ref):
    @pl.when(pl.program_id(1) == 0)
    def _():
        xn_ref[...] = _rms(x_ref[...], g_ref[...]).astype(BF16)

    o_ref[...] = _dot(xn_ref[...], w_ref[...]).astype(o_ref.dtype)


def _norm_matmul(x, gain, w, tn, out_dtype):
    t, d = x.shape
    n = w.shape[1]
    tm = min(512, t)
    return pl.pallas_call(
        _norm_matmul_kernel,
        grid=(t // tm, n // tn),
        in_specs=[
            pl.BlockSpec((tm, d), lambda i, j: (i, 0)),
            pl.BlockSpec((1, d), lambda i, j: (0, 0)),
            pl.BlockSpec((d, tn), lambda i, j: (0, j)),
        ],
        out_specs=pl.BlockSpec((tm, tn), lambda i, j: (i, j)),
        out_shape=jax.ShapeDtypeStruct((t, n), out_dtype),
        scratch_shapes=[pltpu.VMEM((tm, d), BF16)],
        compiler_params=_cparams("parallel", "arbitrary"),
        name="norm_matmul",
    )(x, gain.reshape(1, d), w)


def _ffn_kernel(x_ref, g_ref, wg_ref, wu_ref, wo_ref, fg_ref, o_ref, xn_ref, acc_ref, *, final):
    j = pl.program_id(1)

    @pl.when(j == 0)
    def _():
        xn_ref[...] = _rms(x_ref[...], g_ref[...]).astype(BF16)
        acc_ref[...] = jnp.zeros_like(acc_ref)

    xn = xn_ref[...]
    g = _dot(xn, wg_ref[...])
    u = _dot(xn, wu_ref[...])
    a = (_silu(g) * u).astype(BF16)
    acc_ref[...] += _dot(a, wo_ref[...])

    @pl.when(j == pl.num_programs(1) - 1)
    def _():
        y = x_ref[...] + 0.5 * acc_ref[...]
        if final:
            y = _rms(y, fg_ref[...])
        o_ref[...] = y


def _ffn(x, gain, w_in, w_out, final_gain=None):
    t, d = x.shape
    ff = w_out.shape[0]
    tm = min(512, t)
    tf = 512
    nf = ff // tf
    final = final_gain is not None
    fg = (final_gain if final else gain).reshape(1, d)
    return pl.pallas_call(
        functools.partial(_ffn_kernel, final=final),
        grid=(t // tm, nf),
        in_specs=[
            pl.BlockSpec((tm, d), lambda i, j: (i, 0)),
            pl.BlockSpec((1, d), lambda i, j: (0, 0)),
            pl.BlockSpec((d, tf), lambda i, j: (0, j)),
            pl.BlockSpec((d, tf), lambda i, j: (0, j + nf)),
            pl.BlockSpec((tf, d), lambda i, j: (j, 0)),
            pl.BlockSpec((1, d), lambda i, j: (0, 0)),
        ],
        out_specs=pl.BlockSpec((tm, d), lambda i, j: (i, 0)),
        out_shape=jax.ShapeDtypeStruct((t, d), F32),
        scratch_shapes=[pltpu.VMEM((tm, d), BF16), pltpu.VMEM((tm, d), F32)],
        compiler_params=_cparams("parallel", "arbitrary"),
        name="ffn",
    )(x, gain.reshape(1, d), w_in, w_in, w_out, fg)


def _proj_res_kernel(*refs, n_lhs):
    x_ref = refs[0]
    a_refs = refs[1:1 + n_lhs]
    w_refs = refs[1 + n_lhs:1 + 2 * n_lhs]
    o_ref = refs[1 + 2 * n_lhs]
    y = x_ref[...]
    for a_ref, w_ref in zip(a_refs, w_refs):
        y = y + _dot(a_ref[...], w_ref[...])
    o_ref[...] = y


def _proj_res(x, lhs, ws):
    t, d = x.shape
    tm = min(512, t)
    tn = d // 2
    in_specs = [pl.BlockSpec((tm, tn), lambda i, j: (i, j))]
    in_specs += [pl.BlockSpec((tm, a.shape[1]), lambda i, j: (i, 0)) for a in lhs]
    in_specs += [pl.BlockSpec((w.shape[0], tn), lambda i, j: (0, j)) for w in ws]
    return pl.pallas_call(
        functools.partial(_proj_res_kernel, n_lhs=len(lhs)),
        grid=(t // tm, d // tn),
        in_specs=in_specs,
        out_specs=pl.BlockSpec((tm, tn), lambda i, j: (i, j)),
        out_shape=jax.ShapeDtypeStruct((t, d), F32),
        compiler_params=_cparams("parallel", "arbitrary"),
        name="proj_res",
    )(x, *lhs, *ws)


def _rotate_slab(x, cos, sin_signed, first_half):
    fwd = pltpu.roll(x, HEAD_DIM // 2, 1)
    bwd = pltpu.roll(x, LANES - HEAD_DIM // 2, 1)
    return x * cos + jnp.where(first_half, bwd, fwd) * sin_signed


def _dsa_prep_kernel(q_ref, k_ref, v_ref, qi_ref, ps_ref, cos_ref, sin_ref,
                     qo_ref, ko_ref, vo_ref, qio_ref, kio_ref, *, q_scale):
    cos = cos_ref[...]
    sin = sin_ref[...]
    lane = lax.broadcasted_iota(I32, cos.shape, 1)
    first_half = (lane % HEAD_DIM) < HEAD_DIM // 2
    n_slabs = q_ref.shape[1] // LANES
    for c in range(n_slabs):
        sl = slice(c * LANES, (c + 1) * LANES)
        qo_ref[:, sl] = (_rotate_slab(q_ref[:, sl], cos, sin, first_half) * q_scale).astype(BF16)
        ko_ref[:, sl] = _rotate_slab(k_ref[:, sl], cos, sin, first_half).astype(BF16)
        qio_ref[:, sl] = _rotate_slab(qi_ref[:, sl], cos, sin, first_half).astype(BF16)
    vo_ref[...] = v_ref[...].astype(BF16)
    ki = _rotate_slab(ps_ref[...], cos, sin, first_half)
    ki_dup = jnp.where(lane < HEAD_DIM, ki, pltpu.roll(ki, HEAD_DIM, 1))
    kio_ref[...] = ki_dup.astype(BF16)


def _dsa_prep(pm, ps, cos, sin, seq, hw, col_q):
    t = pm.shape[0]
    tm = min(512, seq)
    npos = seq // tm

    def col(c):
        return pl.BlockSpec((tm, hw), lambda i: (i, c))

    pos_spec = pl.BlockSpec((tm, LANES), lambda i: (i % npos, 0))
    row_hw = pl.BlockSpec((tm, hw), lambda i: (i, 0))
    row_l = pl.BlockSpec((tm, LANES), lambda i: (i, 0))
    shp = jax.ShapeDtypeStruct((t, hw), BF16)
    return pl.pallas_call(
        functools.partial(_dsa_prep_kernel, q_scale=HEAD_DIM ** -0.5),
        grid=(t // tm,),
        in_specs=[col(col_q), col(col_q + 1), col(col_q + 2), col(col_q + 3), row_l, pos_spec, pos_spec],
        out_specs=[row_hw, row_hw, row_hw, row_hw, row_l],
        out_shape=[shp, shp, shp, shp, jax.ShapeDtypeStruct((t, LANES), BF16)],
        compiler_params=_cparams("parallel"),
        name="dsa_prep",
    )(pm, pm, pm, pm, ps, cos, sin)


def _ssd_kernel(z_ref, xr_ref, br_ref, cr_ref, ps_ref,
                wx_ref, wb_ref, wc_ref, bx_ref, bb_ref, bc_ref,
                dtb_ref, alog_ref, dsk_ref, gn_ref,
                ex_ref, lt_ref, sel0_ref, sel1_ref, up0_ref, up1_ref,
                o_ref,
                extx_ref, extb_ref, extc_ref, xs_ref, bm_ref, cm_ref, dt_ref, y_ref, st_ref,
                *, rows, n_pairs_per_group):
    r = pl.program_id(1)
    d = z_ref.shape[1]
    gw = SSD_GROUPS * SSD_STATE
    hist = SUBLANES

    @pl.when(r == 0)
    def _():
        extx_ref[0:hist, :] = jnp.zeros((hist, d), F32)
        extb_ref[0:hist, :] = jnp.zeros((hist, gw), F32)
        extc_ref[0:hist, :] = jnp.zeros((hist, gw), F32)
        st_ref[...] = jnp.zeros_like(st_ref)

    @pl.when(r > 0)
    def _():
        extx_ref[0:hist, :] = extx_ref[rows:rows + hist, :]
        extb_ref[0:hist, :] = extb_ref[rows:rows + hist, :]
        extc_ref[0:hist, :] = extc_ref[rows:rows + hist, :]

    def conv_silu(raw_ref, ext_ref, w_ref, b_ref, dst_ref):
        ext_ref[hist:hist + rows, :] = raw_ref[...]
        acc = b_ref[...] + w_ref[SSD_CONV - 1:SSD_CONV, :] * ext_ref[hist:hist + rows, :]
        for back in range(1, SSD_CONV):
            tap = SSD_CONV - 1 - back
            acc = acc + w_ref[tap:tap + 1, :] * ext_ref[hist - back:hist - back + rows, :]
        dst_ref[...] = _silu(acc)

    conv_silu(xr_ref, extx_ref, wx_ref, bx_ref, xs_ref)
    conv_silu(br_ref, extb_ref, wb_ref, bb_ref, bm_ref)
    conv_silu(cr_ref, extc_ref, wc_ref, bc_ref, cm_ref)

    lane = lax.broadcasted_iota(I32, (1, LANES), 1)
    n_heads = d // HEAD_DIM
    head_lane = (lane >= DT_LANE) & (lane < DT_LANE + n_heads)
    dt_all = jax.nn.softplus(ps_ref[...] + dtb_ref[...])
    dt_ref[...] = jnp.where(head_lane, dt_all, 0.0)
    a_vec = jnp.where(head_lane, -jnp.exp(alog_ref[...]), 0.0)

    ex = ex_ref[...]
    lt = lt_ref[...]
    up0 = up0_ref[...]
    up1 = up1_ref[...]
    t_idx = lax.broadcasted_iota(I32, (CHUNK, LANES), 0)
    s_idx = lax.broadcasted_iota(I32, (CHUNK, LANES), 1)
    tril2 = t_idx >= (s_idx % HEAD_DIM)
    low_half = s_idx < HEAD_DIM
    low_half2 = lax.broadcasted_iota(I32, (LANES, LANES), 1) < HEAD_DIM
    top_rows = lax.broadcasted_iota(I32, (LANES, LANES), 0) < HEAD_DIM
    blockdiag = low_half2 == top_rows

    def chunk_body(c, carry):
        rs = pl.ds(pl.multiple_of(c * CHUNK, CHUNK), CHUNK)
        dt = dt_ref[rs, :]
        ac = dt * a_vec
        a_cum = _dot(lt, ac, precision=HIGHEST)
        act0 = _nt(sel0_ref[...], ac, precision=HIGHEST)
        act1 = _nt(sel1_ref[...], ac, precision=HIGHEST)
        a_cum_t = _dot(act0, up0, precision=HIGHEST) + _dot(act1, up1, precision=HIGHEST)
        col_all = _dot(a_cum, ex, precision=HIGHEST)
        dt_e = _dot(dt, ex, precision=HIGHEST)
        last_e = col_all[CHUNK - 1:CHUNK, :]
        exp_a = jnp.exp(col_all)
        to_end = jnp.exp(last_e - col_all)
        exp_end = jnp.exp(last_e)
        xdt = xs_ref[rs, :] * dt_e
        xdt_b = xdt.astype(BF16)
        xw_b = (xdt * to_end).astype(BF16)
        gwid = d // SSD_GROUPS
        for g in range(SSD_GROUPS):
            gs = slice(g * SSD_STATE, (g + 1) * SSD_STATE)
            gd = slice(g * gwid, (g + 1) * gwid)
            bm_b = bm_ref[rs, gs].astype(BF16)
            cm_b = cm_ref[rs, gs].astype(BF16)
            cb2 = _nt(cm_b, jnp.concatenate([bm_b, bm_b], axis=0))
            st = st_ref[g]
            y_off = _dot(cm_b, st.astype(BF16)) * exp_a[:, gd]
            for jj in range(n_pairs_per_group):
                pidx = g * n_pairs_per_group + jj
                sl = slice(pidx * LANES, (pidx + 1) * LANES)
                seg = col_all[:, sl] - a_cum_t[pidx:pidx + 1, :]
                lmat = (cb2 * jnp.exp(jnp.where(tril2, seg, -jnp.inf))).astype(BF16)
                xp = xdt_b[:, sl]
                rhs = jnp.where(blockdiag, jnp.concatenate([xp, xp], axis=0), jnp.zeros((), BF16))
                y_ref[rs, sl] = _dot(lmat, rhs) + y_off[:, jj * LANES:(jj + 1) * LANES]
            st_ref[g] = st * exp_end[:, gd] + _tn(bm_b, xw_b[:, gd])
        return carry

    lax.fori_loop(0, rows // CHUNK, chunk_body, 0)

    y = (y_ref[...] + xs_ref[...] * dsk_ref[...]) * _silu(z_ref[...])
    gwid = d // SSD_GROUPS
    for g in range(SSD_GROUPS):
        gd = slice(g * gwid, (g + 1) * gwid)
        o_ref[:, gd] = _rms(y[:, gd], gn_ref[:, gd]).astype(o_ref.dtype)


def _ssd_constants(d):
    n_heads = d // HEAD_DIM
    n_pairs = n_heads // 2
    pr = max(SUBLANES, n_pairs)
    ex = np.zeros((LANES, d), np.float32)
    for h in range(n_heads):
        ex[DT_LANE + h, h * HEAD_DIM:(h + 1) * HEAD_DIM] = 1.0
    lt = np.tril(np.ones((CHUNK, CHUNK), np.float32))
    up = np.triu(np.ones((CHUNK, CHUNK), np.float32))
    up0 = np.concatenate([up, np.zeros_like(up)], axis=1)
    up1 = np.concatenate([np.zeros_like(up), up], axis=1)
    sel0 = np.zeros((pr, LANES), np.float32)
    sel1 = np.zeros((pr, LANES), np.float32)
    for j in range(n_pairs):
        sel0[j, DT_LANE + 2 * j] = 1.0
        sel1[j, DT_LANE + 2 * j + 1] = 1.0
    return [jnp.asarray(a) for a in (ex, lt, sel0, sel1, up0, up1)]


def _ssd(pm, ps, conv_w, conv_b, dt_bias, a_log, d_skip, out_norm, batch, seq, d):
    t = pm.shape[0]
    rows = min(256, seq)
    nr = seq // rows
    gw = SSD_GROUPS * SSD_STATE
    n_heads = d // HEAD_DIM
    gwid = d // SSD_GROUPS

    def lane_vec(v):
        return jnp.zeros((1, LANES), F32).at[0, DT_LANE:DT_LANE + n_heads].set(v)

    consts = _ssd_constants(d)
    wx, wb, wc = conv_w[:, :d], conv_w[:, d:d + gw], conv_w[:, d + gw:]
    cb = conv_b.reshape(1, -1)
    bx, bb, bc = cb[:, :d], cb[:, d:d + gw], cb[:, d + gw:]
    dsk = jnp.repeat(d_skip, HEAD_DIM).reshape(1, d)

    def rowblk(width, colblk):
        return pl.BlockSpec((rows, width), lambda b, r: (b * nr + r, colblk))

    def full(a):
        return pl.BlockSpec(a.shape, lambda b, r: (0,) * a.ndim)

    small = [wx, wb, wc, bx, bb, bc, lane_vec(dt_bias), lane_vec(a_log), dsk, out_norm.reshape(1, d)] + consts
    return pl.pallas_call(
        functools.partial(_ssd_kernel, rows=rows, n_pairs_per_group=n_heads // SSD_GROUPS // 2),
        grid=(batch, nr),
        in_specs=[rowblk(d, 0), rowblk(d, 1), rowblk(gw, 2 * d // gw), rowblk(gw, 2 * d // gw + 1),
                  rowblk(LANES, 0)] + [full(a) for a in small],
        out_specs=rowblk(d, 0),
        out_shape=jax.ShapeDtypeStruct((t, d), BF16),
        scratch_shapes=[
            pltpu.VMEM((rows + SUBLANES, d), F32), pltpu.VMEM((rows + SUBLANES, gw), F32),
            pltpu.VMEM((rows + SUBLANES, gw), F32),
            pltpu.VMEM((rows, d), F32), pltpu.VMEM((rows, gw), F32), pltpu.VMEM((rows, gw), F32),
            pltpu.VMEM((rows, LANES), F32), pltpu.VMEM((rows, d), F32),
            pltpu.VMEM((SSD_GROUPS, SSD_STATE, gwid), F32),
        ],
        compiler_params=_cparams("arbitrary", "arbitrary"),
        name="ssd",
    )(pm, pm, pm, pm, ps, *small)


def _order_key(score):
    bits = lax.bitcast_convert_type(score, I32)
    return bits ^ ((bits >> 31) & jnp.int32(0x7FFFFFFF))


_NEG_INF_KEY = int(np.array(-np.inf, np.float32).view(np.int32)) ^ 0x7FFFFFFF


def _dsa_kernel(qi_s, kj_s, q_ref, qi_ref, ps_ref, ki_ref, k_ref, v_ref, o_ref,
                keys_ref, thr_ref, jcut_ref, qst_ref, qis_ref, wb_ref, s_ref, m_ref, l_ref, acc_ref,
                *, topk, w_scale, tk, ts, idx_bits):
    step = pl.program_id(1)
    qblk = qi_s[step]
    kj = kj_s[step]
    start = qblk * DSA_QBLOCK
    hw = q_ref.shape[1]
    n_heads = hw // HEAD_DIM
    n_pairs = n_heads // 2
    lane_q = lax.broadcasted_iota(I32, (DSA_QBLOCK, LANES), 1)
    low = lane_q < HEAD_DIM

    @pl.when(kj == 0)
    def _():
        zero_b = jnp.zeros((), BF16)
        for p in range(n_pairs):
            sl = slice(p * LANES, (p + 1) * LANES)
            qs = q_ref[:, sl]
            qst_ref[p, 0:DSA_QBLOCK, :] = jnp.where(low, qs, zero_b)
            qst_ref[p, DSA_QBLOCK:2 * DSA_QBLOCK, :] = jnp.where(low, zero_b, qs)
            qis = qi_ref[:, sl]
            qis_ref[(2 * p) * DSA_QBLOCK:(2 * p + 1) * DSA_QBLOCK, :] = jnp.where(low, qis, zero_b)
            qis_ref[(2 * p + 1) * DSA_QBLOCK:(2 * p + 2) * DSA_QBLOCK, :] = jnp.where(low, zero_b, qis)
        w = ps_ref[...] * w_scale
        for h in range(n_heads):
            wb_ref[h] = jnp.broadcast_to(w[:, WIDX_LANE + h:WIDX_LANE + h + 1], (DSA_QBLOCK, LANES))
        m_ref[...] = jnp.full(m_ref.shape, NEG_BIG, F32)
        l_ref[...] = jnp.zeros_like(l_ref)
        acc_ref[...] = jnp.zeros_like(acc_ref)

        n_tiles = (start + DSA_QBLOCK + ts - 1) // ts
        row = lax.broadcasted_iota(I32, (DSA_QBLOCK, ts), 0)
        kcol = lax.broadcasted_iota(I32, (DSA_QBLOCK, ts), 1)
        vis_end = start + CHUNK + CHUNK * (row // CHUNK)

        def score_body(t, carry):
            base = pl.multiple_of(t * ts, ts)
            kt = ki_ref[pl.ds(base, ts), :]
            slabs = [jnp.zeros((DSA_QBLOCK, LANES), F32) for _ in range(ts // LANES)]
            for h in range(n_heads):
                rel = _nt(qis_ref[h * DSA_QBLOCK:(h + 1) * DSA_QBLOCK, :], kt)
                wbh = wb_ref[h]
                for c in range(ts // LANES):
                    slabs[c] = slabs[c] + jnp.maximum(rel[:, c * LANES:(c + 1) * LANES], 0.0) * wbh
            sc = jnp.concatenate(slabs, axis=1)
            sc = jnp.where(sc == 0.0, 0.0, sc)
            sc = jnp.where(kcol + base < vis_end, sc, -jnp.inf)
            keys_ref[:, pl.ds(base, ts)] = _order_key(sc)
            return carry

        lax.fori_loop(0, n_tiles, score_body, 0)

        def pad_body(t, carry):
            base = pl.multiple_of(t * ts, ts)
            keys_ref[:, pl.ds(base, ts)] = jnp.full((DSA_QBLOCK, ts), _NEG_INF_KEY, I32)
            return carry

        n_cover = ((start + DSA_QBLOCK + tk - 1) // tk) * (tk // ts)
        lax.fori_loop(n_tiles, n_cover, pad_body, 0)

        sign_bit = jnp.int32(-2 ** 31)

        def bisect_body(it, res):
            cand = res | jnp.left_shift(jnp.int32(1), 31 - it)
            cand_s = cand ^ sign_bit

            def count_body(t, cnt):
                base = pl.multiple_of(t * ts, ts)
                kt = keys_ref[:, pl.ds(base, ts)]
                for c in range(ts // LANES):
                    cnt = cnt + jnp.where(kt[:, c * LANES:(c + 1) * LANES] >= cand_s, 1.0, 0.0)
                return cnt

            cnt = lax.fori_loop(0, n_tiles, count_body, jnp.zeros((DSA_QBLOCK, LANES), F32))
            total = jnp.sum(cnt, axis=1, keepdims=True)
            return jnp.where(total >= float(topk), cand, res)

        res = lax.fori_loop(0, 32, bisect_body, jnp.zeros((DSA_QBLOCK, LANES), I32))
        thr = res ^ sign_bit
        thr_ref[...] = thr

        def tally_body(t, carry):
            gt, ge = carry
            base = pl.multiple_of(t * ts, ts)
            kt = keys_ref[:, pl.ds(base, ts)]
            for c in range(ts // LANES):
                ks = kt[:, c * LANES:(c + 1) * LANES]
                gt = gt + jnp.where(ks > thr, 1.0, 0.0)
                ge = ge + jnp.where(ks >= thr, 1.0, 0.0)
            return gt, ge

        zeros_f = jnp.zeros((DSA_QBLOCK, LANES), F32)
        gt, ge = lax.fori_loop(0, n_tiles, tally_body, (zeros_f, zeros_f))
        need = float(topk) - jnp.sum(gt, axis=1, keepdims=True)
        n_ge = jnp.sum(ge, axis=1, keepdims=True)
        jcut_ref[...] = jnp.full((DSA_QBLOCK, LANES), 2 ** idx_bits - 1, I32)

        @pl.when(jnp.max(n_ge) > float(topk))
        def _():
            def cut_body(it, cut):
                cand = cut | jnp.left_shift(jnp.int32(1), idx_bits - 1 - it)

                def count_body(t, cnt):
                    base = pl.multiple_of(t * ts, ts)
                    kt = keys_ref[:, pl.ds(base, ts)]
                    for c in range(ts // LANES):
                        idx = lane_q + (base + c * LANES)
                        hit = (kt[:, c * LANES:(c + 1) * LANES] == thr) & (idx < cand)
                        cnt = cnt + jnp.where(hit, 1.0, 0.0)
                    return cnt

                cnt = lax.fori_loop(0, n_tiles, count_body, zeros_f)
                return jnp.where(jnp.sum(cnt, axis=1, keepdims=True) <= need, cand, cut)

            jcut_ref[...] = lax.fori_loop(0, idx_bits, cut_body, jnp.zeros((DSA_QBLOCK, LANES), I32))

    kbase = pl.multiple_of(kj * tk, tk)
    key_t = keys_ref[:, pl.ds(kbase, tk)]
    thr = thr_ref[...]
    reps = tk // LANES
    thr_t = jnp.concatenate([thr] * reps, axis=1)
    cut_t = jnp.concatenate([jcut_ref[...]] * reps, axis=1)
    kidx = lax.broadcasted_iota(I32, (DSA_QBLOCK, tk), 1) + kbase
    mask = ((key_t > thr_t) | ((key_t == thr_t) & (kidx < cut_t))) & (key_t > jnp.int32(_NEG_INF_KEY))

    pen = jnp.where(mask, 0.0, NEG_BIG).astype(BF16)
    def qk(p):
        s_ref[p] = _nt(qst_ref[p], k_ref[:, p * LANES:(p + 1) * LANES]).astype(BF16)

    qk(0)

    def lane_slabs(a):
        return [a[:, c * LANES:(c + 1) * LANES] for c in range(reps)]

    def tree(parts, op):
        while len(parts) > 1:
            parts = [op(parts[i], parts[i + 1]) for i in range(0, len(parts) - 1, 2)] + parts[len(parts) & ~1:]
        return parts[0]

    for p in range(n_pairs):
        if p + 1 < n_pairs:
            qk(p + 1)
        pes, alphas = [], []
        for half in range(2):
            rows = slice(half * DSA_QBLOCK, (half + 1) * DSA_QBLOCK)
            s = s_ref[p, rows, :] + pen
            row_max = jnp.max(tree(lane_slabs(s), jnp.maximum).astype(F32), axis=1, keepdims=True)
            m_old = m_ref[p, rows, :]
            m_new = jnp.maximum(m_old, row_max)
            alpha = jnp.exp(m_old - m_new)
            pe = jnp.exp(s - jnp.concatenate([m_new.astype(BF16)] * reps, axis=1))
            row_sum = jnp.sum(tree(lane_slabs(pe), jnp.add).astype(F32), axis=1, keepdims=True)
            m_ref[p, rows, :] = m_new
            l_ref[p, rows, :] = alpha * l_ref[p, rows, :] + row_sum
            pes.append(pe)
            alphas.append(alpha)
        pv = _dot(jnp.concatenate(pes, axis=0), v_ref[:, p * LANES:(p + 1) * LANES])
        acc_ref[p] = jnp.concatenate(alphas, axis=0) * acc_ref[p] + pv

    last = (start + DSA_QBLOCK - 1) // tk

    @pl.when(kj == last)
    def _():
        for p in range(n_pairs):
            oa = acc_ref[p, 0:DSA_QBLOCK, :] / l_ref[p, 0:DSA_QBLOCK, :]
            ob = acc_ref[p, DSA_QBLOCK:2 * DSA_QBLOCK, :] / l_ref[p, DSA_QBLOCK:2 * DSA_QBLOCK, :]
            o_ref[:, p * LANES:(p + 1) * LANES] = jnp.where(low, oa, ob).astype(o_ref.dtype)


def _dsa(q, k, v, qi, ki, ps, batch, seq, n_idx_heads):
    t, hw = q.shape
    nq = seq // DSA_QBLOCK
    tk = min(1024, seq)
    ts = min(512, seq)
    nkt = seq // tk
    topk = min(DSA_TOPK_MAX, seq // 4)
    n_heads = hw // HEAD_DIM
    qi_list, kj_list = [], []
    for i in range(nq):
        for j in range((i * DSA_QBLOCK + DSA_QBLOCK - 1) // tk + 1):
            qi_list.append(i)
            kj_list.append(j)
    qi_arr = jnp.asarray(np.array(qi_list, np.int32))
    kj_arr = jnp.asarray(np.array(kj_list, np.int32))
    nsteps = len(qi_list)

    qspec = pl.BlockSpec((DSA_QBLOCK, hw), lambda b, s, qs, ks: (b * nq + qs[s], 0))
    kspec = pl.BlockSpec((tk, hw), lambda b, s, qs, ks: (b * nkt + ks[s], 0))
    grid_spec = pltpu.PrefetchScalarGridSpec(
        num_scalar_prefetch=2,
        grid=(batch, nsteps),
        in_specs=[
            qspec, qspec,
            pl.BlockSpec((DSA_QBLOCK, LANES), lambda b, s, qs, ks: (b * nq + qs[s], 0)),
            pl.BlockSpec((seq, LANES), lambda b, s, qs, ks: (b, 0)),
            kspec, kspec,
        ],
        out_specs=qspec,
        scratch_shapes=[
            pltpu.VMEM((DSA_QBLOCK, seq), I32),
            pltpu.VMEM((DSA_QBLOCK, LANES), I32),
            pltpu.VMEM((DSA_QBLOCK, LANES), I32),
            pltpu.VMEM((n_heads // 2, 2 * DSA_QBLOCK, LANES), BF16),
            pltpu.VMEM((n_idx_heads * DSA_QBLOCK, LANES), BF16),
            pltpu.VMEM((n_idx_heads, DSA_QBLOCK, LANES), F32),
            pltpu.VMEM((n_heads // 2, 2 * DSA_QBLOCK, tk), BF16),
            pltpu.VMEM((n_heads // 2, 2 * DSA_QBLOCK, LANES), F32),
            pltpu.VMEM((n_heads // 2, 2 * DSA_QBLOCK, LANES), F32),
            pltpu.VMEM((n_heads // 2, 2 * DSA_QBLOCK, LANES), F32),
        ],
    )
    w_scale = float(n_idx_heads) ** -0.5 * float(HEAD_DIM) ** -0.5
    return pl.pallas_call(
        functools.partial(_dsa_kernel, topk=topk, w_scale=w_scale, tk=tk, ts=ts, idx_bits=seq.bit_length()),
        grid_spec=grid_spec,
        out_shape=jax.ShapeDtypeStruct((t, hw), BF16),
        compiler_params=_cparams("arbitrary", "arbitrary"),
        name="dsa",
    )(qi_arr, kj_arr, q, qi, ps, ki, k, v)


BAND_QROWS = 2 * CHUNK
BAND_KBLOCKS = (BAND_PREV * CHUNK) // BAND_QROWS + 1


def _band_kernel(*refs, q_scale):
    q_ref = refs[0]
    k_refs = refs[1:1 + BAND_KBLOCKS]
    v_refs = refs[1 + BAND_KBLOCKS:1 + 2 * BAND_KBLOCKS]
    bias_ref = refs[1 + 2 * BAND_KBLOCKS]
    o_ref = refs[2 + 2 * BAND_KBLOCKS]
    kbuf_ref, vbuf_ref = refs[3 + 2 * BAND_KBLOCKS:]
    i = pl.program_id(1)
    d = q_ref.shape[1]
    nkeys = BAND_KBLOCKS * BAND_QROWS
    for blk in range(BAND_KBLOCKS):
        kbuf_ref[blk * BAND_QROWS:(blk + 1) * BAND_QROWS, :] = k_refs[blk][...]
        vbuf_ref[blk * BAND_QROWS:(blk + 1) * BAND_QROWS, :] = v_refs[blk][...]
    kcol = lax.broadcasted_iota(I32, (2 * BAND_QROWS, nkeys), 1)
    valid = kcol >= (BAND_KBLOCKS - 1 - i) * BAND_QROWS
    low = lax.broadcasted_iota(I32, (BAND_QROWS, LANES), 1) < HEAD_DIM
    zero_b = jnp.zeros((), BF16)
    for p in range(d // LANES):
        sl = slice(p * LANES, (p + 1) * LANES)
        qs = q_ref[:, sl] * q_scale
        qst = jnp.concatenate([jnp.where(low, qs, zero_b), jnp.where(low, zero_b, qs)], axis=0)
        s = _nt(qst, kbuf_ref[:, sl]) + bias_ref[p]
        s = jnp.where(valid, s, -jnp.inf)
        e = jnp.exp(s - jnp.max(s, axis=1, keepdims=True))
        denom = jnp.sum(e, axis=1, keepdims=True)
        pv = _dot(e.astype(BF16), vbuf_ref[:, sl]) / denom
        o_ref[:, sl] = jnp.where(low, pv[0:BAND_QROWS], pv[BAND_QROWS:2 * BAND_QROWS]).astype(o_ref.dtype)


def _band_bias(rel_bias):
    nkeys = BAND_KBLOCKS * BAND_QROWS
    tq = np.arange(BAND_QROWS)[:, None]
    kj = np.arange(nkeys)[None, :]
    dist = kj - (BAND_KBLOCKS - 1) * BAND_QROWS - tq
    idx = np.clip(dist, -REL_PAST, CHUNK - 1) + REL_PAST
    qc = tq // CHUNK
    kc = kj // CHUNK
    visible = (kc >= qc) & (kc <= qc + BAND_PREV)
    bias = rel_bias.astype(F32)[:, idx]
    return jnp.where(jnp.asarray(visible)[None], bias, -jnp.inf)


def _band(qkv, rel_bias, batch, seq, d):
    t = qkv.shape[0]
    nq = seq // BAND_QROWS
    n_heads = d // HEAD_DIM
    nkeys = BAND_KBLOCKS * BAND_QROWS
    bias = _band_bias(rel_bias).reshape(n_heads // 2, 2 * BAND_QROWS, nkeys)

    def kv_spec(blk, col):
        back = BAND_KBLOCKS - 1 - blk
        return pl.BlockSpec((BAND_QROWS, d), lambda b, i: (b * nq + jnp.maximum(i - back, 0), col))

    in_specs = [pl.BlockSpec((BAND_QROWS, d), lambda b, i: (b * nq + i, 0))]
    in_specs += [kv_spec(blk, 1) for blk in range(BAND_KBLOCKS)]
    in_specs += [kv_spec(blk, 2) for blk in range(BAND_KBLOCKS)]
    in_specs += [pl.BlockSpec((n_heads // 2, 2 * BAND_QROWS, nkeys), lambda b, i: (0, 0, 0))]
    return pl.pallas_call(
        functools.partial(_band_kernel, q_scale=HEAD_DIM ** -0.5),
        grid=(batch, nq),
        in_specs=in_specs,
        out_specs=pl.BlockSpec((BAND_QROWS, d), lambda b, i: (b * nq + i, 0)),
        out_shape=jax.ShapeDtypeStruct((t, d), BF16),
        scratch_shapes=[pltpu.VMEM((nkeys, d), BF16), pltpu.VMEM((nkeys, d), BF16)],
        compiler_params=_cparams("parallel", "arbitrary"),
        name="band",
    )(*([qkv] * (1 + 2 * BAND_KBLOCKS)), bias)


def _rope_tables(seq):
    half = HEAD_DIM // 2
    inv = ROPE_THETA ** (-jnp.arange(half, dtype=F32) / half)
    ang = jnp.arange(seq, dtype=F32)[:, None] * inv[None, :]
    cos = jnp.cos(ang)
    sin = jnp.sin(ang)
    reps = LANES // HEAD_DIM
    cos_t = jnp.tile(jnp.concatenate([cos, cos], axis=1), (1, reps))
    sin_t = jnp.tile(jnp.concatenate([-sin, sin], axis=1), (1, reps))
    return cos_t, sin_t


def _ssd_dsa_mixer(x, norm, w_in, conv_w, conv_b, dt_bias, a_log, d_skip, out_norm, w_out, batch, seq):
    t, d = x.shape
    hw = d // 2
    n_ssd_heads = d // HEAD_DIM
    n_idx_heads = d // 128
    xbc = d + 2 * SSD_GROUPS * SSD_STATE
    sizes = (d, xbc, n_ssd_heads, hw, hw, hw, n_idx_heads * HEAD_DIM, HEAD_DIM, n_idx_heads)
    offs = np.concatenate([[0], np.cumsum(sizes)])
    seg = [w_in[:, offs[i]:offs[i + 1]] for i in range(len(sizes))]
    w_z, w_xbc, w_dt, w_q, w_k, w_v, w_qi, w_ki, w_wi = seg
    w_main = jnp.concatenate([w_z, w_xbc, w_q, w_k, w_v, w_qi], axis=1).astype(BF16)
    w_small = jnp.zeros((d, LANES), F32)
    w_small = w_small.at[:, KIDX_LANE:KIDX_LANE + HEAD_DIM].set(w_ki)
    w_small = w_small.at[:, DT_LANE:DT_LANE + n_ssd_heads].set(w_dt)
    w_small = w_small.at[:, WIDX_LANE:WIDX_LANE + n_idx_heads].set(w_wi)
    pm = _norm_matmul(x, norm, w_main, hw, F32)
    ps = _norm_matmul(x, norm, w_small.astype(BF16), LANES, F32)
    y = _ssd(pm, ps, conv_w, conv_b, dt_bias, a_log, d_skip, out_norm, batch, seq, d)
    cos_t, sin_t = _rope_tables(seq)
    col_q = (d + xbc) // hw
    q, k, v, qi, ki = _dsa_prep(pm, ps, cos_t, sin_t, seq, hw, col_q)
    o = _dsa(q, k, v, qi, ki, ps, batch, seq, n_idx_heads)
    w_out_b = w_out.astype(BF16)
    return _proj_res(x, [y, o], [w_out_b[:d], w_out_b[d:]])


def _band_mixer(x, norm, w_qkv, rel_bias, w_out, batch, seq):
    t, d = x.shape
    qkv = _norm_matmul(x, norm, w_qkv.astype(BF16), d // 2, BF16)
    o = _band(qkv, rel_bias, batch, seq, d)
    return _proj_res(x, [o], [w_out.astype(BF16)])


def kernel(x, ffn1_norm, ffn1_w_in, ffn1_w_out, mix_norm, ab_w_in, ssd_conv_w, ssd_conv_b, ssd_dt_bias, ssd_a_log, ssd_d_skip, ssd_out_norm, ab_w_out, c_w_qkv, c_rel_bias, c_w_out, ffn2_norm, ffn2_w_in, ffn2_w_out, final_norm):
    batch, seq, d = x.shape
    depth = ffn1_norm.shape[0]
    xf = x.reshape(batch * seq, d)
    for layer in range(depth):
        xf = _ffn(xf, ffn1_norm[layer], ffn1_w_in[layer].astype(BF16), ffn1_w_out[layer].astype(BF16))
        i = layer // 2
        if layer % 2 == 0:
            xf = _ssd_dsa_mixer(xf, mix_norm[layer], ab_w_in[i], ssd_conv_w[i], ssd_conv_b[i], ssd_dt_bias[i],
                                ssd_a_log[i], ssd_d_skip[i], ssd_out_norm[i], ab_w_out[i], batch, seq)
        else:
            xf = _band_mixer(xf, mix_norm[layer], c_w_qkv[i], c_rel_bias[i], c_w_out[i], batch, seq)
        fin = final_norm if layer == depth - 1 else None
        xf = _ffn(xf, ffn2_norm[layer], ffn2_w_in[layer].astype(BF16), ffn2_w_out[layer].astype(BF16), fin)
    return xf.reshape(batch, seq, d)
```

```python
import functools

import jax
import jax.numpy as jnp
import numpy as np
from jax import lax
from jax.experimental import pallas as pl
from jax.experimental.pallas import tpu as pltpu

F32 = jnp.float32
BF16 = jnp.bfloat16
I32 = jnp.int32

EPS = 1e-5
ROPE_THETA = 10000.0
CHUNK = 64
HEAD_DIM = 64
SSD_GROUPS = 4
SSD_STATE = 128
SSD_CONV = 4
DSA_TOPK_MAX = 256
DSA_QBLOCK = 128
SEARCH_MAX_ITERS = 40
BAND_PREV = 8
REL_PAST = 256

LANES = 128
SUBLANES = 8
VMEM_LIMIT_BYTES = 56 * 1024 * 1024

KIDX_LANE = 0
DT_LANE = 64
WIDX_LANE = 96

NEG_BIG = -1e30
HIGHEST = lax.Precision.HIGHEST


def _cparams(*sem):
    return pltpu.CompilerParams(dimension_semantics=sem, vmem_limit_bytes=VMEM_LIMIT_BYTES)


def _nt(a, b, **kw):
    return lax.dot_general(a, b, (((1,), (1,)), ((), ())), preferred_element_type=F32, **kw)


def _tn(a, b, **kw):
    return lax.dot_general(a, b, (((0,), (0,)), ((), ())), preferred_element_type=F32, **kw)


def _dot(a, b, **kw):
    return jnp.dot(a, b, preferred_element_type=F32, **kw)


def _rms(x, gain):
    ms = jnp.mean(x * x, axis=-1, keepdims=True)
    return x * lax.rsqrt(ms + EPS) * gain


def _silu(x):
    return x * jax.nn.sigmoid(x)


def _tree(parts, op):
    while len(parts) > 1:
        parts = [op(parts[i], parts[i + 1]) for i in range(0, len(parts) - 1, 2)] + parts[len(parts) & ~1:]
    return parts[0]


def _norm_matmul_kernel(x_ref, g_ref, w_ref, o_ref, xn_ref):
    @pl.when(pl.program_id(1) == 0)
    def _():
        xn_ref[...] = _rms(x_ref[...], g_ref[...]).astype(BF16)

    o_ref[...] = _dot(xn_ref[...], w_ref[...]).astype(o_ref.dtype)


def _norm_matmul(x, gain, w, tn, out_dtype):
    t, d = x.shape
    n = w.shape[1]
    tm = min(1024, t)
    return pl.pallas_call(
        _norm_matmul_kernel,
        grid=(t // tm, n // tn),
        in_specs=[
            pl.BlockSpec((tm, d), lambda i, j: (i, 0)),
            pl.BlockSpec((1, d), lambda i, j: (0, 0)),
            pl.BlockSpec((d, tn), lambda i, j: (0, j)),
        ],
        out_specs=pl.BlockSpec((tm, tn), lambda i, j: (i, j)),
        out_shape=jax.ShapeDtypeStruct((t, n), out_dtype),
        scratch_shapes=[pltpu.VMEM((tm, d), BF16)],
        compiler_params=_cparams("parallel", "arbitrary"),
        name="norm_matmul",
    )(x, gain.reshape(1, d), w)


def _ffn_kernel(x_ref, g_ref, wg_ref, wu_ref, wo_ref, fg_ref, o_ref, xn_ref, *, final):
    j = pl.program_id(1)

    @pl.when(j == 0)
    def _():
        xn_ref[...] = _rms(x_ref[...], g_ref[...]).astype(BF16)
        o_ref[...] = jnp.zeros_like(o_ref)

    xn = xn_ref[...]
    g = _dot(xn, wg_ref[...])
    u = _dot(xn, wu_ref[...])
    a = (_silu(g) * u).astype(BF16)
    o_ref[...] += _dot(a, wo_ref[...])

    @pl.when(j == pl.num_programs(1) - 1)
    def _():
        y = x_ref[...] + 0.5 * o_ref[...]
        if final:
            y = _rms(y, fg_ref[...])
        o_ref[...] = y


def _ffn(x, gain, w_in, w_out, final_gain=None):
    t, d = x.shape
    ff = w_out.shape[0]
    tm = min(512, t)
    tf = 512
    nf = ff // tf
    final = final_gain is not None
    fg = (final_gain if final else gain).reshape(1, d)
    return pl.pallas_call(
        functools.partial(_ffn_kernel, final=final),
        grid=(t // tm, nf),
        in_specs=[
            pl.BlockSpec((tm, d), lambda i, j: (i, 0)),
            pl.BlockSpec((1, d), lambda i, j: (0, 0)),
            pl.BlockSpec((d, tf), lambda i, j: (0, j)),
            pl.BlockSpec((d, tf), lambda i, j: (0, j + nf)),
            pl.BlockSpec((tf, d), lambda i, j: (j, 0)),
            pl.BlockSpec((1, d), lambda i, j: (0, 0)),
        ],
        out_specs=pl.BlockSpec((tm, d), lambda i, j: (i, 0)),
        out_shape=jax.ShapeDtypeStruct((t, d), F32),
        scratch_shapes=[pltpu.VMEM((tm, d), BF16)],
        compiler_params=_cparams("parallel", "arbitrary"),
        name="ffn",
    )(x, gain.reshape(1, d), w_in, w_in, w_out, fg)


def _proj_res_kernel(*refs, n_lhs):
    x_ref = refs[0]
    a_refs = refs[1:1 + n_lhs]
    w_refs = refs[1 + n_lhs:1 + 2 * n_lhs]
    o_ref = refs[1 + 2 * n_lhs]
    y = x_ref[...]
    for a_ref, w_ref in zip(a_refs, w_refs):
        y = y + _dot(a_ref[...], w_ref[...])
    o_ref[...] = y


def _proj_res(x, lhs, ws):
    t, d = x.shape
    tm = min(512, t)
    tn = d // 2
    in_specs = [pl.BlockSpec((tm, tn), lambda i, j: (i, j))]
    in_specs += [pl.BlockSpec((tm, a.shape[1]), lambda i, j: (i, 0)) for a in lhs]
    in_specs += [pl.BlockSpec((w.shape[0], tn), lambda i, j: (0, j)) for w in ws]
    return pl.pallas_call(
        functools.partial(_proj_res_kernel, n_lhs=len(lhs)),
        grid=(t // tm, d // tn),
        in_specs=in_specs,
        out_specs=pl.BlockSpec((tm, tn), lambda i, j: (i, j)),
        out_shape=jax.ShapeDtypeStruct((t, d), F32),
        compiler_params=_cparams("parallel", "arbitrary"),
        name="proj_res",
    )(x, *lhs, *ws)


def _rotate_slab(x, cos, sin_signed, first_half):
    fwd = pltpu.roll(x, HEAD_DIM // 2, 1)
    bwd = pltpu.roll(x, LANES - HEAD_DIM // 2, 1)
    return x * cos + jnp.where(first_half, bwd, fwd) * sin_signed


def _dsa_prep_kernel(q_ref, k_ref, v_ref, qi_ref, ps_ref, cos_ref, sin_ref,
                     qo_ref, ko_ref, vo_ref, qio_ref, kio_ref, *, q_scale):
    cos = cos_ref[...]
    sin = sin_ref[...]
    lane = lax.broadcasted_iota(I32, cos.shape, 1)
    first_half = (lane % HEAD_DIM) < HEAD_DIM // 2
    n_slabs = q_ref.shape[1] // LANES
    for c in range(n_slabs):
        sl = slice(c * LANES, (c + 1) * LANES)
        qo_ref[:, sl] = (_rotate_slab(q_ref[:, sl], cos, sin, first_half) * q_scale).astype(BF16)
        ko_ref[:, sl] = _rotate_slab(k_ref[:, sl], cos, sin, first_half).astype(BF16)
        qio_ref[:, sl] = _rotate_slab(qi_ref[:, sl], cos, sin, first_half).astype(BF16)
    vo_ref[...] = v_ref[...].astype(BF16)
    ki = _rotate_slab(ps_ref[...], cos, sin, first_half)
    ki_dup = jnp.where(lane < HEAD_DIM, ki, pltpu.roll(ki, HEAD_DIM, 1))
    kio_ref[...] = ki_dup.astype(BF16)


def _dsa_prep(pm, ps, cos, sin, seq, hw, col_q):
    t = pm.shape[0]
    tm = min(512, seq)
    npos = seq // tm

    def col(c):
        return pl.BlockSpec((tm, hw), lambda i: (i, c))

    pos_spec = pl.BlockSpec((tm, LANES), lambda i: (i % npos, 0))
    row_hw = pl.BlockSpec((tm, hw), lambda i: (i, 0))
    row_l = pl.BlockSpec((tm, LANES), lambda i: (i, 0))
    shp = jax.ShapeDtypeStruct((t, hw), BF16)
    return pl.pallas_call(
        functools.partial(_dsa_prep_kernel, q_scale=HEAD_DIM ** -0.5),
        grid=(t // tm,),
        in_specs=[col(col_q), col(col_q + 1), col(col_q + 2), col(col_q + 3), row_l, pos_spec, pos_spec],
        out_specs=[row_hw, row_hw, row_hw, row_hw, row_l],
        out_shape=[shp, shp, shp, shp, jax.ShapeDtypeStruct((t, LANES), BF16)],
        compiler_params=_cparams("parallel"),
        name="dsa_prep",
    )(pm, pm, pm, pm, ps, cos, sin)


def _ssd_kernel(z_ref, xr_ref, br_ref, cr_ref, ps_ref,
                wx_ref, wb_ref, wc_ref, bx_ref, bb_ref, bc_ref,
                dtb_ref, alog_ref, dsk_ref, gn_ref,
                ex_ref, lt_ref, sel0_ref, sel1_ref, up0_ref, up1_ref,
                o_ref,
                extx_ref, extb_ref, extc_ref, xs_ref, bm_ref, cm_ref, dt_ref, y_ref, st_ref,
                *, rows, n_pairs_per_group):
    r = pl.program_id(1)
    d = z_ref.shape[1]
    gw = SSD_GROUPS * SSD_STATE
    hist = SUBLANES

    @pl.when(r == 0)
    def _():
        extx_ref[0:hist, :] = jnp.zeros((hist, d), F32)
        extb_ref[0:hist, :] = jnp.zeros((hist, gw), F32)
        extc_ref[0:hist, :] = jnp.zeros((hist, gw), F32)
        st_ref[...] = jnp.zeros_like(st_ref)

    @pl.when(r > 0)
    def _():
        extx_ref[0:hist, :] = extx_ref[rows:rows + hist, :]
        extb_ref[0:hist, :] = extb_ref[rows:rows + hist, :]
        extc_ref[0:hist, :] = extc_ref[rows:rows + hist, :]

    def conv_silu(raw_ref, ext_ref, w_ref, b_ref, dst_ref):
        ext_ref[hist:hist + rows, :] = raw_ref[...]
        acc = b_ref[...] + w_ref[SSD_CONV - 1:SSD_CONV, :] * ext_ref[hist:hist + rows, :]
        for back in range(1, SSD_CONV):
            tap = SSD_CONV - 1 - back
            acc = acc + w_ref[tap:tap + 1, :] * ext_ref[hist - back:hist - back + rows, :]
        dst_ref[...] = _silu(acc)

    conv_silu(xr_ref, extx_ref, wx_ref, bx_ref, xs_ref)
    conv_silu(br_ref, extb_ref, wb_ref, bb_ref, bm_ref)
    conv_silu(cr_ref, extc_ref, wc_ref, bc_ref, cm_ref)

    lane = lax.broadcasted_iota(I32, (1, LANES), 1)
    n_heads = d // HEAD_DIM
    head_lane = (lane >= DT_LANE) & (lane < DT_LANE + n_heads)
    dt_all = jax.nn.softplus(ps_ref[...] + dtb_ref[...])
    dt_ref[...] = jnp.where(head_lane, dt_all, 0.0)
    a_vec = jnp.where(head_lane, -jnp.exp(alog_ref[...]), 0.0)

    ex = ex_ref[...]
    lt = lt_ref[...]
    up0 = up0_ref[...]
    up1 = up1_ref[...]
    t_idx = lax.broadcasted_iota(I32, (CHUNK, LANES), 0)
    s_idx = lax.broadcasted_iota(I32, (CHUNK, LANES), 1)
    tril2 = t_idx >= (s_idx % HEAD_DIM)
    low_half = s_idx < HEAD_DIM
    low_half2 = lax.broadcasted_iota(I32, (LANES, LANES), 1) < HEAD_DIM
    top_rows = lax.broadcasted_iota(I32, (LANES, LANES), 0) < HEAD_DIM
    blockdiag = low_half2 == top_rows

    def chunk_body(c, carry):
        rs = pl.ds(pl.multiple_of(c * CHUNK, CHUNK), CHUNK)
        dt = dt_ref[rs, :]
        ac = dt * a_vec
        a_cum = _dot(lt, ac, precision=HIGHEST)
        act0 = _nt(sel0_ref[...], ac, precision=HIGHEST)
        act1 = _nt(sel1_ref[...], ac, precision=HIGHEST)
        a_cum_t = _dot(act0, up0, precision=HIGHEST) + _dot(act1, up1, precision=HIGHEST)
        both = jnp.concatenate([a_cum, dt], axis=0)
        hi = both.astype(BF16)
        rem = both - hi.astype(F32)
        mid = rem.astype(BF16)
        lo = (rem - mid.astype(F32)).astype(BF16)
        both_e = (_dot(hi, ex) + _dot(mid, ex)) + _dot(lo, ex)
        col_all = both_e[0:CHUNK, :]
        dt_e = both_e[CHUNK:2 * CHUNK, :]
        last_e = col_all[CHUNK - 1:CHUNK, :]
        exp_a = jnp.exp(col_all)
        to_end = jnp.exp(last_e - col_all)
        exp_end = jnp.exp(last_e)
        xdt = xs_ref[rs, :] * dt_e
        xdt_b = xdt.astype(BF16)
        xw_b = (xdt * to_end).astype(BF16)
        gwid = d // SSD_GROUPS
        for g in range(SSD_GROUPS):
            gs = slice(g * SSD_STATE, (g + 1) * SSD_STATE)
            gd = slice(g * gwid, (g + 1) * gwid)
            bm_b = bm_ref[rs, gs].astype(BF16)
            cm_b = cm_ref[rs, gs].astype(BF16)
            cb2 = _nt(cm_b, jnp.concatenate([bm_b, bm_b], axis=0))
            st = st_ref[g]
            y_off = _dot(cm_b, st.astype(BF16)) * exp_a[:, gd]
            for jj in range(n_pairs_per_group):
                pidx = g * n_pairs_per_group + jj
                sl = slice(pidx * LANES, (pidx + 1) * LANES)
                seg = col_all[:, sl] - a_cum_t[pidx:pidx + 1, :]
                lmat = (cb2 * jnp.exp(jnp.where(tril2, seg, -jnp.inf))).astype(BF16)
                xp = xdt_b[:, sl]
                rhs = jnp.where(blockdiag, jnp.concatenate([xp, xp], axis=0), jnp.zeros((), BF16))
                y_ref[rs, sl] = _dot(lmat, rhs) + y_off[:, jj * LANES:(jj + 1) * LANES]
            st_ref[g] = st * exp_end[:, gd] + _tn(bm_b, xw_b[:, gd])
        return carry

    lax.fori_loop(0, rows // CHUNK, chunk_body, 0)

    y = (y_ref[...] + xs_ref[...] * dsk_ref[...]) * _silu(z_ref[...])
    gwid = d // SSD_GROUPS
    for g in range(SSD_GROUPS):
        gd = slice(g * gwid, (g + 1) * gwid)
        o_ref[:, gd] = _rms(y[:, gd], gn_ref[:, gd]).astype(o_ref.dtype)


def _ssd_constants(d):
    n_heads = d // HEAD_DIM
    n_pairs = n_heads // 2
    pr = max(SUBLANES, n_pairs)
    ex = np.zeros((LANES, d), np.float32)
    for h in range(n_heads):
        ex[DT_LANE + h, h * HEAD_DIM:(h + 1) * HEAD_DIM] = 1.0
    lt = np.tril(np.ones((CHUNK, CHUNK), np.float32))
    up = np.triu(np.ones((CHUNK, CHUNK), np.float32))
    up0 = np.concatenate([up, np.zeros_like(up)], axis=1)
    up1 = np.concatenate([np.zeros_like(up), up], axis=1)
    sel0 = np.zeros((pr, LANES), np.float32)
    sel1 = np.zeros((pr, LANES), np.float32)
    for j in range(n_pairs):
        sel0[j, DT_LANE + 2 * j] = 1.0
        sel1[j, DT_LANE + 2 * j + 1] = 1.0
    return [jnp.asarray(ex).astype(BF16)] + [jnp.asarray(a) for a in (lt, sel0, sel1, up0, up1)]


def _ssd(pm, ps, conv_w, conv_b, dt_bias, a_log, d_skip, out_norm, batch, seq, d):
    t = pm.shape[0]
    rows = min(256, seq)
    nr = seq // rows
    gw = SSD_GROUPS * SSD_STATE
    n_heads = d // HEAD_DIM
    gwid = d // SSD_GROUPS

    def lane_vec(v):
        return jnp.zeros((1, LANES), F32).at[0, DT_LANE:DT_LANE + n_heads].set(v)

    consts = _ssd_constants(d)
    wx, wb, wc = conv_w[:, :d], conv_w[:, d:d + gw], conv_w[:, d + gw:]
    cb = conv_b.reshape(1, -1)
    bx, bb, bc = cb[:, :d], cb[:, d:d + gw], cb[:, d + gw:]
    dsk = jnp.repeat(d_skip, HEAD_DIM).reshape(1, d)

    def rowblk(width, colblk):
        return pl.BlockSpec((rows, width), lambda b, r: (b * nr + r, colblk))

    def full(a):
        return pl.BlockSpec(a.shape, lambda b, r: (0,) * a.ndim)

    small = [wx, wb, wc, bx, bb, bc, lane_vec(dt_bias), lane_vec(a_log), dsk, out_norm.reshape(1, d)] + consts
    return pl.pallas_call(
        functools.partial(_ssd_kernel, rows=rows, n_pairs_per_group=n_heads // SSD_GROUPS // 2),
        grid=(batch, nr),
        in_specs=[rowblk(d, 0), rowblk(d, 1), rowblk(gw, 2 * d // gw), rowblk(gw, 2 * d // gw + 1),
                  rowblk(LANES, 0)] + [full(a) for a in small],
        out_specs=rowblk(d, 0),
        out_shape=jax.ShapeDtypeStruct((t, d), BF16),
        scratch_shapes=[
            pltpu.VMEM((rows + SUBLANES, d), F32), pltpu.VMEM((rows + SUBLANES, gw), F32),
            pltpu.VMEM((rows + SUBLANES, gw), F32),
            pltpu.VMEM((rows, d), F32), pltpu.VMEM((rows, gw), F32), pltpu.VMEM((rows, gw), F32),
            pltpu.VMEM((rows, LANES), F32), pltpu.VMEM((rows, d), F32),
            pltpu.VMEM((SSD_GROUPS, SSD_STATE, gwid), F32),
        ],
        compiler_params=_cparams("arbitrary", "arbitrary"),
        name="ssd",
    )(pm, pm, pm, pm, ps, *small)


def _order_key(score):
    bits = lax.bitcast_convert_type(score, I32)
    return bits ^ ((bits >> 31) & jnp.int32(0x7FFFFFFF))


_NEG_INF_KEY = int(np.array(-np.inf, np.float32).view(np.int32)) ^ 0x7FFFFFFF


def _dsa_kernel(qi_s, kj_s, q_ref, qi_ref, ps_ref, ki_ref, k_ref, v_ref, o_ref,
                keys_ref, thr_ref, jcut_ref, qst_ref, qis_ref, wb_ref, s_ref, m_ref, l_ref, acc_ref,
                *, topk, w_scale, tk, ts, idx_bits):
    step = pl.program_id(1)
    qblk = qi_s[step]
    kj = kj_s[step]
    start = qblk * DSA_QBLOCK
    hw = q_ref.shape[1]
    n_heads = hw // HEAD_DIM
    n_pairs = n_heads // 2
    lane_q = lax.broadcasted_iota(I32, (DSA_QBLOCK, LANES), 1)
    low = lane_q < HEAD_DIM

    @pl.when(kj == 0)
    def _():
        zero_b = jnp.zeros((), BF16)
        for p in range(n_pairs):
            sl = slice(p * LANES, (p + 1) * LANES)
            qs = q_ref[:, sl]
            qst_ref[p, 0:DSA_QBLOCK, :] = jnp.where(low, qs, zero_b)
            qst_ref[p, DSA_QBLOCK:2 * DSA_QBLOCK, :] = jnp.where(low, zero_b, qs)
            qis = qi_ref[:, sl]
            qis_ref[(2 * p) * DSA_QBLOCK:(2 * p + 1) * DSA_QBLOCK, :] = jnp.where(low, qis, zero_b)
            qis_ref[(2 * p + 1) * DSA_QBLOCK:(2 * p + 2) * DSA_QBLOCK, :] = jnp.where(low, zero_b, qis)
        w = ps_ref[...] * w_scale
        for h in range(n_heads):
            wb_ref[h] = jnp.broadcast_to(w[:, WIDX_LANE + h:WIDX_LANE + h + 1], (DSA_QBLOCK, LANES))
        m_ref[...] = jnp.full(m_ref.shape, NEG_BIG, F32)
        l_ref[...] = jnp.zeros_like(l_ref)
        acc_ref[...] = jnp.zeros_like(acc_ref)

        n_tiles = (start + DSA_QBLOCK + ts - 1) // ts
        row = lax.broadcasted_iota(I32, (DSA_QBLOCK, ts), 0)
        kcol = lax.broadcasted_iota(I32, (DSA_QBLOCK, ts), 1)
        vis_end = start + CHUNK + CHUNK * (row // CHUNK)

        def score_body(t, carry):
            mx, mn = carry
            base = pl.multiple_of(t * ts, ts)
            kt = ki_ref[pl.ds(base, ts), :]
            slabs = [jnp.zeros((DSA_QBLOCK, LANES), F32) for _ in range(ts // LANES)]
            for h in range(n_heads):
                rel = _nt(qis_ref[h * DSA_QBLOCK:(h + 1) * DSA_QBLOCK, :], kt)
                wbh = wb_ref[h]
                for c in range(ts // LANES):
                    slabs[c] = slabs[c] + jnp.maximum(rel[:, c * LANES:(c + 1) * LANES], 0.0) * wbh
            sc = jnp.concatenate(slabs, axis=1)
            sc = jnp.where(sc == 0.0, 0.0, sc)
            adm = kcol + base < vis_end
            lowest = jnp.where(adm, sc, jnp.inf)
            sc = jnp.where(adm, sc, -jnp.inf)
            keys_ref[:, pl.ds(base, ts)] = _order_key(sc)
            for c in range(ts // LANES):
                mx = jnp.maximum(mx, sc[:, c * LANES:(c + 1) * LANES])
                mn = jnp.minimum(mn, lowest[:, c * LANES:(c + 1) * LANES])
            return mx, mn

        mx, mn = lax.fori_loop(0, n_tiles, score_body,
                               (jnp.full((DSA_QBLOCK, LANES), -jnp.inf, F32),
                                jnp.full((DSA_QBLOCK, LANES), jnp.inf, F32)))
        row_max = jnp.broadcast_to(jnp.max(mx, axis=1, keepdims=True), (DSA_QBLOCK, LANES))
        row_min = jnp.broadcast_to(jnp.min(mn, axis=1, keepdims=True), (DSA_QBLOCK, LANES))

        def pad_body(t, carry):
            base = pl.multiple_of(t * ts, ts)
            keys_ref[:, pl.ds(base, ts)] = jnp.full((DSA_QBLOCK, ts), _NEG_INF_KEY, I32)
            return carry

        n_cover = ((start + DSA_QBLOCK + tk - 1) // tk) * (tk // ts)
        lax.fori_loop(n_tiles, n_cover, pad_body, 0)

        zeros_f = jnp.zeros((DSA_QBLOCK, LANES), F32)
        kf = float(topk)

        def count_ge(cand_key):
            def count_body(t, cnt):
                base = pl.multiple_of(t * ts, ts)
                kt = keys_ref[:, pl.ds(base, ts)]
                for c in range(ts // LANES):
                    cnt = cnt + jnp.where(kt[:, c * LANES:(c + 1) * LANES] >= cand_key, 1.0, 0.0)
                return cnt

            cnt = lax.fori_loop(0, n_tiles, count_body, zeros_f)
            return jnp.broadcast_to(jnp.sum(cnt, axis=1, keepdims=True), (DSA_QBLOCK, LANES))

        n_adm = vis_end[:, 0:LANES].astype(F32)
        few = n_adm < kf
        c_max = count_ge(_order_key(row_max))
        at_max = c_max >= kf
        lo0 = jnp.where(few, -jnp.inf, jnp.where(at_max, row_max, row_min))
        done0 = jnp.where(few | at_max | (n_adm == kf), 1.0, 0.0)

        def search_step(st):
            lo, hi, c_lo, f_lo, f_hi, side, done = st
            cand = lo + (hi - lo) * (f_lo / (f_lo - f_hi))
            inside = (cand > lo) & (cand < hi)
            cand = jnp.where(inside, cand, 0.5 * lo + 0.5 * hi)
            stuck = jnp.logical_not((cand > lo) & (cand < hi))
            c = count_ge(_order_key(cand))
            live = jnp.logical_not(stuck) & (done < 0.5)
            up = live & (c >= kf)
            dn = live & (c < kf)
            f_hi = jnp.where(up & (side > 0.5), 0.5 * f_hi, f_hi)
            f_lo = jnp.where(dn & (side < -0.5), 0.5 * f_lo, f_lo)
            side = jnp.where(up, 1.0, jnp.where(dn, -1.0, side))
            lo = jnp.where(up, cand, lo)
            c_lo = jnp.where(up, c, c_lo)
            f_lo = jnp.where(up, c - kf + 0.5, f_lo)
            hi = jnp.where(dn, cand, hi)
            f_hi = jnp.where(dn, c - kf + 0.5, f_hi)
            done = jnp.where(stuck | (c_lo == kf), 1.0, done)
            return (lo, hi, c_lo, f_lo, f_hi, side, done)

        def search_body(st):
            inner = search_step(search_step(st[2:]))
            return (st[0] + 2, (jnp.min(inner[-1]) < 0.5).astype(I32)) + inner

        state = (jnp.int32(0), (jnp.min(done0) < 0.5).astype(I32), lo0, row_max, n_adm,
                 n_adm - kf + 0.5, c_max - kf + 0.5, zeros_f, done0)
        state = lax.while_loop(lambda st: (st[0] < SEARCH_MAX_ITERS) & (st[1] > 0), search_body, state)
        thr_ref[...] = _order_key(state[2])

        @pl.when(state[1] > 0)
        def _():
            sign_bit = jnp.int32(-2 ** 31)

            def bisect_body(it, res):
                cand = res | jnp.left_shift(jnp.int32(1), 31 - it)
                return jnp.where(count_ge(cand ^ sign_bit) >= kf, cand, res)

            res = lax.fori_loop(0, 32, bisect_body, jnp.zeros((DSA_QBLOCK, LANES), I32))
            thr_ref[...] = res ^ sign_bit

        thr = thr_ref[...]

        def tally_body(t, carry):
            gt, ge = carry
            base = pl.multiple_of(t * ts, ts)
            kt = keys_ref[:, pl.ds(base, ts)]
            for c in range(ts // LANES):
                ks = kt[:, c * LANES:(c + 1) * LANES]
                gt = gt + jnp.where(ks > thr, 1.0, 0.0)
                ge = ge + jnp.where(ks >= thr, 1.0, 0.0)
            return gt, ge

        gt, ge = lax.fori_loop(0, n_tiles, tally_body, (zeros_f, zeros_f))
        need = float(topk) - jnp.sum(gt, axis=1, keepdims=True)
        n_ge = jnp.sum(ge, axis=1, keepdims=True)
        jcut_ref[...] = jnp.full((DSA_QBLOCK, LANES), 2 ** idx_bits - 1, I32)

        @pl.when(jnp.max(n_ge) > float(topk))
        def _():
            def cut_body(it, cut):
                cand = cut | jnp.left_shift(jnp.int32(1), idx_bits - 1 - it)

                def count_body(t, cnt):
                    base = pl.multiple_of(t * ts, ts)
                    kt = keys_ref[:, pl.ds(base, ts)]
                    for c in range(ts // LANES):
                        idx = lane_q + (base + c * LANES)
                        hit = (kt[:, c * LANES:(c + 1) * LANES] == thr) & (idx < cand)
                        cnt = cnt + jnp.where(hit, 1.0, 0.0)
                    return cnt

                cnt = lax.fori_loop(0, n_tiles, count_body, zeros_f)
                return jnp.where(jnp.sum(cnt, axis=1, keepdims=True) <= need, cand, cut)

            jcut_ref[...] = lax.fori_loop(0, idx_bits, cut_body, jnp.zeros((DSA_QBLOCK, LANES), I32))

    kbase = pl.multiple_of(kj * tk, tk)
    key_t = keys_ref[:, pl.ds(kbase, tk)]
    thr = thr_ref[...]
    reps = tk // LANES
    thr_t = jnp.concatenate([thr] * reps, axis=1)
    cut_t = jnp.concatenate([jcut_ref[...]] * reps, axis=1)
    kidx = lax.broadcasted_iota(I32, (DSA_QBLOCK, tk), 1) + kbase
    mask = ((key_t > thr_t) | ((key_t == thr_t) & (kidx < cut_t))) & (key_t > jnp.int32(_NEG_INF_KEY))

    pen = jnp.where(mask, 0.0, NEG_BIG).astype(BF16)
    def qk(p):
        s_ref[p] = _nt(qst_ref[p], k_ref[:, p * LANES:(p + 1) * LANES]).astype(BF16)

    qk(0)

    def lane_slabs(a):
        return [a[:, c * LANES:(c + 1) * LANES] for c in range(reps)]

    for p in range(n_pairs):
        if p + 1 < n_pairs:
            qk(p + 1)
        pes, alphas = [], []
        for half in range(2):
            rows = slice(half * DSA_QBLOCK, (half + 1) * DSA_QBLOCK)
            s = s_ref[p, rows, :] + pen
            row_max = jnp.max(_tree(lane_slabs(s), jnp.maximum).astype(F32), axis=1, keepdims=True)
            m_old = m_ref[p, rows, :]
            m_new = jnp.maximum(m_old, row_max)
            alpha = jnp.exp(m_old - m_new)
            pe = jnp.exp(s - jnp.concatenate([m_new.astype(BF16)] * reps, axis=1))
            row_sum = jnp.sum(_tree(lane_slabs(pe), jnp.add).astype(F32), axis=1, keepdims=True)
            m_ref[p, rows, :] = m_new
            l_ref[p, rows, :] = alpha * l_ref[p, rows, :] + row_sum
            pes.append(pe)
            alphas.append(alpha)
        pv = _dot(jnp.concatenate(pes, axis=0), v_ref[:, p * LANES:(p + 1) * LANES])
        acc_ref[p] = jnp.concatenate(alphas, axis=0) * acc_ref[p] + pv

    last = (start + DSA_QBLOCK - 1) // tk

    @pl.when(kj == last)
    def _():
        for p in range(n_pairs):
            oa = acc_ref[p, 0:DSA_QBLOCK, :] / l_ref[p, 0:DSA_QBLOCK, :]
            ob = acc_ref[p, DSA_QBLOCK:2 * DSA_QBLOCK, :] / l_ref[p, DSA_QBLOCK:2 * DSA_QBLOCK, :]
            o_ref[:, p * LANES:(p + 1) * LANES] = jnp.where(low, oa, ob).astype(o_ref.dtype)


def _dsa(q, k, v, qi, ki, ps, batch, seq, n_idx_heads):
    t, hw = q.shape
    nq = seq // DSA_QBLOCK
    tk = min(1024, seq)
    ts = min(512, seq)
    nkt = seq // tk
    topk = min(DSA_TOPK_MAX, seq // 4)
    n_heads = hw // HEAD_DIM
    qi_list, kj_list = [], []
    for i in range(nq):
        for j in range((i * DSA_QBLOCK + DSA_QBLOCK - 1) // tk + 1):
            qi_list.append(i)
            kj_list.append(j)
    qi_arr = jnp.asarray(np.array(qi_list, np.int32))
    kj_arr = jnp.asarray(np.array(kj_list, np.int32))
    nsteps = len(qi_list)

    qspec = pl.BlockSpec((DSA_QBLOCK, hw), lambda b, s, qs, ks: (b * nq + qs[s], 0))
    kspec = pl.BlockSpec((tk, hw), lambda b, s, qs, ks: (b * nkt + ks[s], 0))
    grid_spec = pltpu.PrefetchScalarGridSpec(
        num_scalar_prefetch=2,
        grid=(batch, nsteps),
        in_specs=[
            qspec, qspec,
            pl.BlockSpec((DSA_QBLOCK, LANES), lambda b, s, qs, ks: (b * nq + qs[s], 0)),
            pl.BlockSpec((seq, LANES), lambda b, s, qs, ks: (b, 0)),
            kspec, kspec,
        ],
        out_specs=qspec,
        scratch_shapes=[
            pltpu.VMEM((DSA_QBLOCK, seq), I32),
            pltpu.VMEM((DSA_QBLOCK, LANES), I32),
            pltpu.VMEM((DSA_QBLOCK, LANES), I32),
            pltpu.VMEM((n_heads // 2, 2 * DSA_QBLOCK, LANES), BF16),
            pltpu.VMEM((n_idx_heads * DSA_QBLOCK, LANES), BF16),
            pltpu.VMEM((n_idx_heads, DSA_QBLOCK, LANES), F32),
            pltpu.VMEM((n_heads // 2, 2 * DSA_QBLOCK, tk), BF16),
            pltpu.VMEM((n_heads // 2, 2 * DSA_QBLOCK, LANES), F32),
            pltpu.VMEM((n_heads // 2, 2 * DSA_QBLOCK, LANES), F32),
            pltpu.VMEM((n_heads // 2, 2 * DSA_QBLOCK, LANES), F32),
        ],
    )
    w_scale = float(n_idx_heads) ** -0.5 * float(HEAD_DIM) ** -0.5
    return pl.pallas_call(
        functools.partial(_dsa_kernel, topk=topk, w_scale=w_scale, tk=tk, ts=ts, idx_bits=seq.bit_length()),
        grid_spec=grid_spec,
        out_shape=jax.ShapeDtypeStruct((t, hw), BF16),
        compiler_params=_cparams("arbitrary", "arbitrary"),
        name="dsa",
    )(qi_arr, kj_arr, q, qi, ps, ki, k, v)


BAND_QROWS = 2 * CHUNK
BAND_KBLOCKS = (BAND_PREV * CHUNK) // BAND_QROWS + 1


def _band_kernel(*refs, q_scale):
    q_ref = refs[0]
    k_refs = refs[1:1 + BAND_KBLOCKS]
    v_refs = refs[1 + BAND_KBLOCKS:1 + 2 * BAND_KBLOCKS]
    bias_ref = refs[1 + 2 * BAND_KBLOCKS]
    o_ref = refs[2 + 2 * BAND_KBLOCKS]
    kbuf_ref, vbuf_ref = refs[3 + 2 * BAND_KBLOCKS:]
    i = pl.program_id(1)
    d = q_ref.shape[1]
    nkeys = BAND_KBLOCKS * BAND_QROWS
    for blk in range(BAND_KBLOCKS):
        kbuf_ref[blk * BAND_QROWS:(blk + 1) * BAND_QROWS, :] = k_refs[blk][...]
        vbuf_ref[blk * BAND_QROWS:(blk + 1) * BAND_QROWS, :] = v_refs[blk][...]
    kcol = lax.broadcasted_iota(I32, (2 * BAND_QROWS, nkeys), 1)
    valid = kcol >= (BAND_KBLOCKS - 1 - i) * BAND_QROWS
    low = lax.broadcasted_iota(I32, (BAND_QROWS, LANES), 1) < HEAD_DIM
    zero_b = jnp.zeros((), BF16)
    def logits(p):
        sl = slice(p * LANES, (p + 1) * LANES)
        qs = q_ref[:, sl] * q_scale
        qst = jnp.concatenate([jnp.where(low, qs, zero_b), jnp.where(low, zero_b, qs)], axis=0)
        s = _nt(qst, kbuf_ref[:, sl]) + bias_ref[p]
        return jnp.where(valid, s, -jnp.inf).astype(BF16)

    n_slabs = d // LANES
    s_next = logits(0)
    for p in range(n_slabs):
        sl = slice(p * LANES, (p + 1) * LANES)
        s = s_next
        if p + 1 < n_slabs:
            s_next = logits(p + 1)
        slabs = [s[:, c * LANES:(c + 1) * LANES] for c in range(nkeys // LANES)]
        row_max = jnp.max(_tree(slabs, jnp.maximum).astype(F32), axis=1, keepdims=True).astype(BF16)
        es = [jnp.exp(sl_c - row_max) for sl_c in slabs]
        denom = jnp.sum(_tree(es, jnp.add).astype(F32), axis=1, keepdims=True)
        pv = _dot(jnp.concatenate(es, axis=1), vbuf_ref[:, sl]) / denom
        o_ref[:, sl] = jnp.where(low, pv[0:BAND_QROWS], pv[BAND_QROWS:2 * BAND_QROWS]).astype(o_ref.dtype)


def _band_bias(rel_bias):
    nkeys = BAND_KBLOCKS * BAND_QROWS
    tq = np.arange(BAND_QROWS)[:, None]
    kj = np.arange(nkeys)[None, :]
    dist = kj - (BAND_KBLOCKS - 1) * BAND_QROWS - tq
    idx = np.clip(dist, -REL_PAST, CHUNK - 1) + REL_PAST
    qc = tq // CHUNK
    kc = kj // CHUNK
    visible = (kc >= qc) & (kc <= qc + BAND_PREV)
    bias = rel_bias.astype(F32)[:, idx]
    return jnp.where(jnp.asarray(visible)[None], bias, -jnp.inf)


def _band(qkv, rel_bias, batch, seq, d):
    t = qkv.shape[0]
    nq = seq // BAND_QROWS
    n_heads = d // HEAD_DIM
    nkeys = BAND_KBLOCKS * BAND_QROWS
    bias = _band_bias(rel_bias).reshape(n_heads // 2, 2 * BAND_QROWS, nkeys)

    def kv_spec(blk, col):
        back = BAND_KBLOCKS - 1 - blk
        return pl.BlockSpec((BAND_QROWS, d), lambda b, i: (b * nq + jnp.maximum(i - back, 0), col))

    in_specs = [pl.BlockSpec((BAND_QROWS, d), lambda b, i: (b * nq + i, 0))]
    in_specs += [kv_spec(blk, 1) for blk in range(BAND_KBLOCKS)]
    in_specs += [kv_spec(blk, 2) for blk in range(BAND_KBLOCKS)]
    in_specs += [pl.BlockSpec((n_heads // 2, 2 * BAND_QROWS, nkeys), lambda b, i: (0, 0, 0))]
    return pl.pallas_call(
        functools.partial(_band_kernel, q_scale=HEAD_DIM ** -0.5),
        grid=(batch, nq),
        in_specs=in_specs,
        out_specs=pl.BlockSpec((BAND_QROWS, d), lambda b, i: (b * nq + i, 0)),
        out_shape=jax.ShapeDtypeStruct((t, d), BF16),
        scratch_shapes=[pltpu.VMEM((nkeys, d), BF16), pltpu.VMEM((nkeys, d), BF16)],
        compiler_params=_cparams("parallel", "arbitrary"),
        name="band",
    )(*([qkv] * (1 + 2 * BAND_KBLOCKS)), bias)


def _rope_tables(seq):
    half = HEAD_DIM // 2
    inv = ROPE_THETA ** (-jnp.arange(half, dtype=F32) / half)
    ang = jnp.arange(seq, dtype=F32)[:, None] * inv[None, :]
    cos = jnp.cos(ang)
    sin = jnp.sin(ang)
    reps = LANES // HEAD_DIM
    cos_t = jnp.tile(jnp.concatenate([cos, cos], axis=1), (1, reps))
    sin_t = jnp.tile(jnp.concatenate([-sin, sin], axis=1), (1, reps))
    return cos_t, sin_t


def _ssd_dsa_mixer(x, norm, w_in, conv_w, conv_b, dt_bias, a_log, d_skip, out_norm, w_out, batch, seq):
    t, d = x.shape
    hw = d // 2
    n_ssd_heads = d // HEAD_DIM
    n_idx_heads = d // 128
    xbc = d + 2 * SSD_GROUPS * SSD_STATE
    sizes = (d, xbc, n_ssd_heads, hw, hw, hw, n_idx_heads * HEAD_DIM, HEAD_DIM, n_idx_heads)
    offs = np.concatenate([[0], np.cumsum(sizes)])
    seg = [w_in[:, offs[i]:offs[i + 1]] for i in range(len(sizes))]
    w_z, w_xbc, w_dt, w_q, w_k, w_v, w_qi, w_ki, w_wi = seg
    w_main = jnp.concatenate([w_z, w_xbc, w_q, w_k, w_v, w_qi], axis=1).astype(BF16)
    w_small = jnp.zeros((d, LANES), F32)
    w_small = w_small.at[:, KIDX_LANE:KIDX_LANE + HEAD_DIM].set(w_ki)
    w_small = w_small.at[:, DT_LANE:DT_LANE + n_ssd_heads].set(w_dt)
    w_small = w_small.at[:, WIDX_LANE:WIDX_LANE + n_idx_heads].set(w_wi)
    pm = _norm_matmul(x, norm, w_main, hw, F32)
    ps = _norm_matmul(x, norm, w_small.astype(BF16), LANES, F32)
    y = _ssd(pm, ps, conv_w, conv_b, dt_bias, a_log, d_skip, out_norm, batch, seq, d)
    cos_t, sin_t = _rope_tables(seq)
    col_q = (d + xbc) // hw
    q, k, v, qi, ki = _dsa_prep(pm, ps, cos_t, sin_t, seq, hw, col_q)
    o = _dsa(q, k, v, qi, ki, ps, batch, seq, n_idx_heads)
    w_out_b = w_out.astype(BF16)
    return _proj_res(x, [y, o], [w_out_b[:d], w_out_b[d:]])


def _band_mixer(x, norm, w_qkv, rel_bias, w_out, batch, seq):
    t, d = x.shape
    qkv = _norm_matmul(x, norm, w_qkv.astype(BF16), d // 2, BF16)
    o = _band(qkv, rel_bias, batch, seq, d)
    return _proj_res(x, [o], [w_out.astype(BF16)])


def kernel(x, ffn1_norm, ffn1_w_in, ffn1_w_out, mix_norm, ab_w_in, ssd_conv_w, ssd_conv_b, ssd_dt_bias, ssd_a_log, ssd_d_skip, ssd_out_norm, ab_w_out, c_w_qkv, c_rel_bias, c_w_out, ffn2_norm, ffn2_w_in, ffn2_w_out, final_norm):
    batch, seq, d = x.shape
    depth = ffn1_norm.shape[0]
    xf = x.reshape(batch * seq, d)
    for layer in range(depth):
        xf = _ffn(xf, ffn1_norm[layer], ffn1_w_in[layer].astype(BF16), ffn1_w_out[layer].astype(BF16))
        i = layer // 2
        if layer % 2 == 0:
            xf = _ssd_dsa_mixer(xf, mix_norm[layer], ab_w_in[i], ssd_conv_w[i], ssd_conv_b[i], ssd_dt_bias[i],
                                ssd_a_log[i], ssd_d_skip[i], ssd_out_norm[i], ab_w_out[i], batch, seq)
        else:
            xf = _band_mixer(xf, mix_norm[layer], c_w_qkv[i], c_rel_bias[i], c_w_out[i], batch, seq)
        fin = final_norm if layer == depth - 1 else None
        xf = _ffn(xf, ffn2_norm[layer], ffn2_w_in[layer].astype(BF16), ffn2_w_out[layer].astype(BF16), fin)
    return xf.reshape(batch, seq, d)
```

```python
import functools

import jax
import jax.numpy as jnp
import numpy as np
from jax import lax
from jax.experimental import pallas as pl
from jax.experimental.pallas import tpu as pltpu

F32 = jnp.float32
BF16 = jnp.bfloat16
I32 = jnp.int32

EPS = 1e-5
ROPE_THETA = 10000.0
CHUNK = 64
HEAD_DIM = 64
SSD_GROUPS = 4
SSD_STATE = 128
SSD_CONV = 4
DSA_TOPK_MAX = 256
DSA_QBLOCK = 128
SEARCH_MAX_ITERS = 40
BAND_PREV = 8
REL_PAST = 256

LANES = 128
SUBLANES = 8
VMEM_LIMIT_BYTES = 56 * 1024 * 1024

KIDX_LANE = 0
DT_LANE = 64
WIDX_LANE = 96

NEG_BIG = -1e30
HIGHEST = lax.Precision.HIGHEST


def _cparams(*sem):
    return pltpu.CompilerParams(dimension_semantics=sem, vmem_limit_bytes=VMEM_LIMIT_BYTES)


def _nt(a, b, **kw):
    return lax.dot_general(a, b, (((1,), (1,)), ((), ())), preferred_element_type=F32, **kw)


def _tn(a, b, **kw):
    return lax.dot_general(a, b, (((0,), (0,)), ((), ())), preferred_element_type=F32, **kw)


def _dot(a, b, **kw):
    return jnp.dot(a, b, preferred_element_type=F32, **kw)


def _rms(x, gain):
    ms = jnp.mean(x * x, axis=-1, keepdims=True)
    return x * lax.rsqrt(ms + EPS) * gain


def _silu(x):
    return x * jax.nn.sigmoid(x)


def _tree(parts, op):
    while len(parts) > 1:
        parts = [op(parts[i], parts[i + 1]) for i in range(0, len(parts) - 1, 2)] + parts[len(parts) & ~1:]
    return parts[0]


def _norm_matmul_kernel(x_ref, g_ref, w_ref, o_ref, xn_ref):
    @pl.when(pl.program_id(1) == 0)
    def _():
        xn_ref[...] = _rms(x_ref[...], g_ref[...]).astype(BF16)

    o_ref[...] = _dot(xn_ref[...], w_ref[...]).astype(o_ref.dtype)


def _norm_matmul(x, gain, w, tn, out_dtype):
    t, d = x.shape
    n = w.shape[1]
    tm = min(1024, t)
    return pl.pallas_call(
        _norm_matmul_kernel,
        grid=(t // tm, n // tn),
        in_specs=[
            pl.BlockSpec((tm, d), lambda i, j: (i, 0)),
            pl.BlockSpec((1, d), lambda i, j: (0, 0)),
            pl.BlockSpec((d, tn), lambda i, j: (0, j)),
        ],
        out_specs=pl.BlockSpec((tm, tn), lambda i, j: (i, j)),
        out_shape=jax.ShapeDtypeStruct((t, n), out_dtype),
        scratch_shapes=[pltpu.VMEM((tm, d), BF16)],
        compiler_params=_cparams("parallel", "arbitrary"),
        name="norm_matmul",
    )(x, gain.reshape(1, d), w)


def _ffn_kernel(x_ref, g_ref, wg_ref, wu_ref, wo_ref, fg_ref, o_ref, xn_ref, *, final):
    j = pl.program_id(1)

    @pl.when(j == 0)
    def _():
        xn_ref[...] = _rms(x_ref[...], g_ref[...]).astype(BF16)
        o_ref[...] = jnp.zeros_like(o_ref)

    xn = xn_ref[...]
    g = _dot(xn, wg_ref[...])
    u = _dot(xn, wu_ref[...])
    a = (_silu(g) * u).astype(BF16)
    o_ref[...] += _dot(a, wo_ref[...])

    @pl.when(j == pl.num_programs(1) - 1)
    def _():
        y = x_ref[...] + 0.5 * o_ref[...]
        if final:
            y = _rms(y, fg_ref[...])
        o_ref[...] = y


def _ffn(x, gain, w_in, w_out, final_gain=None):
    t, d = x.shape
    ff = w_out.shape[0]
    tm = min(512, t)
    tf = 512
    nf = ff // tf
    final = final_gain is not None
    fg = (final_gain if final else gain).reshape(1, d)
    return pl.pallas_call(
        functools.partial(_ffn_kernel, final=final),
        grid=(t // tm, nf),
        in_specs=[
            pl.BlockSpec((tm, d), lambda i, j: (i, 0)),
            pl.BlockSpec((1, d), lambda i, j: (0, 0)),
            pl.BlockSpec((d, tf), lambda i, j: (0, j)),
            pl.BlockSpec((d, tf), lambda i, j: (0, j + nf)),
            pl.BlockSpec((tf, d), lambda i, j: (j, 0)),
            pl.BlockSpec((1, d), lambda i, j: (0, 0)),
        ],
        out_specs=pl.BlockSpec((tm, d), lambda i, j: (i, 0)),
        out_shape=jax.ShapeDtypeStruct((t, d), F32),
        scratch_shapes=[pltpu.VMEM((tm, d), BF16)],
        compiler_params=_cparams("parallel", "arbitrary"),
        name="ffn",
    )(x, gain.reshape(1, d), w_in, w_in, w_out, fg)


def _proj_res_kernel(*refs, n_lhs):
    x_ref = refs[0]
    a_refs = refs[1:1 + n_lhs]
    w_refs = refs[1 + n_lhs:1 + 2 * n_lhs]
    o_ref = refs[1 + 2 * n_lhs]
    y = x_ref[...]
    for a_ref, w_ref in zip(a_refs, w_refs):
        y = y + _dot(a_ref[...], w_ref[...])
    o_ref[...] = y


def _proj_res(x, lhs, ws):
    t, d = x.shape
    tm = min(1024, t)
    tn = d // 2
    in_specs = [pl.BlockSpec((tm, tn), lambda i, j: (i, j))]
    in_specs += [pl.BlockSpec((tm, a.shape[1]), lambda i, j: (i, 0)) for a in lhs]
    in_specs += [pl.BlockSpec((w.shape[0], tn), lambda i, j: (0, j)) for w in ws]
    return pl.pallas_call(
        functools.partial(_proj_res_kernel, n_lhs=len(lhs)),
        grid=(t // tm, d // tn),
        in_specs=in_specs,
        out_specs=pl.BlockSpec((tm, tn), lambda i, j: (i, j)),
        out_shape=jax.ShapeDtypeStruct((t, d), F32),
        compiler_params=_cparams("parallel", "arbitrary"),
        name="proj_res",
    )(x, *lhs, *ws)


def _rotate_slab(x, cos, sin_signed, first_half):
    fwd = pltpu.roll(x, HEAD_DIM // 2, 1)
    bwd = pltpu.roll(x, LANES - HEAD_DIM // 2, 1)
    return x * cos + jnp.where(first_half, bwd, fwd) * sin_signed


def _dsa_prep_kernel(q_ref, k_ref, v_ref, qi_ref, ps_ref, cos_ref, sin_ref,
                     qo_ref, ko_ref, vo_ref, qio_ref, kio_ref, *, q_scale):
    cos = cos_ref[...]
    sin = sin_ref[...]
    lane = lax.broadcasted_iota(I32, cos.shape, 1)
    first_half = (lane % HEAD_DIM) < HEAD_DIM // 2
    n_slabs = q_ref.shape[1] // LANES
    for c in range(n_slabs):
        sl = slice(c * LANES, (c + 1) * LANES)
        qo_ref[:, sl] = (_rotate_slab(q_ref[:, sl], cos, sin, first_half) * q_scale).astype(BF16)
        ko_ref[:, sl] = _rotate_slab(k_ref[:, sl], cos, sin, first_half).astype(BF16)
        qio_ref[:, sl] = _rotate_slab(qi_ref[:, sl], cos, sin, first_half).astype(BF16)
    vo_ref[...] = v_ref[...].astype(BF16)
    ki = _rotate_slab(ps_ref[...], cos, sin, first_half)
    ki_dup = jnp.where(lane < HEAD_DIM, ki, pltpu.roll(ki, HEAD_DIM, 1))
    kio_ref[...] = ki_dup.astype(BF16)


def _dsa_prep(pm, ps, cos, sin, seq, hw, col_q):
    t = pm.shape[0]
    tm = min(512, seq)
    npos = seq // tm

    def col(c):
        return pl.BlockSpec((tm, hw), lambda i: (i, c))

    pos_spec = pl.BlockSpec((tm, LANES), lambda i: (i % npos, 0))
    row_hw = pl.BlockSpec((tm, hw), lambda i: (i, 0))
    row_l = pl.BlockSpec((tm, LANES), lambda i: (i, 0))
    shp = jax.ShapeDtypeStruct((t, hw), BF16)
    return pl.pallas_call(
        functools.partial(_dsa_prep_kernel, q_scale=HEAD_DIM ** -0.5),
        grid=(t // tm,),
        in_specs=[col(col_q), col(col_q + 1), col(col_q + 2), col(col_q + 3), row_l, pos_spec, pos_spec],
        out_specs=[row_hw, row_hw, row_hw, row_hw, row_l],
        out_shape=[shp, shp, shp, shp, jax.ShapeDtypeStruct((t, LANES), BF16)],
        compiler_params=_cparams("parallel"),
        name="dsa_prep",
    )(pm, pm, pm, pm, ps, cos, sin)


def _ssd_kernel(z_ref, xr_ref, br_ref, cr_ref, ps_ref,
                wx_ref, wb_ref, wc_ref, bx_ref, bb_ref, bc_ref,
                dtb_ref, alog_ref, dsk_ref, gn_ref,
                ex_ref, lt_ref, sel0_ref, sel1_ref, up0_ref, up1_ref,
                o_ref,
                extx_ref, extb_ref, extc_ref, xs_ref, bm_ref, cm_ref, dt_ref, y_ref, st_ref,
                *, rows, n_pairs_per_group):
    r = pl.program_id(1)
    d = z_ref.shape[1]
    gw = SSD_GROUPS * SSD_STATE
    hist = SUBLANES

    @pl.when(r == 0)
    def _():
        extx_ref[0:hist, :] = jnp.zeros((hist, d), F32)
        extb_ref[0:hist, :] = jnp.zeros((hist, gw), F32)
        extc_ref[0:hist, :] = jnp.zeros((hist, gw), F32)
        st_ref[...] = jnp.zeros_like(st_ref)

    @pl.when(r > 0)
    def _():
        extx_ref[0:hist, :] = extx_ref[rows:rows + hist, :]
        extb_ref[0:hist, :] = extb_ref[rows:rows + hist, :]
        extc_ref[0:hist, :] = extc_ref[rows:rows + hist, :]

    def conv_silu(raw_ref, ext_ref, w_ref, b_ref, dst_ref):
        ext_ref[hist:hist + rows, :] = raw_ref[...]
        acc = b_ref[...] + w_ref[SSD_CONV - 1:SSD_CONV, :] * ext_ref[hist:hist + rows, :]
        for back in range(1, SSD_CONV):
            tap = SSD_CONV - 1 - back
            acc = acc + w_ref[tap:tap + 1, :] * ext_ref[hist - back:hist - back + rows, :]
        dst_ref[...] = _silu(acc)

    conv_silu(xr_ref, extx_ref, wx_ref, bx_ref, xs_ref)
    conv_silu(br_ref, extb_ref, wb_ref, bb_ref, bm_ref)
    conv_silu(cr_ref, extc_ref, wc_ref, bc_ref, cm_ref)

    lane = lax.broadcasted_iota(I32, (1, LANES), 1)
    n_heads = d // HEAD_DIM
    head_lane = (lane >= DT_LANE) & (lane < DT_LANE + n_heads)
    dt_all = jax.nn.softplus(ps_ref[...] + dtb_ref[...])
    dt_ref[...] = jnp.where(head_lane, dt_all, 0.0)
    a_vec = jnp.where(head_lane, -jnp.exp(alog_ref[...]), 0.0)

    ex = ex_ref[...]
    lt = lt_ref[...]
    up0 = up0_ref[...]
    up1 = up1_ref[...]
    t_idx = lax.broadcasted_iota(I32, (CHUNK, LANES), 0)
    s_idx = lax.broadcasted_iota(I32, (CHUNK, LANES), 1)
    tril2 = t_idx >= (s_idx % HEAD_DIM)
    low_half = s_idx < HEAD_DIM
    low_half2 = lax.broadcasted_iota(I32, (LANES, LANES), 1) < HEAD_DIM
    top_rows = lax.broadcasted_iota(I32, (LANES, LANES), 0) < HEAD_DIM
    blockdiag = low_half2 == top_rows

    def chunk_body(c, carry):
        rs = pl.ds(pl.multiple_of(c * CHUNK, CHUNK), CHUNK)
        dt = dt_ref[rs, :]
        ac = dt * a_vec
        a_cum = _dot(lt, ac, precision=HIGHEST)
        act0 = _nt(sel0_ref[...], ac, precision=HIGHEST)
        act1 = _nt(sel1_ref[...], ac, precision=HIGHEST)
        a_cum_t = _dot(act0, up0, precision=HIGHEST) + _dot(act1, up1, precision=HIGHEST)
        both = jnp.concatenate([a_cum, dt], axis=0)
        hi = both.astype(BF16)
        rem = both - hi.astype(F32)
        mid = rem.astype(BF16)
        lo = (rem - mid.astype(F32)).astype(BF16)
        both_e = (_dot(hi, ex) + _dot(mid, ex)) + _dot(lo, ex)
        col_all = both_e[0:CHUNK, :]
        dt_e = both_e[CHUNK:2 * CHUNK, :]
        last_e = col_all[CHUNK - 1:CHUNK, :]
        exp_a = jnp.exp(col_all)
        to_end = jnp.exp(last_e - col_all)
        exp_end = jnp.exp(last_e)
        xdt = xs_ref[rs, :] * dt_e
        xdt_b = xdt.astype(BF16)
        xw_b = (xdt * to_end).astype(BF16)
        gwid = d // SSD_GROUPS
        for g in range(SSD_GROUPS):
            gs = slice(g * SSD_STATE, (g + 1) * SSD_STATE)
            gd = slice(g * gwid, (g + 1) * gwid)
            bm_b = bm_ref[rs, gs].astype(BF16)
            cm_b = cm_ref[rs, gs].astype(BF16)
            cb2 = _nt(cm_b, jnp.concatenate([bm_b, bm_b], axis=0))
            st = st_ref[g]
            y_off = _dot(cm_b, st.astype(BF16)) * exp_a[:, gd]
            for jj in range(n_pairs_per_group):
                pidx = g * n_pairs_per_group + jj
                sl = slice(pidx * LANES, (pidx + 1) * LANES)
                seg = col_all[:, sl] - a_cum_t[pidx:pidx + 1, :]
                lmat = (cb2 * jnp.exp(jnp.where(tril2, seg, -jnp.inf))).astype(BF16)
                xp = xdt_b[:, sl]
                rhs = jnp.where(blockdiag, jnp.concatenate([xp, xp], axis=0), jnp.zeros((), BF16))
                y_ref[rs, sl] = _dot(lmat, rhs) + y_off[:, jj * LANES:(jj + 1) * LANES]
            st_ref[g] = st * exp_end[:, gd] + _tn(bm_b, xw_b[:, gd])
        return carry

    lax.fori_loop(0, rows // CHUNK, chunk_body, 0)

    y = (y_ref[...] + xs_ref[...] * dsk_ref[...]) * _silu(z_ref[...])
    gwid = d // SSD_GROUPS
    for g in range(SSD_GROUPS):
        gd = slice(g * gwid, (g + 1) * gwid)
        o_ref[:, gd] = _rms(y[:, gd], gn_ref[:, gd]).astype(o_ref.dtype)


def _ssd_constants(d):
    n_heads = d // HEAD_DIM
    n_pairs = n_heads // 2
    pr = max(SUBLANES, n_pairs)
    ex = np.zeros((LANES, d), np.float32)
    for h in range(n_heads):
        ex[DT_LANE + h, h * HEAD_DIM:(h + 1) * HEAD_DIM] = 1.0
    lt = np.tril(np.ones((CHUNK, CHUNK), np.float32))
    up = np.triu(np.ones((CHUNK, CHUNK), np.float32))
    up0 = np.concatenate([up, np.zeros_like(up)], axis=1)
    up1 = np.concatenate([np.zeros_like(up), up], axis=1)
    sel0 = np.zeros((pr, LANES), np.float32)
    sel1 = np.zeros((pr, LANES), np.float32)
    for j in range(n_pairs):
        sel0[j, DT_LANE + 2 * j] = 1.0
        sel1[j, DT_LANE + 2 * j + 1] = 1.0
    return [jnp.asarray(ex).astype(BF16)] + [jnp.asarray(a) for a in (lt, sel0, sel1, up0, up1)]


def _ssd(pm, ps, conv_w, conv_b, dt_bias, a_log, d_skip, out_norm, batch, seq, d):
    t = pm.shape[0]
    rows = min(256, seq)
    nr = seq // rows
    gw = SSD_GROUPS * SSD_STATE
    n_heads = d // HEAD_DIM
    gwid = d // SSD_GROUPS

    def lane_vec(v):
        return jnp.zeros((1, LANES), F32).at[0, DT_LANE:DT_LANE + n_heads].set(v)

    consts = _ssd_constants(d)
    wx, wb, wc = conv_w[:, :d], conv_w[:, d:d + gw], conv_w[:, d + gw:]
    cb = conv_b.reshape(1, -1)
    bx, bb, bc = cb[:, :d], cb[:, d:d + gw], cb[:, d + gw:]
    dsk = jnp.repeat(d_skip, HEAD_DIM).reshape(1, d)

    def rowblk(width, colblk):
        return pl.BlockSpec((rows, width), lambda b, r: (b * nr + r, colblk))

    def full(a):
        return pl.BlockSpec(a.shape, lambda b, r: (0,) * a.ndim)

    small = [wx, wb, wc, bx, bb, bc, lane_vec(dt_bias), lane_vec(a_log), dsk, out_norm.reshape(1, d)] + consts
    return pl.pallas_call(
        functools.partial(_ssd_kernel, rows=rows, n_pairs_per_group=n_heads // SSD_GROUPS // 2),
        grid=(batch, nr),
        in_specs=[rowblk(d, 0), rowblk(d, 1), rowblk(gw, 2 * d // gw), rowblk(gw, 2 * d // gw + 1),
                  rowblk(LANES, 0)] + [full(a) for a in small],
        out_specs=rowblk(d, 0),
        out_shape=jax.ShapeDtypeStruct((t, d), BF16),
        scratch_shapes=[
            pltpu.VMEM((rows + SUBLANES, d), F32), pltpu.VMEM((rows + SUBLANES, gw), F32),
            pltpu.VMEM((rows + SUBLANES, gw), F32),
            pltpu.VMEM((rows, d), F32), pltpu.VMEM((rows, gw), F32), pltpu.VMEM((rows, gw), F32),
            pltpu.VMEM((rows, LANES), F32), pltpu.VMEM((rows, d), F32),
            pltpu.VMEM((SSD_GROUPS, SSD_STATE, gwid), F32),
        ],
        compiler_params=_cparams("arbitrary", "arbitrary"),
        name="ssd",
    )(pm, pm, pm, pm, ps, *small)


def _order_key(score):
    bits = lax.bitcast_convert_type(score, I32)
    return bits ^ ((bits >> 31) & jnp.int32(0x7FFFFFFF))


_NEG_INF_KEY = int(np.array(-np.inf, np.float32).view(np.int32)) ^ 0x7FFFFFFF


def _dsa_kernel(qi_s, kj_s, q_ref, qi_ref, ps_ref, ki_ref, k_ref, v_ref, o_ref,
                keys_ref, thr_ref, jcut_ref, qst_ref, qis_ref, wb_ref, s_ref, m_ref, l_ref, acc_ref,
                *, topk, w_scale, tk, ts, idx_bits):
    step = pl.program_id(1)
    qblk = qi_s[step]
    kj = kj_s[step]
    start = qblk * DSA_QBLOCK
    hw = q_ref.shape[1]
    n_heads = hw // HEAD_DIM
    n_pairs = n_heads // 2
    lane_q = lax.broadcasted_iota(I32, (DSA_QBLOCK, LANES), 1)
    low = lane_q < HEAD_DIM

    @pl.when(kj == 0)
    def _():
        zero_b = jnp.zeros((), BF16)
        for p in range(n_pairs):
            sl = slice(p * LANES, (p + 1) * LANES)
            qs = q_ref[:, sl]
            qst_ref[p, 0:DSA_QBLOCK, :] = jnp.where(low, qs, zero_b)
            qst_ref[p, DSA_QBLOCK:2 * DSA_QBLOCK, :] = jnp.where(low, zero_b, qs)
            qis = qi_ref[:, sl]
            qis_ref[(2 * p) * DSA_QBLOCK:(2 * p + 1) * DSA_QBLOCK, :] = jnp.where(low, qis, zero_b)
            qis_ref[(2 * p + 1) * DSA_QBLOCK:(2 * p + 2) * DSA_QBLOCK, :] = jnp.where(low, zero_b, qis)
        w = ps_ref[...] * w_scale
        for h in range(n_heads):
            wb_ref[h] = jnp.broadcast_to(w[:, WIDX_LANE + h:WIDX_LANE + h + 1], (DSA_QBLOCK, LANES))
        m_ref[...] = jnp.full(m_ref.shape, NEG_BIG, F32)
        l_ref[...] = jnp.zeros_like(l_ref)
        acc_ref[...] = jnp.zeros_like(acc_ref)

        n_tiles = (start + DSA_QBLOCK + ts - 1) // ts
        row = lax.broadcasted_iota(I32, (DSA_QBLOCK, ts), 0)
        kcol = lax.broadcasted_iota(I32, (DSA_QBLOCK, ts), 1)
        vis_end = start + CHUNK + CHUNK * (row // CHUNK)

        def score_body(t, carry):
            mx, mn = carry
            base = pl.multiple_of(t * ts, ts)
            kt = ki_ref[pl.ds(base, ts), :]
            rel_all = _nt(qis_ref[...], kt)
            slabs = []
            for c in range(ts // LANES):
                acc = jnp.zeros((DSA_QBLOCK, LANES), F32)
                for h in range(n_heads):
                    rel = rel_all[h * DSA_QBLOCK:(h + 1) * DSA_QBLOCK, c * LANES:(c + 1) * LANES]
                    acc = acc + jnp.maximum(rel, 0.0) * wb_ref[h]
                slabs.append(acc)
            sc = jnp.concatenate(slabs, axis=1)
            sc = jnp.where(sc == 0.0, 0.0, sc)
            adm = kcol + base < vis_end
            lowest = jnp.where(adm, sc, jnp.inf)
            sc = jnp.where(adm, sc, -jnp.inf)
            keys_ref[:, pl.ds(base, ts)] = _order_key(sc)
            for c in range(ts // LANES):
                mx = jnp.maximum(mx, sc[:, c * LANES:(c + 1) * LANES])
                mn = jnp.minimum(mn, lowest[:, c * LANES:(c + 1) * LANES])
            return mx, mn

        mx, mn = lax.fori_loop(0, n_tiles, score_body,
                               (jnp.full((DSA_QBLOCK, LANES), -jnp.inf, F32),
                                jnp.full((DSA_QBLOCK, LANES), jnp.inf, F32)))
        row_max = jnp.broadcast_to(jnp.max(mx, axis=1, keepdims=True), (DSA_QBLOCK, LANES))
        row_min = jnp.broadcast_to(jnp.min(mn, axis=1, keepdims=True), (DSA_QBLOCK, LANES))

        def pad_body(t, carry):
            base = pl.multiple_of(t * ts, ts)
            keys_ref[:, pl.ds(base, ts)] = jnp.full((DSA_QBLOCK, ts), _NEG_INF_KEY, I32)
            return carry

        n_cover = ((start + DSA_QBLOCK + tk - 1) // tk) * (tk // ts)
        lax.fori_loop(n_tiles, n_cover, pad_body, 0)

        zeros_f = jnp.zeros((DSA_QBLOCK, LANES), F32)
        kf = float(topk)

        def count_ge(cand_key):
            def count_body(t, cnt):
                base = pl.multiple_of(t * ts, ts)
                kt = keys_ref[:, pl.ds(base, ts)]
                for c in range(ts // LANES):
                    cnt = cnt + jnp.where(kt[:, c * LANES:(c + 1) * LANES] >= cand_key, 1.0, 0.0)
                return cnt

            cnt = lax.fori_loop(0, n_tiles, count_body, zeros_f)
            return jnp.broadcast_to(jnp.sum(cnt, axis=1, keepdims=True), (DSA_QBLOCK, LANES))

        n_adm = vis_end[:, 0:LANES].astype(F32)
        few = n_adm < kf
        c_max = count_ge(_order_key(row_max))
        at_max = c_max >= kf
        lo0 = jnp.where(few, -jnp.inf, jnp.where(at_max, row_max, row_min))
        done0 = jnp.where(few | at_max | (n_adm == kf), 1.0, 0.0)

        def search_step(st):
            lo, hi, c_lo, f_lo, f_hi, side, done = st
            cand = lo + (hi - lo) * (f_lo / (f_lo - f_hi))
            inside = (cand > lo) & (cand < hi)
            cand = jnp.where(inside, cand, 0.5 * lo + 0.5 * hi)
            stuck = jnp.logical_not((cand > lo) & (cand < hi))
            c = count_ge(_order_key(cand))
            live = jnp.logical_not(stuck) & (done < 0.5)
            up = live & (c >= kf)
            dn = live & (c < kf)
            f_hi = jnp.where(up & (side > 0.5), 0.5 * f_hi, f_hi)
            f_lo = jnp.where(dn & (side < -0.5), 0.5 * f_lo, f_lo)
            side = jnp.where(up, 1.0, jnp.where(dn, -1.0, side))
            lo = jnp.where(up, cand, lo)
            c_lo = jnp.where(up, c, c_lo)
            f_lo = jnp.where(up, c - kf + 0.5, f_lo)
            hi = jnp.where(dn, cand, hi)
            f_hi = jnp.where(dn, c - kf + 0.5, f_hi)
            done = jnp.where(stuck | (c_lo == kf), 1.0, done)
            return (lo, hi, c_lo, f_lo, f_hi, side, done)

        def search_body(st):
            inner = search_step(search_step(st[2:]))
            return (st[0] + 2, (jnp.min(inner[-1]) < 0.5).astype(I32)) + inner

        state = (jnp.int32(0), (jnp.min(done0) < 0.5).astype(I32), lo0, row_max, n_adm,
                 n_adm - kf + 0.5, c_max - kf + 0.5, zeros_f, done0)
        state = lax.while_loop(lambda st: (st[0] < SEARCH_MAX_ITERS) & (st[1] > 0), search_body, state)
        thr_ref[...] = _order_key(state[2])

        @pl.when(state[1] > 0)
        def _():
            sign_bit = jnp.int32(-2 ** 31)

            def bisect_body(it, res):
                cand = res | jnp.left_shift(jnp.int32(1), 31 - it)
                return jnp.where(count_ge(cand ^ sign_bit) >= kf, cand, res)

            res = lax.fori_loop(0, 32, bisect_body, jnp.zeros((DSA_QBLOCK, LANES), I32))
            thr_ref[...] = res ^ sign_bit

        thr = thr_ref[...]

        def tally_body(t, carry):
            gt, ge = carry
            base = pl.multiple_of(t * ts, ts)
            kt = keys_ref[:, pl.ds(base, ts)]
            for c in range(ts // LANES):
                ks = kt[:, c * LANES:(c + 1) * LANES]
                gt = gt + jnp.where(ks > thr, 1.0, 0.0)
                ge = ge + jnp.where(ks >= thr, 1.0, 0.0)
            return gt, ge

        gt, ge = lax.fori_loop(0, n_tiles, tally_body, (zeros_f, zeros_f))
        need = float(topk) - jnp.sum(gt, axis=1, keepdims=True)
        n_ge = jnp.sum(ge, axis=1, keepdims=True)
        jcut_ref[...] = jnp.full((DSA_QBLOCK, LANES), 2 ** idx_bits - 1, I32)

        @pl.when(jnp.max(n_ge) > float(topk))
        def _():
            def cut_body(it, cut):
                cand = cut | jnp.left_shift(jnp.int32(1), idx_bits - 1 - it)

                def count_body(t, cnt):
                    base = pl.multiple_of(t * ts, ts)
                    kt = keys_ref[:, pl.ds(base, ts)]
                    for c in range(ts // LANES):
                        idx = lane_q + (base + c * LANES)
                        hit = (kt[:, c * LANES:(c + 1) * LANES] == thr) & (idx < cand)
                        cnt = cnt + jnp.where(hit, 1.0, 0.0)
                    return cnt

                cnt = lax.fori_loop(0, n_tiles, count_body, zeros_f)
                return jnp.where(jnp.sum(cnt, axis=1, keepdims=True) <= need, cand, cut)

            jcut_ref[...] = lax.fori_loop(0, idx_bits, cut_body, jnp.zeros((DSA_QBLOCK, LANES), I32))

    kbase = pl.multiple_of(kj * tk, tk)
    key_t = keys_ref[:, pl.ds(kbase, tk)]
    thr = thr_ref[...]
    reps = tk // LANES
    thr_t = jnp.concatenate([thr] * reps, axis=1)
    cut_t = jnp.concatenate([jcut_ref[...]] * reps, axis=1)
    kidx = lax.broadcasted_iota(I32, (DSA_QBLOCK, tk), 1) + kbase
    mask = ((key_t > thr_t) | ((key_t == thr_t) & (kidx < cut_t))) & (key_t > jnp.int32(_NEG_INF_KEY))

    pen = jnp.where(mask, 0.0, NEG_BIG).astype(BF16)
    def qk(p):
        s_ref[p] = _nt(qst_ref[p], k_ref[:, p * LANES:(p + 1) * LANES]).astype(BF16)

    qk(0)

    def lane_slabs(a):
        return [a[:, c * LANES:(c + 1) * LANES] for c in range(reps)]

    for p in range(n_pairs):
        if p + 1 < n_pairs:
            qk(p + 1)
        pes, alphas = [], []
        for half in range(2):
            rows = slice(half * DSA_QBLOCK, (half + 1) * DSA_QBLOCK)
            s = s_ref[p, rows, :] + pen
            row_max = jnp.max(_tree(lane_slabs(s), jnp.maximum).astype(F32), axis=1, keepdims=True)
            m_old = m_ref[p, rows, :]
            m_new = jnp.maximum(m_old, row_max)
            alpha = jnp.exp(m_old - m_new)
            pe = jnp.exp(s - jnp.concatenate([m_new.astype(BF16)] * reps, axis=1))
            row_sum = jnp.sum(_tree(lane_slabs(pe), jnp.add).astype(F32), axis=1, keepdims=True)
            m_ref[p, rows, :] = m_new
            l_ref[p, rows, :] = alpha * l_ref[p, rows, :] + row_sum
            pes.append(pe)
            alphas.append(alpha)
        pv = _dot(jnp.concatenate(pes, axis=0), v_ref[:, p * LANES:(p + 1) * LANES])
        acc_ref[p] = jnp.concatenate(alphas, axis=0) * acc_ref[p] + pv

    last = (start + DSA_QBLOCK - 1) // tk

    @pl.when(kj == last)
    def _():
        for p in range(n_pairs):
            oa = acc_ref[p, 0:DSA_QBLOCK, :] / l_ref[p, 0:DSA_QBLOCK, :]
            ob = acc_ref[p, DSA_QBLOCK:2 * DSA_QBLOCK, :] / l_ref[p, DSA_QBLOCK:2 * DSA_QBLOCK, :]
            o_ref[:, p * LANES:(p + 1) * LANES] = jnp.where(low, oa, ob).astype(o_ref.dtype)


def _dsa(q, k, v, qi, ki, ps, batch, seq, n_idx_heads):
    t, hw = q.shape
    nq = seq // DSA_QBLOCK
    tk = min(1024, seq)
    ts = min(1024, seq)
    nkt = seq // tk
    topk = min(DSA_TOPK_MAX, seq // 4)
    n_heads = hw // HEAD_DIM
    qi_list, kj_list = [], []
    for i in range(nq):
        for j in range((i * DSA_QBLOCK + DSA_QBLOCK - 1) // tk + 1):
            qi_list.append(i)
            kj_list.append(j)
    qi_arr = jnp.asarray(np.array(qi_list, np.int32))
    kj_arr = jnp.asarray(np.array(kj_list, np.int32))
    nsteps = len(qi_list)

    qspec = pl.BlockSpec((DSA_QBLOCK, hw), lambda b, s, qs, ks: (b * nq + qs[s], 0))
    kspec = pl.BlockSpec((tk, hw), lambda b, s, qs, ks: (b * nkt + ks[s], 0))
    grid_spec = pltpu.PrefetchScalarGridSpec(
        num_scalar_prefetch=2,
        grid=(batch, nsteps),
        in_specs=[
            qspec, qspec,
            pl.BlockSpec((DSA_QBLOCK, LANES), lambda b, s, qs, ks: (b * nq + qs[s], 0)),
            pl.BlockSpec((seq, LANES), lambda b, s, qs, ks: (b, 0)),
            kspec, kspec,
        ],
        out_specs=qspec,
        scratch_shapes=[
            pltpu.VMEM((DSA_QBLOCK, seq), I32),
            pltpu.VMEM((DSA_QBLOCK, LANES), I32),
            pltpu.VMEM((DSA_QBLOCK, LANES), I32),
            pltpu.VMEM((n_heads // 2, 2 * DSA_QBLOCK, LANES), BF16),
            pltpu.VMEM((n_idx_heads * DSA_QBLOCK, LANES), BF16),
            pltpu.VMEM((n_idx_heads, DSA_QBLOCK, LANES), F32),
            pltpu.VMEM((n_heads // 2, 2 * DSA_QBLOCK, tk), BF16),
            pltpu.VMEM((n_heads // 2, 2 * DSA_QBLOCK, LANES), F32),
            pltpu.VMEM((n_heads // 2, 2 * DSA_QBLOCK, LANES), F32),
            pltpu.VMEM((n_heads // 2, 2 * DSA_QBLOCK, LANES), F32),
        ],
    )
    w_scale = float(n_idx_heads) ** -0.5 * float(HEAD_DIM) ** -0.5
    return pl.pallas_call(
        functools.partial(_dsa_kernel, topk=topk, w_scale=w_scale, tk=tk, ts=ts, idx_bits=seq.bit_length()),
        grid_spec=grid_spec,
        out_shape=jax.ShapeDtypeStruct((t, hw), BF16),
        compiler_params=_cparams("arbitrary", "arbitrary"),
        name="dsa",
    )(qi_arr, kj_arr, q, qi, ps, ki, k, v)


BAND_QROWS = 2 * CHUNK
BAND_KBLOCKS = (BAND_PREV * CHUNK) // BAND_QROWS + 1


def _band_kernel(*refs, q_scale):
    q_ref = refs[0]
    k_refs = refs[1:1 + BAND_KBLOCKS]
    v_refs = refs[1 + BAND_KBLOCKS:1 + 2 * BAND_KBLOCKS]
    bias_ref = refs[1 + 2 * BAND_KBLOCKS]
    o_ref = refs[2 + 2 * BAND_KBLOCKS]
    kbuf_ref, vbuf_ref = refs[3 + 2 * BAND_KBLOCKS:]
    i = pl.program_id(1)
    d = q_ref.shape[1]
    nkeys = BAND_KBLOCKS * BAND_QROWS
    for blk in range(BAND_KBLOCKS):
        kbuf_ref[blk * BAND_QROWS:(blk + 1) * BAND_QROWS, :] = k_refs[blk][...]
        vbuf_ref[blk * BAND_QROWS:(blk + 1) * BAND_QROWS, :] = v_refs[blk][...]
    kcol = lax.broadcasted_iota(I32, (2 * BAND_QROWS, nkeys), 1)
    valid = kcol >= (BAND_KBLOCKS - 1 - i) * BAND_QROWS
    low = lax.broadcasted_iota(I32, (BAND_QROWS, LANES), 1) < HEAD_DIM
    zero_b = jnp.zeros((), BF16)
    def logits(p):
        sl = slice(p * LANES, (p + 1) * LANES)
        qs = q_ref[:, sl] * q_scale
        qst = jnp.concatenate([jnp.where(low, qs, zero_b), jnp.where(low, zero_b, qs)], axis=0)
        s = _nt(qst, kbuf_ref[:, sl]) + bias_ref[p]
        return jnp.where(valid, s, -jnp.inf).astype(BF16)

    n_slabs = d // LANES
    s_next = logits(0)
    for p in range(n_slabs):
        sl = slice(p * LANES, (p + 1) * LANES)
        s = s_next
        if p + 1 < n_slabs:
            s_next = logits(p + 1)
        slabs = [s[:, c * LANES:(c + 1) * LANES] for c in range(nkeys // LANES)]
        row_max = jnp.max(_tree(slabs, jnp.maximum).astype(F32), axis=1, keepdims=True).astype(BF16)
        es = [jnp.exp(sl_c - row_max) for sl_c in slabs]
        denom = jnp.sum(_tree(es, jnp.add).astype(F32), axis=1, keepdims=True)
        pv = _dot(jnp.concatenate(es, axis=1), vbuf_ref[:, sl]) / denom
        o_ref[:, sl] = jnp.where(low, pv[0:BAND_QROWS], pv[BAND_QROWS:2 * BAND_QROWS]).astype(o_ref.dtype)


def _band_bias(rel_bias):
    nkeys = BAND_KBLOCKS * BAND_QROWS
    tq = np.arange(BAND_QROWS)[:, None]
    kj = np.arange(nkeys)[None, :]
    qc = tq // CHUNK
    kc = kj // CHUNK
    visible = (kc >= qc) & (kc <= qc + BAND_PREV)
    shift = BAND_QROWS - 1
    m = np.arange(nkeys + shift)
    dist = m - shift - (BAND_KBLOCKS - 1) * BAND_QROWS
    ext = rel_bias.astype(F32)[:, np.clip(dist, -REL_PAST, CHUNK - 1) + REL_PAST]
    bias = jnp.stack([ext[:, shift - t:shift - t + nkeys] for t in range(BAND_QROWS)], axis=1)
    return jnp.where(jnp.asarray(visible)[None], bias, -jnp.inf)


def _band(qkv, rel_bias, batch, seq, d):
    t = qkv.shape[0]
    nq = seq // BAND_QROWS
    n_heads = d // HEAD_DIM
    nkeys = BAND_KBLOCKS * BAND_QROWS
    bias = _band_bias(rel_bias).reshape(n_heads // 2, 2 * BAND_QROWS, nkeys)

    def kv_spec(blk, col):
        back = BAND_KBLOCKS - 1 - blk
        return pl.BlockSpec((BAND_QROWS, d), lambda b, i: (b * nq + jnp.maximum(i - back, 0), col))

    in_specs = [pl.BlockSpec((BAND_QROWS, d), lambda b, i: (b * nq + i, 0))]
    in_specs += [kv_spec(blk, 1) for blk in range(BAND_KBLOCKS)]
    in_specs += [kv_spec(blk, 2) for blk in range(BAND_KBLOCKS)]
    in_specs += [pl.BlockSpec((n_heads // 2, 2 * BAND_QROWS, nkeys), lambda b, i: (0, 0, 0))]
    return pl.pallas_call(
        functools.partial(_band_kernel, q_scale=HEAD_DIM ** -0.5),
        grid=(batch, nq),
        in_specs=in_specs,
        out_specs=pl.BlockSpec((BAND_QROWS, d), lambda b, i: (b * nq + i, 0)),
        out_shape=jax.ShapeDtypeStruct((t, d), BF16),
        scratch_shapes=[pltpu.VMEM((nkeys, d), BF16), pltpu.VMEM((nkeys, d), BF16)],
        compiler_params=_cparams("parallel", "arbitrary"),
        name="band",
    )(*([qkv] * (1 + 2 * BAND_KBLOCKS)), bias)


def _rope_tables(seq):
    half = HEAD_DIM // 2
    inv = ROPE_THETA ** (-jnp.arange(half, dtype=F32) / half)
    ang = jnp.arange(seq, dtype=F32)[:, None] * inv[None, :]
    cos = jnp.cos(ang)
    sin = jnp.sin(ang)
    reps = LANES // HEAD_DIM
    cos_t = jnp.tile(jnp.concatenate([cos, cos], axis=1), (1, reps))
    sin_t = jnp.tile(jnp.concatenate([-sin, sin], axis=1), (1, reps))
    return cos_t, sin_t


def _ssd_dsa_mixer(x, norm, w_in, conv_w, conv_b, dt_bias, a_log, d_skip, out_norm, w_out, batch, seq):
    t, d = x.shape
    hw = d // 2
    n_ssd_heads = d // HEAD_DIM
    n_idx_heads = d // 128
    xbc = d + 2 * SSD_GROUPS * SSD_STATE
    sizes = (d, xbc, n_ssd_heads, hw, hw, hw, n_idx_heads * HEAD_DIM, HEAD_DIM, n_idx_heads)
    offs = np.concatenate([[0], np.cumsum(sizes)])
    seg = [w_in[:, offs[i]:offs[i + 1]] for i in range(len(sizes))]
    w_z, w_xbc, w_dt, w_q, w_k, w_v, w_qi, w_ki, w_wi = seg
    w_main = jnp.concatenate([w_z, w_xbc, w_q, w_k, w_v, w_qi], axis=1).astype(BF16)
    w_small = jnp.zeros((d, LANES), F32)
    w_small = w_small.at[:, KIDX_LANE:KIDX_LANE + HEAD_DIM].set(w_ki)
    w_small = w_small.at[:, DT_LANE:DT_LANE + n_ssd_heads].set(w_dt)
    w_small = w_small.at[:, WIDX_LANE:WIDX_LANE + n_idx_heads].set(w_wi)
    pm = _norm_matmul(x, norm, w_main, hw, F32)
    ps = _norm_matmul(x, norm, w_small.astype(BF16), LANES, F32)
    y = _ssd(pm, ps, conv_w, conv_b, dt_bias, a_log, d_skip, out_norm, batch, seq, d)
    cos_t, sin_t = _rope_tables(seq)
    col_q = (d + xbc) // hw
    q, k, v, qi, ki = _dsa_prep(pm, ps, cos_t, sin_t, seq, hw, col_q)
    o = _dsa(q, k, v, qi, ki, ps, batch, seq, n_idx_heads)
    w_out_b = w_out.astype(BF16)
    return _proj_res(x, [y, o], [w_out_b[:d], w_out_b[d:]])


def _band_mixer(x, norm, w_qkv, rel_bias, w_out, batch, seq):
    t, d = x.shape
    qkv = _norm_matmul(x, norm, w_qkv.astype(BF16), d // 2, BF16)
    o = _band(qkv, rel_bias, batch, seq, d)
    return _proj_res(x, [o], [w_out.astype(BF16)])


def kernel(x, ffn1_norm, ffn1_w_in, ffn1_w_out, mix_norm, ab_w_in, ssd_conv_w, ssd_conv_b, ssd_dt_bias, ssd_a_log, ssd_d_skip, ssd_out_norm, ab_w_out, c_w_qkv, c_rel_bias, c_w_out, ffn2_norm, ffn2_w_in, ffn2_w_out, final_norm):
    batch, seq, d = x.shape
    depth = ffn1_norm.shape[0]
    xf = x.reshape(batch * seq, d)
    for layer in range(depth):
        xf = _ffn(xf, ffn1_norm[layer], ffn1_w_in[layer].astype(BF16), ffn1_w_out[layer].astype(BF16))
        i = layer // 2
        if layer % 2 == 0:
            xf = _ssd_dsa_mixer(xf, mix_norm[layer], ab_w_in[i], ssd_conv_w[i], ssd_conv_b[i], ssd_dt_bias[i],
                                ssd_a_log[i], ssd_d_skip[i], ssd_out_norm[i], ab_w_out[i], batch, seq)
        else:
            xf = _band_mixer(xf, mix_norm[layer], c_w_qkv[i], c_rel_bias[i], c_w_out[i], batch, seq)
        fin = final_norm if layer == depth - 1 else None
        xf = _ffn(xf, ffn2_norm[layer], ffn2_w_in[layer].astype(BF16), ffn2_w_out[layer].astype(BF16), fin)
    return xf.reshape(batch, seq, d)
```

```python
import functools

import jax
import jax.numpy as jnp
import numpy as np
from jax import lax
from jax.experimental import pallas as pl
from jax.experimental.pallas import tpu as pltpu

F32 = jnp.float32
BF16 = jnp.bfloat16
I32 = jnp.int32

EPS = 1e-5
ROPE_THETA = 10000.0
CHUNK = 64
HEAD_DIM = 64
SSD_GROUPS = 4
SSD_STATE = 128
SSD_CONV = 4
DSA_TOPK_MAX = 256
DSA_QBLOCK = 128
SEARCH_MAX_ITERS = 40
BAND_PREV = 8
REL_PAST = 256

LANES = 128
SUBLANES = 8
VMEM_LIMIT_BYTES = 56 * 1024 * 1024

KIDX_LANE = 0
DT_LANE = 64
WIDX_LANE = 96

NEG_BIG = -1e30
HIGHEST = lax.Precision.HIGHEST


def _cparams(*sem):
    return pltpu.CompilerParams(dimension_semantics=sem, vmem_limit_bytes=VMEM_LIMIT_BYTES)


def _nt(a, b, **kw):
    return lax.dot_general(a, b, (((1,), (1,)), ((), ())), preferred_element_type=F32, **kw)


def _tn(a, b, **kw):
    return lax.dot_general(a, b, (((0,), (0,)), ((), ())), preferred_element_type=F32, **kw)


def _dot(a, b, **kw):
    return jnp.dot(a, b, preferred_element_type=F32, **kw)


def _rms(x, gain):
    ms = jnp.mean(x * x, axis=-1, keepdims=True)
    return x * lax.rsqrt(ms + EPS) * gain


def _silu(x):
    return x * jax.nn.sigmoid(x)


def _tree(parts, op):
    while len(parts) > 1:
        parts = [op(parts[i], parts[i + 1]) for i in range(0, len(parts) - 1, 2)] + parts[len(parts) & ~1:]
    return parts[0]


def _norm_matmul_kernel(x_ref, g_ref, w_ref, o_ref, xn_ref):
    @pl.when(pl.program_id(1) == 0)
    def _():
        xn_ref[...] = _rms(x_ref[...], g_ref[...]).astype(BF16)

    o_ref[...] = _dot(xn_ref[...], w_ref[...]).astype(o_ref.dtype)


def _norm_matmul(x, gain, w, tn, out_dtype):
    t, d = x.shape
    n = w.shape[1]
    tm = min(1024, t)
    return pl.pallas_call(
        _norm_matmul_kernel,
        grid=(t // tm, n // tn),
        in_specs=[
            pl.BlockSpec((tm, d), lambda i, j: (i, 0)),
            pl.BlockSpec((1, d), lambda i, j: (0, 0)),
            pl.BlockSpec((d, tn), lambda i, j: (0, j)),
        ],
        out_specs=pl.BlockSpec((tm, tn), lambda i, j: (i, j)),
        out_shape=jax.ShapeDtypeStruct((t, n), out_dtype),
        scratch_shapes=[pltpu.VMEM((tm, d), BF16)],
        compiler_params=_cparams("parallel", "arbitrary"),
        name="norm_matmul",
    )(x, gain.reshape(1, d), w)


def _ffn_kernel(x_ref, g_ref, wg_ref, wu_ref, wo_ref, fg_ref, o_ref, xn_ref, *, final):
    j = pl.program_id(1)

    @pl.when(j == 0)
    def _():
        xn_ref[...] = _rms(x_ref[...], g_ref[...]).astype(BF16)
        o_ref[...] = jnp.zeros_like(o_ref)

    xn = xn_ref[...]
    g = _dot(xn, wg_ref[...])
    u = _dot(xn, wu_ref[...])
    a = (_silu(g) * u).astype(BF16)
    o_ref[...] += _dot(a, wo_ref[...])

    @pl.when(j == pl.num_programs(1) - 1)
    def _():
        y = x_ref[...] + 0.5 * o_ref[...]
        if final:
            y = _rms(y, fg_ref[...])
        o_ref[...] = y


def _ffn(x, gain, w_in, w_out, layer, final_gain=None):
    t, d = x.shape
    ff = w_out.shape[1]
    tm = min(512, t)
    tf = 512
    nf = ff // tf
    final = final_gain is not None
    fg = (final_gain if final else gain).reshape(1, d)
    return pl.pallas_call(
        functools.partial(_ffn_kernel, final=final),
        grid=(t // tm, nf),
        in_specs=[
            pl.BlockSpec((tm, d), lambda i, j: (i, 0)),
            pl.BlockSpec((1, d), lambda i, j: (0, 0)),
            pl.BlockSpec((None, d, tf), lambda i, j: (layer, 0, j)),
            pl.BlockSpec((None, d, tf), lambda i, j: (layer, 0, j + nf)),
            pl.BlockSpec((None, tf, d), lambda i, j: (layer, j, 0)),
            pl.BlockSpec((1, d), lambda i, j: (0, 0)),
        ],
        out_specs=pl.BlockSpec((tm, d), lambda i, j: (i, 0)),
        out_shape=jax.ShapeDtypeStruct((t, d), F32),
        scratch_shapes=[pltpu.VMEM((tm, d), BF16)],
        compiler_params=_cparams("parallel", "arbitrary"),
        name="ffn",
    )(x, gain.reshape(1, d), w_in, w_in, w_out, fg)


def _proj_res_kernel(*refs, n_lhs):
    x_ref = refs[0]
    a_refs = refs[1:1 + n_lhs]
    w_refs = refs[1 + n_lhs:1 + 2 * n_lhs]
    o_ref = refs[1 + 2 * n_lhs]
    y = x_ref[...]
    for a_ref, w_ref in zip(a_refs, w_refs):
        y = y + _dot(a_ref[...], w_ref[...])
    o_ref[...] = y


def _proj_res(x, lhs, ws):
    t, d = x.shape
    tm = min(1024, t)
    tn = d // 2
    in_specs = [pl.BlockSpec((tm, tn), lambda i, j: (i, j))]
    in_specs += [pl.BlockSpec((tm, a.shape[1]), lambda i, j: (i, 0)) for a in lhs]
    in_specs += [pl.BlockSpec((w.shape[0], tn), lambda i, j: (0, j)) for w in ws]
    return pl.pallas_call(
        functools.partial(_proj_res_kernel, n_lhs=len(lhs)),
        grid=(t // tm, d // tn),
        in_specs=in_specs,
        out_specs=pl.BlockSpec((tm, tn), lambda i, j: (i, j)),
        out_shape=jax.ShapeDtypeStruct((t, d), F32),
        compiler_params=_cparams("parallel", "arbitrary"),
        name="proj_res",
    )(x, *lhs, *ws)


def _rotate_slab(x, cos, sin_signed, first_half):
    fwd = pltpu.roll(x, HEAD_DIM // 2, 1)
    bwd = pltpu.roll(x, LANES - HEAD_DIM // 2, 1)
    return x * cos + jnp.where(first_half, bwd, fwd) * sin_signed


def _dsa_prep_kernel(q_ref, k_ref, v_ref, qi_ref, ps_ref, cos_ref, sin_ref,
                     qo_ref, ko_ref, vo_ref, qio_ref, kio_ref, *, q_scale):
    cos = cos_ref[...]
    sin = sin_ref[...]
    lane = lax.broadcasted_iota(I32, cos.shape, 1)
    first_half = (lane % HEAD_DIM) < HEAD_DIM // 2
    n_slabs = q_ref.shape[1] // LANES
    for c in range(n_slabs):
        sl = slice(c * LANES, (c + 1) * LANES)
        qo_ref[:, sl] = (_rotate_slab(q_ref[:, sl], cos, sin, first_half) * q_scale).astype(BF16)
        ko_ref[:, sl] = _rotate_slab(k_ref[:, sl], cos, sin, first_half).astype(BF16)
        qio_ref[:, sl] = _rotate_slab(qi_ref[:, sl], cos, sin, first_half).astype(BF16)
    vo_ref[...] = v_ref[...].astype(BF16)
    ki = _rotate_slab(ps_ref[...], cos, sin, first_half)
    ki_dup = jnp.where(lane < HEAD_DIM, ki, pltpu.roll(ki, HEAD_DIM, 1))
    kio_ref[...] = ki_dup.astype(BF16)


def _dsa_prep(pm, ps, cos, sin, seq, hw, col_q):
    t = pm.shape[0]
    tm = min(512, seq)
    npos = seq // tm

    def col(c):
        return pl.BlockSpec((tm, hw), lambda i: (i, c))

    pos_spec = pl.BlockSpec((tm, LANES), lambda i: (i % npos, 0))
    row_hw = pl.BlockSpec((tm, hw), lambda i: (i, 0))
    row_l = pl.BlockSpec((tm, LANES), lambda i: (i, 0))
    shp = jax.ShapeDtypeStruct((t, hw), BF16)
    return pl.pallas_call(
        functools.partial(_dsa_prep_kernel, q_scale=HEAD_DIM ** -0.5),
        grid=(t // tm,),
        in_specs=[col(col_q), col(col_q + 1), col(col_q + 2), col(col_q + 3), row_l, pos_spec, pos_spec],
        out_specs=[row_hw, row_hw, row_hw, row_hw, row_l],
        out_shape=[shp, shp, shp, shp, jax.ShapeDtypeStruct((t, LANES), BF16)],
        compiler_params=_cparams("parallel"),
        name="dsa_prep",
    )(pm, pm, pm, pm, ps, cos, sin)


def _ssd_kernel(z_ref, xr_ref, br_ref, cr_ref, ps_ref,
                wx_ref, wb_ref, wc_ref, bx_ref, bb_ref, bc_ref,
                dtb_ref, alog_ref, dsk_ref, gn_ref,
                ex_ref, lt_ref, sel0_ref, sel1_ref, up0_ref, up1_ref,
                o_ref,
                extx_ref, extb_ref, extc_ref, xs_ref, bm_ref, cm_ref, dt_ref, y_ref, st_ref,
                *, rows, n_pairs_per_group):
    r = pl.program_id(1)
    d = z_ref.shape[1]
    gw = SSD_GROUPS * SSD_STATE
    hist = SUBLANES

    @pl.when(r == 0)
    def _():
        extx_ref[0:hist, :] = jnp.zeros((hist, d), F32)
        extb_ref[0:hist, :] = jnp.zeros((hist, gw), F32)
        extc_ref[0:hist, :] = jnp.zeros((hist, gw), F32)
        st_ref[...] = jnp.zeros_like(st_ref)

    @pl.when(r > 0)
    def _():
        extx_ref[0:hist, :] = extx_ref[rows:rows + hist, :]
        extb_ref[0:hist, :] = extb_ref[rows:rows + hist, :]
        extc_ref[0:hist, :] = extc_ref[rows:rows + hist, :]

    def conv_silu(raw_ref, ext_ref, w_ref, b_ref, dst_ref):
        ext_ref[hist:hist + rows, :] = raw_ref[...]
        acc = b_ref[...] + w_ref[SSD_CONV - 1:SSD_CONV, :] * ext_ref[hist:hist + rows, :]
        for back in range(1, SSD_CONV):
            tap = SSD_CONV - 1 - back
            acc = acc + w_ref[tap:tap + 1, :] * ext_ref[hist - back:hist - back + rows, :]
        dst_ref[...] = _silu(acc)

    conv_silu(xr_ref, extx_ref, wx_ref, bx_ref, xs_ref)
    conv_silu(br_ref, extb_ref, wb_ref, bb_ref, bm_ref)
    conv_silu(cr_ref, extc_ref, wc_ref, bc_ref, cm_ref)

    lane = lax.broadcasted_iota(I32, (1, LANES), 1)
    n_heads = d // HEAD_DIM
    head_lane = (lane >= DT_LANE) & (lane < DT_LANE + n_heads)
    dt_all = jax.nn.softplus(ps_ref[...] + dtb_ref[...])
    dt_ref[...] = jnp.where(head_lane, dt_all, 0.0)
    a_vec = jnp.where(head_lane, -jnp.exp(alog_ref[...]), 0.0)

    ex = ex_ref[...]
    lt = lt_ref[...]
    up0 = up0_ref[...]
    up1 = up1_ref[...]
    t_idx = lax.broadcasted_iota(I32, (CHUNK, LANES), 0)
    s_idx = lax.broadcasted_iota(I32, (CHUNK, LANES), 1)
    tril2 = t_idx >= (s_idx % HEAD_DIM)
    low_half = s_idx < HEAD_DIM
    low_half2 = lax.broadcasted_iota(I32, (LANES, LANES), 1) < HEAD_DIM
    top_rows = lax.broadcasted_iota(I32, (LANES, LANES), 0) < HEAD_DIM
    blockdiag = low_half2 == top_rows

    def chunk_body(c, carry):
        rs = pl.ds(pl.multiple_of(c * CHUNK, CHUNK), CHUNK)
        dt = dt_ref[rs, :]
        ac = dt * a_vec
        a_cum = _dot(lt, ac, precision=HIGHEST)
        act0 = _nt(sel0_ref[...], ac, precision=HIGHEST)
        act1 = _nt(sel1_ref[...], ac, precision=HIGHEST)
        a_cum_t = _dot(act0, up0, precision=HIGHEST) + _dot(act1, up1, precision=HIGHEST)
        both = jnp.concatenate([a_cum, dt], axis=0)
        hi = both.astype(BF16)
        rem = both - hi.astype(F32)
        mid = rem.astype(BF16)
        lo = (rem - mid.astype(F32)).astype(BF16)
        both_e = (_dot(hi, ex) + _dot(mid, ex)) + _dot(lo, ex)
        col_all = both_e[0:CHUNK, :]
        dt_e = both_e[CHUNK:2 * CHUNK, :]
        last_e = col_all[CHUNK - 1:CHUNK, :]
        exp_a = jnp.exp(col_all)
        to_end = jnp.exp(last_e - col_all)
        exp_end = jnp.exp(last_e)
        xdt = xs_ref[rs, :] * dt_e
        xdt_b = xdt.astype(BF16)
        xw_b = (xdt * to_end).astype(BF16)
        gwid = d // SSD_GROUPS
        for g in range(SSD_GROUPS):
            gs = slice(g * SSD_STATE, (g + 1) * SSD_STATE)
            gd = slice(g * gwid, (g + 1) * gwid)
            bm_b = bm_ref[rs, gs].astype(BF16)
            cm_b = cm_ref[rs, gs].astype(BF16)
            cb2 = _nt(cm_b, jnp.concatenate([bm_b, bm_b], axis=0))
            st = st_ref[g]
            y_off = _dot(cm_b, st.astype(BF16)) * exp_a[:, gd]
            for jj in range(n_pairs_per_group):
                pidx = g * n_pairs_per_group + jj
                sl = slice(pidx * LANES, (pidx + 1) * LANES)
                seg = col_all[:, sl] - a_cum_t[pidx:pidx + 1, :]
                lmat = (cb2 * jnp.exp(jnp.where(tril2, seg, -jnp.inf))).astype(BF16)
                xp = xdt_b[:, sl]
                rhs = jnp.where(blockdiag, jnp.concatenate([xp, xp], axis=0), jnp.zeros((), BF16))
                y_ref[rs, sl] = _dot(lmat, rhs) + y_off[:, jj * LANES:(jj + 1) * LANES]
            st_ref[g] = st * exp_end[:, gd] + _tn(bm_b, xw_b[:, gd])
        return carry

    lax.fori_loop(0, rows // CHUNK, chunk_body, 0)

    y = (y_ref[...] + xs_ref[...] * dsk_ref[...]) * _silu(z_ref[...])
    gwid = d // SSD_GROUPS
    for g in range(SSD_GROUPS):
        gd = slice(g * gwid, (g + 1) * gwid)
        o_ref[:, gd] = _rms(y[:, gd], gn_ref[:, gd]).astype(o_ref.dtype)


def _ssd_constants(d):
    n_heads = d // HEAD_DIM
    n_pairs = n_heads // 2
    pr = max(SUBLANES, n_pairs)
    ex = np.zeros((LANES, d), np.float32)
    for h in range(n_heads):
        ex[DT_LANE + h, h * HEAD_DIM:(h + 1) * HEAD_DIM] = 1.0
    lt = np.tril(np.ones((CHUNK, CHUNK), np.float32))
    up = np.triu(np.ones((CHUNK, CHUNK), np.float32))
    up0 = np.concatenate([up, np.zeros_like(up)], axis=1)
    up1 = np.concatenate([np.zeros_like(up), up], axis=1)
    sel0 = np.zeros((pr, LANES), np.float32)
    sel1 = np.zeros((pr, LANES), np.float32)
    for j in range(n_pairs):
        sel0[j, DT_LANE + 2 * j] = 1.0
        sel1[j, DT_LANE + 2 * j + 1] = 1.0
    return [jnp.asarray(ex).astype(BF16)] + [jnp.asarray(a) for a in (lt, sel0, sel1, up0, up1)]


def _ssd(pm, ps, conv_w, conv_b, dt_bias, a_log, d_skip, out_norm, batch, seq, d):
    t = pm.shape[0]
    rows = min(256, seq)
    nr = seq // rows
    gw = SSD_GROUPS * SSD_STATE
    n_heads = d // HEAD_DIM
    gwid = d // SSD_GROUPS

    def lane_vec(v):
        return jnp.zeros((1, LANES), F32).at[0, DT_LANE:DT_LANE + n_heads].set(v)

    consts = _ssd_constants(d)
    wx, wb, wc = conv_w[:, :d], conv_w[:, d:d + gw], conv_w[:, d + gw:]
    cb = conv_b.reshape(1, -1)
    bx, bb, bc = cb[:, :d], cb[:, d:d + gw], cb[:, d + gw:]
    dsk = jnp.repeat(d_skip, HEAD_DIM).reshape(1, d)

    def rowblk(width, colblk):
        return pl.BlockSpec((rows, width), lambda b, r: (b * nr + r, colblk))

    def full(a):
        return pl.BlockSpec(a.shape, lambda b, r: (0,) * a.ndim)

    small = [wx, wb, wc, bx, bb, bc, lane_vec(dt_bias), lane_vec(a_log), dsk, out_norm.reshape(1, d)] + consts
    return pl.pallas_call(
        functools.partial(_ssd_kernel, rows=rows, n_pairs_per_group=n_heads // SSD_GROUPS // 2),
        grid=(batch, nr),
        in_specs=[rowblk(d, 0), rowblk(d, 1), rowblk(gw, 2 * d // gw), rowblk(gw, 2 * d // gw + 1),
                  rowblk(LANES, 0)] + [full(a) for a in small],
        out_specs=rowblk(d, 0),
        out_shape=jax.ShapeDtypeStruct((t, d), BF16),
        scratch_shapes=[
            pltpu.VMEM((rows + SUBLANES, d), F32), pltpu.VMEM((rows + SUBLANES, gw), F32),
            pltpu.VMEM((rows + SUBLANES, gw), F32),
            pltpu.VMEM((rows, d), F32), pltpu.VMEM((rows, gw), F32), pltpu.VMEM((rows, gw), F32),
            pltpu.VMEM((rows, LANES), F32), pltpu.VMEM((rows, d), F32),
            pltpu.VMEM((SSD_GROUPS, SSD_STATE, gwid), F32),
        ],
        compiler_params=_cparams("arbitrary", "arbitrary"),
        name="ssd",
    )(pm, pm, pm, pm, ps, *small)


def _order_key(score):
    bits = lax.bitcast_convert_type(score, I32)
    return bits ^ ((bits >> 31) & jnp.int32(0x7FFFFFFF))


_NEG_INF_KEY = int(np.array(-np.inf, np.float32).view(np.int32)) ^ 0x7FFFFFFF


def _dsa_kernel(qi_s, kj_s, q_ref, qi_ref, ps_ref, ki_ref, k_ref, v_ref, o_ref,
                keys_ref, thr_ref, jcut_ref, qst_ref, qis_ref, wb_ref, s_ref, m_ref, l_ref, acc_ref,
                *, topk, w_scale, tk, ts, idx_bits):
    step = pl.program_id(1)
    qblk = qi_s[step]
    kj = kj_s[step]
    start = qblk * DSA_QBLOCK
    hw = q_ref.shape[1]
    n_heads = hw // HEAD_DIM
    n_pairs = n_heads // 2
    lane_q = lax.broadcasted_iota(I32, (DSA_QBLOCK, LANES), 1)
    low = lane_q < HEAD_DIM

    @pl.when(kj == 0)
    def _():
        zero_b = jnp.zeros((), BF16)
        for p in range(n_pairs):
            sl = slice(p * LANES, (p + 1) * LANES)
            qs = q_ref[:, sl]
            qst_ref[p, 0:DSA_QBLOCK, :] = jnp.where(low, qs, zero_b)
            qst_ref[p, DSA_QBLOCK:2 * DSA_QBLOCK, :] = jnp.where(low, zero_b, qs)
            qis = qi_ref[:, sl]
            qis_ref[(2 * p) * DSA_QBLOCK:(2 * p + 1) * DSA_QBLOCK, :] = jnp.where(low, qis, zero_b)
            qis_ref[(2 * p + 1) * DSA_QBLOCK:(2 * p + 2) * DSA_QBLOCK, :] = jnp.where(low, zero_b, qis)
        w = ps_ref[...] * w_scale
        for h in range(n_heads):
            wb_ref[h] = jnp.broadcast_to(w[:, WIDX_LANE + h:WIDX_LANE + h + 1], (DSA_QBLOCK, LANES))
        m_ref[...] = jnp.full(m_ref.shape, NEG_BIG, F32)
        l_ref[...] = jnp.zeros_like(l_ref)
        acc_ref[...] = jnp.zeros_like(acc_ref)

        n_tiles = (start + DSA_QBLOCK + ts - 1) // ts
        row = lax.broadcasted_iota(I32, (DSA_QBLOCK, ts), 0)
        kcol = lax.broadcasted_iota(I32, (DSA_QBLOCK, ts), 1)
        vis_end = start + CHUNK + CHUNK * (row // CHUNK)

        def score_body(t, carry):
            mx, mn = carry
            base = pl.multiple_of(t * ts, ts)
            kt = ki_ref[pl.ds(base, ts), :]
            rel_all = _nt(qis_ref[...], kt)
            slabs = []
            for c in range(ts // LANES):
                acc = jnp.zeros((DSA_QBLOCK, LANES), F32)
                for h in range(n_heads):
                    rel = rel_all[h * DSA_QBLOCK:(h + 1) * DSA_QBLOCK, c * LANES:(c + 1) * LANES]
                    acc = acc + jnp.maximum(rel, 0.0) * wb_ref[h]
                slabs.append(acc)
            sc = jnp.concatenate(slabs, axis=1)
            sc = jnp.where(sc == 0.0, 0.0, sc)
            adm = kcol + base < vis_end
            lowest = jnp.where(adm, sc, jnp.inf)
            sc = jnp.where(adm, sc, -jnp.inf)
            keys_ref[:, pl.ds(base, ts)] = _order_key(sc)
            for c in range(ts // LANES):
                mx = jnp.maximum(mx, sc[:, c * LANES:(c + 1) * LANES])
                mn = jnp.minimum(mn, lowest[:, c * LANES:(c + 1) * LANES])
            return mx, mn

        mx, mn = lax.fori_loop(0, n_tiles, score_body,
                               (jnp.full((DSA_QBLOCK, LANES), -jnp.inf, F32),
                                jnp.full((DSA_QBLOCK, LANES), jnp.inf, F32)))
        row_max = jnp.broadcast_to(jnp.max(mx, axis=1, keepdims=True), (DSA_QBLOCK, LANES))
        row_min = jnp.broadcast_to(jnp.min(mn, axis=1, keepdims=True), (DSA_QBLOCK, LANES))

        def pad_body(t, carry):
            base = pl.multiple_of(t * ts, ts)
            keys_ref[:, pl.ds(base, ts)] = jnp.full((DSA_QBLOCK, ts), _NEG_INF_KEY, I32)
            return carry

        n_cover = ((start + DSA_QBLOCK + tk - 1) // tk) * (tk // ts)
        lax.fori_loop(n_tiles, n_cover, pad_body, 0)

        zeros_f = jnp.zeros((DSA_QBLOCK, LANES), F32)
        kf = float(topk)

        def count_ge(cand_key):
            def count_body(t, cnt):
                base = pl.multiple_of(t * ts, ts)
                kt = keys_ref[:, pl.ds(base, ts)]
                for c in range(ts // LANES):
                    cnt = cnt + jnp.where(kt[:, c * LANES:(c + 1) * LANES] >= cand_key, 1.0, 0.0)
                return cnt

            cnt = lax.fori_loop(0, n_tiles, count_body, zeros_f)
            return jnp.broadcast_to(jnp.sum(cnt, axis=1, keepdims=True), (DSA_QBLOCK, LANES))

        n_adm = vis_end[:, 0:LANES].astype(F32)
        few = n_adm < kf
        c_max = count_ge(_order_key(row_max))
        at_max = c_max >= kf
        lo0 = jnp.where(few, -jnp.inf, jnp.where(at_max, row_max, row_min))
        done0 = jnp.where(few | at_max | (n_adm == kf), 1.0, 0.0)

        def search_step(st):
            lo, hi, c_lo, f_lo, f_hi, side, done = st
            cand = lo + (hi - lo) * (f_lo / (f_lo - f_hi))
            inside = (cand > lo) & (cand < hi)
            cand = jnp.where(inside, cand, 0.5 * lo + 0.5 * hi)
            stuck = jnp.logical_not((cand > lo) & (cand < hi))
            c = count_ge(_order_key(cand))
            live = jnp.logical_not(stuck) & (done < 0.5)
            up = live & (c >= kf)
            dn = live & (c < kf)
            f_hi = jnp.where(up & (side > 0.5), 0.5 * f_hi, f_hi)
            f_lo = jnp.where(dn & (side < -0.5), 0.5 * f_lo, f_lo)
            side = jnp.where(up, 1.0, jnp.where(dn, -1.0, side))
            lo = jnp.where(up, cand, lo)
            c_lo = jnp.where(up, c, c_lo)
            f_lo = jnp.where(up, c - kf + 0.5, f_lo)
            hi = jnp.where(dn, cand, hi)
            f_hi = jnp.where(dn, c - kf + 0.5, f_hi)
            done = jnp.where(stuck | (c_lo == kf), 1.0, done)
            return (lo, hi, c_lo, f_lo, f_hi, side, done)

        def search_body(st):
            inner = search_step(search_step(st[2:]))
            return (st[0] + 2, (jnp.min(inner[-1]) < 0.5).astype(I32)) + inner

        state = (jnp.int32(0), (jnp.min(done0) < 0.5).astype(I32), lo0, row_max, n_adm,
                 n_adm - kf + 0.5, c_max - kf + 0.5, zeros_f, done0)
        state = lax.while_loop(lambda st: (st[0] < SEARCH_MAX_ITERS) & (st[1] > 0), search_body, state)
        thr_ref[...] = _order_key(state[2])

        @pl.when(state[1] > 0)
        def _():
            sign_bit = jnp.int32(-2 ** 31)

            def bisect_body(it, res):
                cand = res | jnp.left_shift(jnp.int32(1), 31 - it)
                return jnp.where(count_ge(cand ^ sign_bit) >= kf, cand, res)

            res = lax.fori_loop(0, 32, bisect_body, jnp.zeros((DSA_QBLOCK, LANES), I32))
            thr_ref[...] = res ^ sign_bit

        thr = thr_ref[...]
        jcut_ref[...] = jnp.full((DSA_QBLOCK, LANES), 2 ** idx_bits - 1, I32)
        settled = jnp.where((state[4] == kf) | few, 1.0, 0.0)

        @pl.when(jnp.min(settled) < 0.5)
        def _():
            def tally_body(t, carry):
                gt, ge = carry
                base = pl.multiple_of(t * ts, ts)
                kt = keys_ref[:, pl.ds(base, ts)]
                for c in range(ts // LANES):
                    ks = kt[:, c * LANES:(c + 1) * LANES]
                    gt = gt + jnp.where(ks > thr, 1.0, 0.0)
                    ge = ge + jnp.where(ks >= thr, 1.0, 0.0)
                return gt, ge

            gt, ge = lax.fori_loop(0, n_tiles, tally_body, (zeros_f, zeros_f))
            need = float(topk) - jnp.sum(gt, axis=1, keepdims=True)
            n_ge = jnp.sum(ge, axis=1, keepdims=True)

            @pl.when(jnp.max(n_ge) > float(topk))
            def _():
                def cut_body(it, cut):
                    cand = cut | jnp.left_shift(jnp.int32(1), idx_bits - 1 - it)

                    def count_body(t, cnt):
                        base = pl.multiple_of(t * ts, ts)
                        kt = keys_ref[:, pl.ds(base, ts)]
                        for c in range(ts // LANES):
                            idx = lane_q + (base + c * LANES)
                            hit = (kt[:, c * LANES:(c + 1) * LANES] == thr) & (idx < cand)
                            cnt = cnt + jnp.where(hit, 1.0, 0.0)
                        return cnt

                    cnt = lax.fori_loop(0, n_tiles, count_body, zeros_f)
                    return jnp.where(jnp.sum(cnt, axis=1, keepdims=True) <= need, cand, cut)

                jcut_ref[...] = lax.fori_loop(0, idx_bits, cut_body, jnp.zeros((DSA_QBLOCK, LANES), I32))

    kbase = pl.multiple_of(kj * tk, tk)
    key_t = keys_ref[:, pl.ds(kbase, tk)]
    thr = thr_ref[...]
    reps = tk // LANES
    thr_t = jnp.concatenate([thr] * reps, axis=1)
    cut_t = jnp.concatenate([jcut_ref[...]] * reps, axis=1)
    kidx = lax.broadcasted_iota(I32, (DSA_QBLOCK, tk), 1) + kbase
    mask = ((key_t > thr_t) | ((key_t == thr_t) & (kidx < cut_t))) & (key_t > jnp.int32(_NEG_INF_KEY))

    pen = jnp.where(mask, 0.0, NEG_BIG).astype(BF16)
    def qk(p):
        s_ref[p] = _nt(qst_ref[p], k_ref[:, p * LANES:(p + 1) * LANES]).astype(BF16)

    qk(0)

    def lane_slabs(a):
        return [a[:, c * LANES:(c + 1) * LANES] for c in range(reps)]

    for p in range(n_pairs):
        if p + 1 < n_pairs:
            qk(p + 1)
        pes, alphas = [], []
        for half in range(2):
            rows = slice(half * DSA_QBLOCK, (half + 1) * DSA_QBLOCK)
            s = s_ref[p, rows, :] + pen
            row_max = jnp.max(_tree(lane_slabs(s), jnp.maximum).astype(F32), axis=1, keepdims=True)
            m_old = m_ref[p, rows, :]
            m_new = jnp.maximum(m_old, row_max)
            alpha = jnp.exp(m_old - m_new)
            pe = jnp.exp(s - jnp.concatenate([m_new.astype(BF16)] * reps, axis=1))
            row_sum = jnp.sum(_tree(lane_slabs(pe), jnp.add).astype(F32), axis=1, keepdims=True)
            m_ref[p, rows, :] = m_new
            l_ref[p, rows, :] = alpha * l_ref[p, rows, :] + row_sum
            pes.append(pe)
            alphas.append(alpha)
        pv = _dot(jnp.concatenate(pes, axis=0), v_ref[:, p * LANES:(p + 1) * LANES])
        acc_ref[p] = jnp.concatenate(alphas, axis=0) * acc_ref[p] + pv

    last = (start + DSA_QBLOCK - 1) // tk

    @pl.when(kj == last)
    def _():
        for p in range(n_pairs):
            oa = acc_ref[p, 0:DSA_QBLOCK, :] / l_ref[p, 0:DSA_QBLOCK, :]
            ob = acc_ref[p, DSA_QBLOCK:2 * DSA_QBLOCK, :] / l_ref[p, DSA_QBLOCK:2 * DSA_QBLOCK, :]
            o_ref[:, p * LANES:(p + 1) * LANES] = jnp.where(low, oa, ob).astype(o_ref.dtype)


def _dsa(q, k, v, qi, ki, ps, batch, seq, n_idx_heads):
    t, hw = q.shape
    nq = seq // DSA_QBLOCK
    tk = min(1024, seq)
    ts = min(1024, seq)
    nkt = seq // tk
    topk = min(DSA_TOPK_MAX, seq // 4)
    n_heads = hw // HEAD_DIM
    qi_list, kj_list = [], []
    for i in range(nq):
        for j in range((i * DSA_QBLOCK + DSA_QBLOCK - 1) // tk + 1):
            qi_list.append(i)
            kj_list.append(j)
    qi_arr = jnp.asarray(np.array(qi_list, np.int32))
    kj_arr = jnp.asarray(np.array(kj_list, np.int32))
    nsteps = len(qi_list)

    qspec = pl.BlockSpec((DSA_QBLOCK, hw), lambda b, s, qs, ks: (b * nq + qs[s], 0))
    kspec = pl.BlockSpec((tk, hw), lambda b, s, qs, ks: (b * nkt + ks[s], 0))
    grid_spec = pltpu.PrefetchScalarGridSpec(
        num_scalar_prefetch=2,
        grid=(batch, nsteps),
        in_specs=[
            qspec, qspec,
            pl.BlockSpec((DSA_QBLOCK, LANES), lambda b, s, qs, ks: (b * nq + qs[s], 0)),
            pl.BlockSpec((seq, LANES), lambda b, s, qs, ks: (b, 0)),
            kspec, kspec,
        ],
        out_specs=qspec,
        scratch_shapes=[
            pltpu.VMEM((DSA_QBLOCK, seq), I32),
            pltpu.VMEM((DSA_QBLOCK, LANES), I32),
            pltpu.VMEM((DSA_QBLOCK, LANES), I32),
            pltpu.VMEM((n_heads // 2, 2 * DSA_QBLOCK, LANES), BF16),
            pltpu.VMEM((n_idx_heads * DSA_QBLOCK, LANES), BF16),
            pltpu.VMEM((n_idx_heads, DSA_QBLOCK, LANES), F32),
            pltpu.VMEM((n_heads // 2, 2 * DSA_QBLOCK, tk), BF16),
            pltpu.VMEM((n_heads // 2, 2 * DSA_QBLOCK, LANES), F32),
            pltpu.VMEM((n_heads // 2, 2 * DSA_QBLOCK, LANES), F32),
            pltpu.VMEM((n_heads // 2, 2 * DSA_QBLOCK, LANES), F32),
        ],
    )
    w_scale = float(n_idx_heads) ** -0.5 * float(HEAD_DIM) ** -0.5
    return pl.pallas_call(
        functools.partial(_dsa_kernel, topk=topk, w_scale=w_scale, tk=tk, ts=ts, idx_bits=seq.bit_length()),
        grid_spec=grid_spec,
        out_shape=jax.ShapeDtypeStruct((t, hw), BF16),
        compiler_params=_cparams("arbitrary", "arbitrary"),
        name="dsa",
    )(qi_arr, kj_arr, q, qi, ps, ki, k, v)


BAND_QROWS = 2 * CHUNK
BAND_KBLOCKS = (BAND_PREV * CHUNK) // BAND_QROWS + 1


def _band_kernel(*refs, q_scale):
    q_ref = refs[0]
    k_refs = refs[1:1 + BAND_KBLOCKS]
    v_refs = refs[1 + BAND_KBLOCKS:1 + 2 * BAND_KBLOCKS]
    bias_ref = refs[1 + 2 * BAND_KBLOCKS]
    o_ref = refs[2 + 2 * BAND_KBLOCKS]
    kbuf_ref, vbuf_ref = refs[3 + 2 * BAND_KBLOCKS:]
    i = pl.program_id(1)
    d = q_ref.shape[1]
    nkeys = BAND_KBLOCKS * BAND_QROWS
    for blk in range(BAND_KBLOCKS):
        kbuf_ref[blk * BAND_QROWS:(blk + 1) * BAND_QROWS, :] = k_refs[blk][...]
        vbuf_ref[blk * BAND_QROWS:(blk + 1) * BAND_QROWS, :] = v_refs[blk][...]
    kcol = lax.broadcasted_iota(I32, (2 * BAND_QROWS, nkeys), 1)
    valid = kcol >= (BAND_KBLOCKS - 1 - i) * BAND_QROWS
    low = lax.broadcasted_iota(I32, (BAND_QROWS, LANES), 1) < HEAD_DIM
    zero_b = jnp.zeros((), BF16)
    def logits(p):
        sl = slice(p * LANES, (p + 1) * LANES)
        qs = q_ref[:, sl] * q_scale
        qst = jnp.concatenate([jnp.where(low, qs, zero_b), jnp.where(low, zero_b, qs)], axis=0)
        s = _nt(qst, kbuf_ref[:, sl]) + bias_ref[p]
        return jnp.where(valid, s, -jnp.inf).astype(BF16)

    n_slabs = d // LANES
    s_next = logits(0)
    for p in range(n_slabs):
        sl = slice(p * LANES, (p + 1) * LANES)
        s = s_next
        if p + 1 < n_slabs:
            s_next = logits(p + 1)
        slabs = [s[:, c * LANES:(c + 1) * LANES] for c in range(nkeys // LANES)]
        row_max = jnp.max(_tree(slabs, jnp.maximum).astype(F32), axis=1, keepdims=True).astype(BF16)
        es = [jnp.exp(sl_c - row_max) for sl_c in slabs]
        denom = jnp.sum(_tree(es, jnp.add).astype(F32), axis=1, keepdims=True)
        pv = _dot(jnp.concatenate(es, axis=1), vbuf_ref[:, sl]) / denom
        o_ref[:, sl] = jnp.where(low, pv[0:BAND_QROWS], pv[BAND_QROWS:2 * BAND_QROWS]).astype(o_ref.dtype)


def _band_bias(rel_bias):
    nkeys = BAND_KBLOCKS * BAND_QROWS
    tq = np.arange(BAND_QROWS)[:, None]
    kj = np.arange(nkeys)[None, :]
    qc = tq // CHUNK
    kc = kj // CHUNK
    visible = (kc >= qc) & (kc <= qc + BAND_PREV)
    shift = BAND_QROWS - 1
    m = np.arange(nkeys + shift)
    dist = m - shift - (BAND_KBLOCKS - 1) * BAND_QROWS
    ext = rel_bias.astype(F32)[:, np.clip(dist, -REL_PAST, CHUNK - 1) + REL_PAST]
    bias = jnp.stack([ext[:, shift - t:shift - t + nkeys] for t in range(BAND_QROWS)], axis=1)
    return jnp.where(jnp.asarray(visible)[None], bias, -jnp.inf)


def _band(qkv, rel_bias, batch, seq, d):
    t = qkv.shape[0]
    nq = seq // BAND_QROWS
    n_heads = d // HEAD_DIM
    nkeys = BAND_KBLOCKS * BAND_QROWS
    bias = _band_bias(rel_bias).reshape(n_heads // 2, 2 * BAND_QROWS, nkeys)

    def kv_spec(blk, col):
        back = BAND_KBLOCKS - 1 - blk
        return pl.BlockSpec((BAND_QROWS, d), lambda b, i: (b * nq + jnp.maximum(i - back, 0), col))

    in_specs = [pl.BlockSpec((BAND_QROWS, d), lambda b, i: (b * nq + i, 0))]
    in_specs += [kv_spec(blk, 1) for blk in range(BAND_KBLOCKS)]
    in_specs += [kv_spec(blk, 2) for blk in range(BAND_KBLOCKS)]
    in_specs += [pl.BlockSpec((n_heads // 2, 2 * BAND_QROWS, nkeys), lambda b, i: (0, 0, 0))]
    return pl.pallas_call(
        functools.partial(_band_kernel, q_scale=HEAD_DIM ** -0.5),
        grid=(batch, nq),
        in_specs=in_specs,
        out_specs=pl.BlockSpec((BAND_QROWS, d), lambda b, i: (b * nq + i, 0)),
        out_shape=jax.ShapeDtypeStruct((t, d), BF16),
        scratch_shapes=[pltpu.VMEM((nkeys, d), BF16), pltpu.VMEM((nkeys, d), BF16)],
        compiler_params=_cparams("parallel", "arbitrary"),
        name="band",
    )(*([qkv] * (1 + 2 * BAND_KBLOCKS)), bias)


def _rope_tables(seq):
    half = HEAD_DIM // 2
    inv = ROPE_THETA ** (-jnp.arange(half, dtype=F32) / half)
    ang = jnp.arange(seq, dtype=F32)[:, None] * inv[None, :]
    cos = jnp.cos(ang)
    sin = jnp.sin(ang)
    reps = LANES // HEAD_DIM
    cos_t = jnp.tile(jnp.concatenate([cos, cos], axis=1), (1, reps))
    sin_t = jnp.tile(jnp.concatenate([-sin, sin], axis=1), (1, reps))
    return cos_t, sin_t


def _ssd_dsa_mixer(x, norm, w_in, conv_w, conv_b, dt_bias, a_log, d_skip, out_norm, w_out, batch, seq):
    t, d = x.shape
    hw = d // 2
    n_ssd_heads = d // HEAD_DIM
    n_idx_heads = d // 128
    xbc = d + 2 * SSD_GROUPS * SSD_STATE
    sizes = (d, xbc, n_ssd_heads, hw, hw, hw, n_idx_heads * HEAD_DIM, HEAD_DIM, n_idx_heads)
    offs = np.concatenate([[0], np.cumsum(sizes)])
    seg = [w_in[:, offs[i]:offs[i + 1]] for i in range(len(sizes))]
    w_z, w_xbc, w_dt, w_q, w_k, w_v, w_qi, w_ki, w_wi = seg
    w_main = jnp.concatenate([w_z, w_xbc, w_q, w_k, w_v, w_qi], axis=1).astype(BF16)
    w_small = jnp.zeros((d, LANES), F32)
    w_small = w_small.at[:, KIDX_LANE:KIDX_LANE + HEAD_DIM].set(w_ki)
    w_small = w_small.at[:, DT_LANE:DT_LANE + n_ssd_heads].set(w_dt)
    w_small = w_small.at[:, WIDX_LANE:WIDX_LANE + n_idx_heads].set(w_wi)
    pm = _norm_matmul(x, norm, w_main, hw, F32)
    ps = _norm_matmul(x, norm, w_small.astype(BF16), LANES, F32)
    y = _ssd(pm, ps, conv_w, conv_b, dt_bias, a_log, d_skip, out_norm, batch, seq, d)
    cos_t, sin_t = _rope_tables(seq)
    col_q = (d + xbc) // hw
    q, k, v, qi, ki = _dsa_prep(pm, ps, cos_t, sin_t, seq, hw, col_q)
    o = _dsa(q, k, v, qi, ki, ps, batch, seq, n_idx_heads)
    w_out_b = w_out.astype(BF16)
    return _proj_res(x, [y, o], [w_out_b[:d], w_out_b[d:]])


def _band_mixer(x, norm, w_qkv, rel_bias, w_out, batch, seq):
    t, d = x.shape
    qkv = _norm_matmul(x, norm, w_qkv.astype(BF16), d // 2, BF16)
    o = _band(qkv, rel_bias, batch, seq, d)
    return _proj_res(x, [o], [w_out.astype(BF16)])


def kernel(x, ffn1_norm, ffn1_w_in, ffn1_w_out, mix_norm, ab_w_in, ssd_conv_w, ssd_conv_b, ssd_dt_bias, ssd_a_log, ssd_d_skip, ssd_out_norm, ab_w_out, c_w_qkv, c_rel_bias, c_w_out, ffn2_norm, ffn2_w_in, ffn2_w_out, final_norm):
    batch, seq, d = x.shape
    depth = ffn1_norm.shape[0]
    xf = x.reshape(batch * seq, d)
    w1_in, w1_out = ffn1_w_in.astype(BF16), ffn1_w_out.astype(BF16)
    w2_in, w2_out = ffn2_w_in.astype(BF16), ffn2_w_out.astype(BF16)
    for layer in range(depth):
        xf = _ffn(xf, ffn1_norm[layer], w1_in, w1_out, layer)
        i = layer // 2
        if layer % 2 == 0:
            xf = _ssd_dsa_mixer(xf, mix_norm[layer], ab_w_in[i], ssd_conv_w[i], ssd_conv_b[i], ssd_dt_bias[i],
                                ssd_a_log[i], ssd_d_skip[i], ssd_out_norm[i], ab_w_out[i], batch, seq)
        else:
            xf = _band_mixer(xf, mix_norm[layer], c_w_qkv[i], c_rel_bias[i], c_w_out[i], batch, seq)
        fin = final_norm if layer == depth - 1 else None
        xf = _ffn(xf, ffn2_norm[layer], w2_in, w2_out, layer, fin)
    return xf.reshape(batch, seq, d)
```

```python
import functools

import jax
import jax.numpy as jnp
import numpy as np
from jax import lax
from jax.experimental import pallas as pl
from jax.experimental.pallas import tpu as pltpu

F32 = jnp.float32
BF16 = jnp.bfloat16
I32 = jnp.int32

EPS = 1e-5
ROPE_THETA = 10000.0
CHUNK = 64
HEAD_DIM = 64
SSD_GROUPS = 4
SSD_STATE = 128
SSD_CONV = 4
DSA_TOPK_MAX = 256
DSA_QBLOCK = 128
SEARCH_MAX_ITERS = 40
BAND_PREV = 8
REL_PAST = 256

LANES = 128
SUBLANES = 8
VMEM_LIMIT_BYTES = 56 * 1024 * 1024

KIDX_LANE = 0
DT_LANE = 64
WIDX_LANE = 96

NEG_BIG = -1e30
HIGHEST = lax.Precision.HIGHEST


def _cparams(*sem):
    return pltpu.CompilerParams(dimension_semantics=sem, vmem_limit_bytes=VMEM_LIMIT_BYTES)


def _nt(a, b, **kw):
    return lax.dot_general(a, b, (((1,), (1,)), ((), ())), preferred_element_type=F32, **kw)


def _tn(a, b, **kw):
    return lax.dot_general(a, b, (((0,), (0,)), ((), ())), preferred_element_type=F32, **kw)


def _dot(a, b, **kw):
    return jnp.dot(a, b, preferred_element_type=F32, **kw)


def _rms(x, gain):
    ms = jnp.mean(x * x, axis=-1, keepdims=True)
    return x * lax.rsqrt(ms + EPS) * gain


def _silu(x):
    return x * jax.nn.sigmoid(x)


def _tree(parts, op):
    while len(parts) > 1:
        parts = [op(parts[i], parts[i + 1]) for i in range(0, len(parts) - 1, 2)] + parts[len(parts) & ~1:]
    return parts[0]


def _norm_matmul_kernel(x_ref, g_ref, w_ref, o_ref, xn_ref):
    @pl.when(pl.program_id(1) == 0)
    def _():
        xn_ref[...] = _rms(x_ref[...], g_ref[...]).astype(BF16)

    o_ref[...] = _dot(xn_ref[...], w_ref[...]).astype(o_ref.dtype)


def _norm_matmul(x, gain, w, tn, out_dtype):
    t, d = x.shape
    n = w.shape[1]
    tm = min(1024, t)
    return pl.pallas_call(
        _norm_matmul_kernel,
        grid=(t // tm, n // tn),
        in_specs=[
            pl.BlockSpec((tm, d), lambda i, j: (i, 0)),
            pl.BlockSpec((1, d), lambda i, j: (0, 0)),
            pl.BlockSpec((d, tn), lambda i, j: (0, j)),
        ],
        out_specs=pl.BlockSpec((tm, tn), lambda i, j: (i, j)),
        out_shape=jax.ShapeDtypeStruct((t, n), out_dtype),
        scratch_shapes=[pltpu.VMEM((tm, d), BF16)],
        compiler_params=_cparams("parallel", "arbitrary"),
        name="norm_matmul",
    )(x, gain.reshape(1, d), w)


def _ffn_kernel(x_ref, g_ref, wg_ref, wu_ref, wo_ref, fg_ref, o_ref, xn_ref, *, final):
    j = pl.program_id(1)

    @pl.when(j == 0)
    def _():
        xn_ref[...] = _rms(x_ref[...], g_ref[...]).astype(BF16)
        o_ref[...] = jnp.zeros_like(o_ref)

    xn = xn_ref[...]
    g = _dot(xn, wg_ref[...])
    u = _dot(xn, wu_ref[...])
    a = (_silu(g) * u).astype(BF16)
    o_ref[...] += _dot(a, wo_ref[...])

    @pl.when(j == pl.num_programs(1) - 1)
    def _():
        y = x_ref[...] + 0.5 * o_ref[...]
        if final:
            y = _rms(y, fg_ref[...])
        o_ref[...] = y


def _ffn(x, gain, w_in, w_out, layer, final_gain=None):
    t, d = x.shape
    ff = w_out.shape[1]
    tm = min(512, t)
    tf = 512
    nf = ff // tf
    final = final_gain is not None
    fg = (final_gain if final else gain).reshape(1, d)
    return pl.pallas_call(
        functools.partial(_ffn_kernel, final=final),
        grid=(t // tm, nf),
        in_specs=[
            pl.BlockSpec((tm, d), lambda i, j: (i, 0)),
            pl.BlockSpec((1, d), lambda i, j: (0, 0)),
            pl.BlockSpec((None, d, tf), lambda i, j: (layer, 0, j)),
            pl.BlockSpec((None, d, tf), lambda i, j: (layer, 0, j + nf)),
            pl.BlockSpec((None, tf, d), lambda i, j: (layer, j, 0)),
            pl.BlockSpec((1, d), lambda i, j: (0, 0)),
        ],
        out_specs=pl.BlockSpec((tm, d), lambda i, j: (i, 0)),
        out_shape=jax.ShapeDtypeStruct((t, d), F32),
        scratch_shapes=[pltpu.VMEM((tm, d), BF16)],
        compiler_params=_cparams("parallel", "arbitrary"),
        name="ffn",
    )(x, gain.reshape(1, d), w_in, w_in, w_out, fg)


def _proj_res_kernel(*refs, n_lhs):
    x_ref = refs[0]
    a_refs = refs[1:1 + n_lhs]
    w_refs = refs[1 + n_lhs:1 + 2 * n_lhs]
    o_ref = refs[1 + 2 * n_lhs]
    y = x_ref[...]
    for a_ref, w_ref in zip(a_refs, w_refs):
        y = y + _dot(a_ref[...], w_ref[...])
    o_ref[...] = y


def _proj_res(x, lhs, ws):
    t, d = x.shape
    tm = min(1024, t)
    tn = d // 2
    in_specs = [pl.BlockSpec((tm, tn), lambda i, j: (i, j))]
    in_specs += [pl.BlockSpec((tm, a.shape[1]), lambda i, j: (i, 0)) for a in lhs]
    in_specs += [pl.BlockSpec((w.shape[0], tn), lambda i, j: (0, j)) for w in ws]
    return pl.pallas_call(
        functools.partial(_proj_res_kernel, n_lhs=len(lhs)),
        grid=(t // tm, d // tn),
        in_specs=in_specs,
        out_specs=pl.BlockSpec((tm, tn), lambda i, j: (i, j)),
        out_shape=jax.ShapeDtypeStruct((t, d), F32),
        compiler_params=_cparams("parallel", "arbitrary"),
        name="proj_res",
    )(x, *lhs, *ws)


def _rotate_slab(x, cos, sin_signed, first_half):
    fwd = pltpu.roll(x, HEAD_DIM // 2, 1)
    bwd = pltpu.roll(x, LANES - HEAD_DIM // 2, 1)
    return x * cos + jnp.where(first_half, bwd, fwd) * sin_signed


def _dsa_prep_kernel(q_ref, k_ref, v_ref, qi_ref, ps_ref, cos_ref, sin_ref,
                     qo_ref, ko_ref, vo_ref, qio_ref, kio_ref, *, q_scale):
    cos = cos_ref[...]
    sin = sin_ref[...]
    lane = lax.broadcasted_iota(I32, cos.shape, 1)
    first_half = (lane % HEAD_DIM) < HEAD_DIM // 2
    n_slabs = q_ref.shape[1] // LANES
    for c in range(n_slabs):
        sl = slice(c * LANES, (c + 1) * LANES)
        qo_ref[:, sl] = (_rotate_slab(q_ref[:, sl].astype(F32), cos, sin, first_half) * q_scale).astype(BF16)
        ko_ref[:, sl] = _rotate_slab(k_ref[:, sl].astype(F32), cos, sin, first_half).astype(BF16)
        qio_ref[:, sl] = _rotate_slab(qi_ref[:, sl].astype(F32), cos, sin, first_half).astype(BF16)
    vo_ref[...] = v_ref[...].astype(BF16)
    ki = _rotate_slab(ps_ref[...], cos, sin, first_half)
    ki_dup = jnp.where(lane < HEAD_DIM, ki, pltpu.roll(ki, HEAD_DIM, 1))
    kio_ref[...] = ki_dup.astype(BF16)


def _dsa_prep(pm, ps, cos, sin, seq, hw, col_q):
    t = pm.shape[0]
    tm = min(512, seq)
    npos = seq // tm

    def col(c):
        return pl.BlockSpec((tm, hw), lambda i: (i, c))

    pos_spec = pl.BlockSpec((tm, LANES), lambda i: (i % npos, 0))
    row_hw = pl.BlockSpec((tm, hw), lambda i: (i, 0))
    row_l = pl.BlockSpec((tm, LANES), lambda i: (i, 0))
    shp = jax.ShapeDtypeStruct((t, hw), BF16)
    return pl.pallas_call(
        functools.partial(_dsa_prep_kernel, q_scale=HEAD_DIM ** -0.5),
        grid=(t // tm,),
        in_specs=[col(col_q), col(col_q + 1), col(col_q + 2), col(col_q + 3), row_l, pos_spec, pos_spec],
        out_specs=[row_hw, row_hw, row_hw, row_hw, row_l],
        out_shape=[shp, shp, shp, shp, jax.ShapeDtypeStruct((t, LANES), BF16)],
        compiler_params=_cparams("parallel"),
        name="dsa_prep",
    )(pm, pm, pm, pm, ps, cos, sin)


def _ssd_kernel(z_ref, xr_ref, br_ref, cr_ref, ps_ref,
                wx_ref, wb_ref, wc_ref, bx_ref, bb_ref, bc_ref,
                dtb_ref, alog_ref, dsk_ref, gn_ref,
                ex_ref, lt_ref, sel0_ref, sel1_ref, up0_ref, up1_ref,
                o_ref,
                extx_ref, extb_ref, extc_ref, xs_ref, bm_ref, cm_ref, dt_ref, y_ref, st_ref,
                *, rows, n_pairs_per_group):
    r = pl.program_id(1)
    d = z_ref.shape[1]
    gw = SSD_GROUPS * SSD_STATE
    hist = SUBLANES

    @pl.when(r == 0)
    def _():
        extx_ref[0:hist, :] = jnp.zeros((hist, d), F32)
        extb_ref[0:hist, :] = jnp.zeros((hist, gw), F32)
        extc_ref[0:hist, :] = jnp.zeros((hist, gw), F32)
        st_ref[...] = jnp.zeros_like(st_ref)

    @pl.when(r > 0)
    def _():
        extx_ref[0:hist, :] = extx_ref[rows:rows + hist, :]
        extb_ref[0:hist, :] = extb_ref[rows:rows + hist, :]
        extc_ref[0:hist, :] = extc_ref[rows:rows + hist, :]

    def conv_silu(raw_ref, ext_ref, w_ref, b_ref, dst_ref):
        ext_ref[hist:hist + rows, :] = raw_ref[...].astype(F32)
        acc = b_ref[...] + w_ref[SSD_CONV - 1:SSD_CONV, :] * ext_ref[hist:hist + rows, :]
        for back in range(1, SSD_CONV):
            tap = SSD_CONV - 1 - back
            acc = acc + w_ref[tap:tap + 1, :] * ext_ref[hist - back:hist - back + rows, :]
        dst_ref[...] = _silu(acc)

    conv_silu(xr_ref, extx_ref, wx_ref, bx_ref, xs_ref)
    conv_silu(br_ref, extb_ref, wb_ref, bb_ref, bm_ref)
    conv_silu(cr_ref, extc_ref, wc_ref, bc_ref, cm_ref)

    lane = lax.broadcasted_iota(I32, (1, LANES), 1)
    n_heads = d // HEAD_DIM
    head_lane = (lane >= DT_LANE) & (lane < DT_LANE + n_heads)
    dt_all = jax.nn.softplus(ps_ref[...] + dtb_ref[...])
    dt_ref[...] = jnp.where(head_lane, dt_all, 0.0)
    a_vec = jnp.where(head_lane, -jnp.exp(alog_ref[...]), 0.0)

    ex = ex_ref[...]
    lt = lt_ref[...]
    up0 = up0_ref[...]
    up1 = up1_ref[...]
    t_idx = lax.broadcasted_iota(I32, (CHUNK, LANES), 0)
    s_idx = lax.broadcasted_iota(I32, (CHUNK, LANES), 1)
    tril2 = t_idx >= (s_idx % HEAD_DIM)
    low_half = s_idx < HEAD_DIM
    low_half2 = lax.broadcasted_iota(I32, (LANES, LANES), 1) < HEAD_DIM
    top_rows = lax.broadcasted_iota(I32, (LANES, LANES), 0) < HEAD_DIM
    blockdiag = low_half2 == top_rows

    def chunk_body(c, carry):
        rs = pl.ds(pl.multiple_of(c * CHUNK, CHUNK), CHUNK)
        dt = dt_ref[rs, :]
        ac = dt * a_vec
        a_cum = _dot(lt, ac, precision=HIGHEST)
        act0 = _nt(sel0_ref[...], ac, precision=HIGHEST)
        act1 = _nt(sel1_ref[...], ac, precision=HIGHEST)
        a_cum_t = _dot(act0, up0, precision=HIGHEST) + _dot(act1, up1, precision=HIGHEST)
        both = jnp.concatenate([a_cum, dt], axis=0)
        hi = both.astype(BF16)
        rem = both - hi.astype(F32)
        mid = rem.astype(BF16)
        lo = (rem - mid.astype(F32)).astype(BF16)
        both_e = (_dot(hi, ex) + _dot(mid, ex)) + _dot(lo, ex)
        col_all = both_e[0:CHUNK, :]
        dt_e = both_e[CHUNK:2 * CHUNK, :]
        last_e = col_all[CHUNK - 1:CHUNK, :]
        exp_a = jnp.exp(col_all)
        to_end = jnp.exp(last_e - col_all)
        exp_end = jnp.exp(last_e)
        xdt = xs_ref[rs, :] * dt_e
        xdt_b = xdt.astype(BF16)
        xw_b = (xdt * to_end).astype(BF16)
        gwid = d // SSD_GROUPS
        for g in range(SSD_GROUPS):
            gs = slice(g * SSD_STATE, (g + 1) * SSD_STATE)
            gd = slice(g * gwid, (g + 1) * gwid)
            bm_b = bm_ref[rs, gs].astype(BF16)
            cm_b = cm_ref[rs, gs].astype(BF16)
            cb2 = _nt(cm_b, jnp.concatenate([bm_b, bm_b], axis=0))
            st = st_ref[g]
            y_off = _dot(cm_b, st.astype(BF16)) * exp_a[:, gd]
            for jj in range(n_pairs_per_group):
                pidx = g * n_pairs_per_group + jj
                sl = slice(pidx * LANES, (pidx + 1) * LANES)
                seg = col_all[:, sl] - a_cum_t[pidx:pidx + 1, :]
                lmat = (cb2 * jnp.exp(jnp.where(tril2, seg, -jnp.inf))).astype(BF16)
                xp = xdt_b[:, sl]
                rhs = jnp.where(blockdiag, jnp.concatenate([xp, xp], axis=0), jnp.zeros((), BF16))
                y_ref[rs, sl] = _dot(lmat, rhs) + y_off[:, jj * LANES:(jj + 1) * LANES]
            st_ref[g] = st * exp_end[:, gd] + _tn(bm_b, xw_b[:, gd])
        return carry

    lax.fori_loop(0, rows // CHUNK, chunk_body, 0)

    y = (y_ref[...] + xs_ref[...] * dsk_ref[...]) * _silu(z_ref[...].astype(F32))
    gwid = d // SSD_GROUPS
    for g in range(SSD_GROUPS):
        gd = slice(g * gwid, (g + 1) * gwid)
        o_ref[:, gd] = _rms(y[:, gd], gn_ref[:, gd]).astype(o_ref.dtype)


def _ssd_constants(d):
    n_heads = d // HEAD_DIM
    n_pairs = n_heads // 2
    pr = max(SUBLANES, n_pairs)
    ex = np.zeros((LANES, d), np.float32)
    for h in range(n_heads):
        ex[DT_LANE + h, h * HEAD_DIM:(h + 1) * HEAD_DIM] = 1.0
    lt = np.tril(np.ones((CHUNK, CHUNK), np.float32))
    up = np.triu(np.ones((CHUNK, CHUNK), np.float32))
    up0 = np.concatenate([up, np.zeros_like(up)], axis=1)
    up1 = np.concatenate([np.zeros_like(up), up], axis=1)
    sel0 = np.zeros((pr, LANES), np.float32)
    sel1 = np.zeros((pr, LANES), np.float32)
    for j in range(n_pairs):
        sel0[j, DT_LANE + 2 * j] = 1.0
        sel1[j, DT_LANE + 2 * j + 1] = 1.0
    return [jnp.asarray(ex).astype(BF16)] + [jnp.asarray(a) for a in (lt, sel0, sel1, up0, up1)]


def _ssd(pm, ps, conv_w, conv_b, dt_bias, a_log, d_skip, out_norm, batch, seq, d):
    t = pm.shape[0]
    rows = min(256, seq)
    nr = seq // rows
    gw = SSD_GROUPS * SSD_STATE
    n_heads = d // HEAD_DIM
    gwid = d // SSD_GROUPS

    def lane_vec(v):
        return jnp.zeros((1, LANES), F32).at[0, DT_LANE:DT_LANE + n_heads].set(v)

    consts = _ssd_constants(d)
    wx, wb, wc = conv_w[:, :d], conv_w[:, d:d + gw], conv_w[:, d + gw:]
    cb = conv_b.reshape(1, -1)
    bx, bb, bc = cb[:, :d], cb[:, d:d + gw], cb[:, d + gw:]
    dsk = jnp.repeat(d_skip, HEAD_DIM).reshape(1, d)

    def rowblk(width, colblk):
        return pl.BlockSpec((rows, width), lambda b, r: (b * nr + r, colblk))

    def full(a):
        return pl.BlockSpec(a.shape, lambda b, r: (0,) * a.ndim)

    small = [wx, wb, wc, bx, bb, bc, lane_vec(dt_bias), lane_vec(a_log), dsk, out_norm.reshape(1, d)] + consts
    return pl.pallas_call(
        functools.partial(_ssd_kernel, rows=rows, n_pairs_per_group=n_heads // SSD_GROUPS // 2),
        grid=(batch, nr),
        in_specs=[rowblk(d, 0), rowblk(d, 1), rowblk(gw, 2 * d // gw), rowblk(gw, 2 * d // gw + 1),
                  rowblk(LANES, 0)] + [full(a) for a in small],
        out_specs=rowblk(d, 0),
        out_shape=jax.ShapeDtypeStruct((t, d), BF16),
        scratch_shapes=[
            pltpu.VMEM((rows + SUBLANES, d), F32), pltpu.VMEM((rows + SUBLANES, gw), F32),
            pltpu.VMEM((rows + SUBLANES, gw), F32),
            pltpu.VMEM((rows, d), F32), pltpu.VMEM((rows, gw), F32), pltpu.VMEM((rows, gw), F32),
            pltpu.VMEM((rows, LANES), F32), pltpu.VMEM((rows, d), F32),
            pltpu.VMEM((SSD_GROUPS, SSD_STATE, gwid), F32),
        ],
        compiler_params=_cparams("arbitrary", "arbitrary"),
        name="ssd",
    )(pm, pm, pm, pm, ps, *small)


def _order_key(score):
    bits = lax.bitcast_convert_type(score, I32)
    return bits ^ ((bits >> 31) & jnp.int32(0x7FFFFFFF))


_NEG_INF_KEY = int(np.array(-np.inf, np.float32).view(np.int32)) ^ 0x7FFFFFFF


def _dsa_kernel(qi_s, kj_s, q_ref, qi_ref, ps_ref, ki_ref, k_ref, v_ref, o_ref,
                keys_ref, thr_ref, jcut_ref, qst_ref, qis_ref, wb_ref, s_ref, m_ref, l_ref, acc_ref,
                *, topk, w_scale, tk, ts, idx_bits):
    step = pl.program_id(1)
    qblk = qi_s[step]
    kj = kj_s[step]
    start = qblk * DSA_QBLOCK
    hw = q_ref.shape[1]
    n_heads = hw // HEAD_DIM
    n_pairs = n_heads // 2
    lane_q = lax.broadcasted_iota(I32, (DSA_QBLOCK, LANES), 1)
    low = lane_q < HEAD_DIM

    @pl.when(kj == 0)
    def _():
        zero_b = jnp.zeros((), BF16)
        for p in range(n_pairs):
            sl = slice(p * LANES, (p + 1) * LANES)
            qs = q_ref[:, sl]
            qst_ref[p, 0:DSA_QBLOCK, :] = jnp.where(low, qs, zero_b)
            qst_ref[p, DSA_QBLOCK:2 * DSA_QBLOCK, :] = jnp.where(low, zero_b, qs)
            qis = qi_ref[:, sl]
            qis_ref[(2 * p) * DSA_QBLOCK:(2 * p + 1) * DSA_QBLOCK, :] = jnp.where(low, qis, zero_b)
            qis_ref[(2 * p + 1) * DSA_QBLOCK:(2 * p + 2) * DSA_QBLOCK, :] = jnp.where(low, zero_b, qis)
        w = ps_ref[...] * w_scale
        for h in range(n_heads):
            wb_ref[h] = jnp.broadcast_to(w[:, WIDX_LANE + h:WIDX_LANE + h + 1], (DSA_QBLOCK, LANES))
        m_ref[...] = jnp.full(m_ref.shape, NEG_BIG, F32)
        l_ref[...] = jnp.zeros_like(l_ref)
        acc_ref[...] = jnp.zeros_like(acc_ref)

        n_tiles = (start + DSA_QBLOCK + ts - 1) // ts
        row = lax.broadcasted_iota(I32, (DSA_QBLOCK, ts), 0)
        kcol = lax.broadcasted_iota(I32, (DSA_QBLOCK, ts), 1)
        vis_end = start + CHUNK + CHUNK * (row // CHUNK)

        def score_body(t, carry):
            mx, mn = carry
            base = pl.multiple_of(t * ts, ts)
            kt = ki_ref[pl.ds(base, ts), :]
            rel_all = _nt(qis_ref[...], kt)
            slabs = []
            for c in range(ts // LANES):
                acc = jnp.zeros((DSA_QBLOCK, LANES), F32)
                for h in range(n_heads):
                    rel = rel_all[h * DSA_QBLOCK:(h + 1) * DSA_QBLOCK, c * LANES:(c + 1) * LANES]
                    acc = acc + jnp.maximum(rel, 0.0) * wb_ref[h]
                slabs.append(acc)
            sc = jnp.concatenate(slabs, axis=1)
            sc = jnp.where(sc == 0.0, 0.0, sc)
            adm = kcol + base < vis_end
            lowest = jnp.where(adm, sc, jnp.inf)
            sc = jnp.where(adm, sc, -jnp.inf)
            keys_ref[:, pl.ds(base, ts)] = _order_key(sc)
            for c in range(ts // LANES):
                mx = jnp.maximum(mx, sc[:, c * LANES:(c + 1) * LANES])
                mn = jnp.minimum(mn, lowest[:, c * LANES:(c + 1) * LANES])
            return mx, mn

        mx, mn = lax.fori_loop(0, n_tiles, score_body,
                               (jnp.full((DSA_QBLOCK, LANES), -jnp.inf, F32),
                                jnp.full((DSA_QBLOCK, LANES), jnp.inf, F32)))
        row_max = jnp.broadcast_to(jnp.max(mx, axis=1, keepdims=True), (DSA_QBLOCK, LANES))
        row_min = jnp.broadcast_to(jnp.min(mn, axis=1, keepdims=True), (DSA_QBLOCK, LANES))

        def pad_body(t, carry):
            base = pl.multiple_of(t * ts, ts)
            keys_ref[:, pl.ds(base, ts)] = jnp.full((DSA_QBLOCK, ts), _NEG_INF_KEY, I32)
            return carry

        n_cover = ((start + DSA_QBLOCK + tk - 1) // tk) * (tk // ts)
        lax.fori_loop(n_tiles, n_cover, pad_body, 0)

        zeros_f = jnp.zeros((DSA_QBLOCK, LANES), F32)
        kf = float(topk)

        def count_ge(cand_key):
            def count_body(t, cnt):
                base = pl.multiple_of(t * ts, ts)
                kt = keys_ref[:, pl.ds(base, ts)]
                for c in range(ts // LANES):
                    cnt = cnt + jnp.where(kt[:, c * LANES:(c + 1) * LANES] >= cand_key, 1.0, 0.0)
                return cnt

            cnt = lax.fori_loop(0, n_tiles, count_body, zeros_f)
            return jnp.broadcast_to(jnp.sum(cnt, axis=1, keepdims=True), (DSA_QBLOCK, LANES))

        n_adm = vis_end[:, 0:LANES].astype(F32)
        few = n_adm < kf
        c_max = count_ge(_order_key(row_max))
        at_max = c_max >= kf
        lo0 = jnp.where(few, -jnp.inf, jnp.where(at_max, row_max, row_min))
        done0 = jnp.where(few | at_max | (n_adm == kf), 1.0, 0.0)

        def search_step(st):
            lo, hi, c_lo, f_lo, f_hi, side, done = st
            cand = lo + (hi - lo) * (f_lo / (f_lo - f_hi))
            inside = (cand > lo) & (cand < hi)
            cand = jnp.where(inside, cand, 0.5 * lo + 0.5 * hi)
            stuck = jnp.logical_not((cand > lo) & (cand < hi))
            c = count_ge(_order_key(cand))
            live = jnp.logical_not(stuck) & (done < 0.5)
            up = live & (c >= kf)
            dn = live & (c < kf)
            f_hi = jnp.where(up & (side > 0.5), 0.5 * f_hi, f_hi)
            f_lo = jnp.where(dn & (side < -0.5), 0.5 * f_lo, f_lo)
            side = jnp.where(up, 1.0, jnp.where(dn, -1.0, side))
            lo = jnp.where(up, cand, lo)
            c_lo = jnp.where(up, c, c_lo)
            f_lo = jnp.where(up, c - kf + 0.5, f_lo)
            hi = jnp.where(dn, cand, hi)
            f_hi = jnp.where(dn, c - kf + 0.5, f_hi)
            done = jnp.where(stuck | (c_lo == kf), 1.0, done)
            return (lo, hi, c_lo, f_lo, f_hi, side, done)

        def search_body(st):
            inner = search_step(search_step(st[2:]))
            return (st[0] + 2, (jnp.min(inner[-1]) < 0.5).astype(I32)) + inner

        state = (jnp.int32(0), (jnp.min(done0) < 0.5).astype(I32), lo0, row_max, n_adm,
                 n_adm - kf + 0.5, c_max - kf + 0.5, zeros_f, done0)
        state = lax.while_loop(lambda st: (st[0] < SEARCH_MAX_ITERS) & (st[1] > 0), search_body, state)
        thr_ref[...] = _order_key(state[2])

        @pl.when(state[1] > 0)
        def _():
            sign_bit = jnp.int32(-2 ** 31)

            def bisect_body(it, res):
                cand = res | jnp.left_shift(jnp.int32(1), 31 - it)
                return jnp.where(count_ge(cand ^ sign_bit) >= kf, cand, res)

            res = lax.fori_loop(0, 32, bisect_body, jnp.zeros((DSA_QBLOCK, LANES), I32))
            thr_ref[...] = res ^ sign_bit

        thr = thr_ref[...]
        jcut_ref[...] = jnp.full((DSA_QBLOCK, LANES), 2 ** idx_bits - 1, I32)
        settled = jnp.where((state[4] == kf) | few, 1.0, 0.0)

        @pl.when(jnp.min(settled) < 0.5)
        def _():
            def tally_body(t, carry):
                gt, ge = carry
                base = pl.multiple_of(t * ts, ts)
                kt = keys_ref[:, pl.ds(base, ts)]
                for c in range(ts // LANES):
                    ks = kt[:, c * LANES:(c + 1) * LANES]
                    gt = gt + jnp.where(ks > thr, 1.0, 0.0)
                    ge = ge + jnp.where(ks >= thr, 1.0, 0.0)
                return gt, ge

            gt, ge = lax.fori_loop(0, n_tiles, tally_body, (zeros_f, zeros_f))
            need = float(topk) - jnp.sum(gt, axis=1, keepdims=True)
            n_ge = jnp.sum(ge, axis=1, keepdims=True)

            @pl.when(jnp.max(n_ge) > float(topk))
            def _():
                def cut_body(it, cut):
                    cand = cut | jnp.left_shift(jnp.int32(1), idx_bits - 1 - it)

                    def count_body(t, cnt):
                        base = pl.multiple_of(t * ts, ts)
                        kt = keys_ref[:, pl.ds(base, ts)]
                        for c in range(ts // LANES):
                            idx = lane_q + (base + c * LANES)
                            hit = (kt[:, c * LANES:(c + 1) * LANES] == thr) & (idx < cand)
                            cnt = cnt + jnp.where(hit, 1.0, 0.0)
                        return cnt

                    cnt = lax.fori_loop(0, n_tiles, count_body, zeros_f)
                    return jnp.where(jnp.sum(cnt, axis=1, keepdims=True) <= need, cand, cut)

                jcut_ref[...] = lax.fori_loop(0, idx_bits, cut_body, jnp.zeros((DSA_QBLOCK, LANES), I32))

    kbase = pl.multiple_of(kj * tk, tk)
    key_t = keys_ref[:, pl.ds(kbase, tk)]
    thr = thr_ref[...]
    reps = tk // LANES
    thr_t = jnp.concatenate([thr] * reps, axis=1)
    cut_t = jnp.concatenate([jcut_ref[...]] * reps, axis=1)
    kidx = lax.broadcasted_iota(I32, (DSA_QBLOCK, tk), 1) + kbase
    mask = ((key_t > thr_t) | ((key_t == thr_t) & (kidx < cut_t))) & (key_t > jnp.int32(_NEG_INF_KEY))

    pen = jnp.where(mask, 0.0, NEG_BIG).astype(BF16)
    def qk(p):
        s_ref[p] = _nt(qst_ref[p], k_ref[:, p * LANES:(p + 1) * LANES]).astype(BF16)

    qk(0)

    def lane_slabs(a):
        return [a[:, c * LANES:(c + 1) * LANES] for c in range(reps)]

    for p in range(n_pairs):
        if p + 1 < n_pairs:
            qk(p + 1)
        pes, alphas = [], []
        for half in range(2):
            rows = slice(half * DSA_QBLOCK, (half + 1) * DSA_QBLOCK)
            s = s_ref[p, rows, :] + pen
            row_max = jnp.max(_tree(lane_slabs(s), jnp.maximum).astype(F32), axis=1, keepdims=True)
            m_old = m_ref[p, rows, :]
            m_new = jnp.maximum(m_old, row_max)
            alpha = jnp.exp(m_old - m_new)
            pe = jnp.exp(s - jnp.concatenate([m_new.astype(BF16)] * reps, axis=1))
            row_sum = jnp.sum(_tree(lane_slabs(pe), jnp.add).astype(F32), axis=1, keepdims=True)
            m_ref[p, rows, :] = m_new
            l_ref[p, rows, :] = alpha * l_ref[p, rows, :] + row_sum
            pes.append(pe)
            alphas.append(alpha)
        pv = _dot(jnp.concatenate(pes, axis=0), v_ref[:, p * LANES:(p + 1) * LANES])
        acc_ref[p] = jnp.concatenate(alphas, axis=0) * acc_ref[p] + pv

    last = (start + DSA_QBLOCK - 1) // tk

    @pl.when(kj == last)
    def _():
        for p in range(n_pairs):
            oa = acc_ref[p, 0:DSA_QBLOCK, :] / l_ref[p, 0:DSA_QBLOCK, :]
            ob = acc_ref[p, DSA_QBLOCK:2 * DSA_QBLOCK, :] / l_ref[p, DSA_QBLOCK:2 * DSA_QBLOCK, :]
            o_ref[:, p * LANES:(p + 1) * LANES] = jnp.where(low, oa, ob).astype(o_ref.dtype)


def _dsa(q, k, v, qi, ki, ps, batch, seq, n_idx_heads):
    t, hw = q.shape
    nq = seq // DSA_QBLOCK
    tk = min(1024, seq)
    ts = min(1024, seq)
    nkt = seq // tk
    topk = min(DSA_TOPK_MAX, seq // 4)
    n_heads = hw // HEAD_DIM
    qi_list, kj_list = [], []
    for i in range(nq):
        for j in range((i * DSA_QBLOCK + DSA_QBLOCK - 1) // tk + 1):
            qi_list.append(i)
            kj_list.append(j)
    qi_arr = jnp.asarray(np.array(qi_list, np.int32))
    kj_arr = jnp.asarray(np.array(kj_list, np.int32))
    nsteps = len(qi_list)

    qspec = pl.BlockSpec((DSA_QBLOCK, hw), lambda b, s, qs, ks: (b * nq + qs[s], 0))
    kspec = pl.BlockSpec((tk, hw), lambda b, s, qs, ks: (b * nkt + ks[s], 0))
    grid_spec = pltpu.PrefetchScalarGridSpec(
        num_scalar_prefetch=2,
        grid=(batch, nsteps),
        in_specs=[
            qspec, qspec,
            pl.BlockSpec((DSA_QBLOCK, LANES), lambda b, s, qs, ks: (b * nq + qs[s], 0)),
            pl.BlockSpec((seq, LANES), lambda b, s, qs, ks: (b, 0)),
            kspec, kspec,
        ],
        out_specs=qspec,
        scratch_shapes=[
            pltpu.VMEM((DSA_QBLOCK, seq), I32),
            pltpu.VMEM((DSA_QBLOCK, LANES), I32),
            pltpu.VMEM((DSA_QBLOCK, LANES), I32),
            pltpu.VMEM((n_heads // 2, 2 * DSA_QBLOCK, LANES), BF16),
            pltpu.VMEM((n_idx_heads * DSA_QBLOCK, LANES), BF16),
            pltpu.VMEM((n_idx_heads, DSA_QBLOCK, LANES), F32),
            pltpu.VMEM((n_heads // 2, 2 * DSA_QBLOCK, tk), BF16),
            pltpu.VMEM((n_heads // 2, 2 * DSA_QBLOCK, LANES), F32),
            pltpu.VMEM((n_heads // 2, 2 * DSA_QBLOCK, LANES), F32),
            pltpu.VMEM((n_heads // 2, 2 * DSA_QBLOCK, LANES), F32),
        ],
    )
    w_scale = float(n_idx_heads) ** -0.5 * float(HEAD_DIM) ** -0.5
    return pl.pallas_call(
        functools.partial(_dsa_kernel, topk=topk, w_scale=w_scale, tk=tk, ts=ts, idx_bits=seq.bit_length()),
        grid_spec=grid_spec,
        out_shape=jax.ShapeDtypeStruct((t, hw), BF16),
        compiler_params=_cparams("arbitrary", "arbitrary"),
        name="dsa",
    )(qi_arr, kj_arr, q, qi, ps, ki, k, v)


BAND_QROWS = 2 * CHUNK
BAND_KBLOCKS = (BAND_PREV * CHUNK) // BAND_QROWS + 1


def _band_kernel(*refs, q_scale):
    q_ref = refs[0]
    k_refs = refs[1:1 + BAND_KBLOCKS]
    v_refs = refs[1 + BAND_KBLOCKS:1 + 2 * BAND_KBLOCKS]
    bias_ref = refs[1 + 2 * BAND_KBLOCKS]
    o_ref = refs[2 + 2 * BAND_KBLOCKS]
    kbuf_ref, vbuf_ref = refs[3 + 2 * BAND_KBLOCKS:]
    i = pl.program_id(1)
    d = q_ref.shape[1]
    nkeys = BAND_KBLOCKS * BAND_QROWS
    for blk in range(BAND_KBLOCKS):
        kbuf_ref[blk * BAND_QROWS:(blk + 1) * BAND_QROWS, :] = k_refs[blk][...]
        vbuf_ref[blk * BAND_QROWS:(blk + 1) * BAND_QROWS, :] = v_refs[blk][...]
    kcol = lax.broadcasted_iota(I32, (2 * BAND_QROWS, nkeys), 1)
    valid = kcol >= (BAND_KBLOCKS - 1 - i) * BAND_QROWS
    low = lax.broadcasted_iota(I32, (BAND_QROWS, LANES), 1) < HEAD_DIM
    zero_b = jnp.zeros((), BF16)
    def logits(p):
        sl = slice(p * LANES, (p + 1) * LANES)
        qs = q_ref[:, sl] * q_scale
        qst = jnp.concatenate([jnp.where(low, qs, zero_b), jnp.where(low, zero_b, qs)], axis=0)
        s = _nt(qst, kbuf_ref[:, sl]) + bias_ref[p]
        return jnp.where(valid, s, -jnp.inf).astype(BF16)

    n_slabs = d // LANES
    s_next = logits(0)
    for p in range(n_slabs):
        sl = slice(p * LANES, (p + 1) * LANES)
        s = s_next
        if p + 1 < n_slabs:
            s_next = logits(p + 1)
        slabs = [s[:, c * LANES:(c + 1) * LANES] for c in range(nkeys // LANES)]
        row_max = jnp.max(_tree(slabs, jnp.maximum).astype(F32), axis=1, keepdims=True).astype(BF16)
        es = [jnp.exp(sl_c - row_max) for sl_c in slabs]
        denom = jnp.sum(_tree(es, jnp.add).astype(F32), axis=1, keepdims=True)
        pv = _dot(jnp.concatenate(es, axis=1), vbuf_ref[:, sl]) / denom
        o_ref[:, sl] = jnp.where(low, pv[0:BAND_QROWS], pv[BAND_QROWS:2 * BAND_QROWS]).astype(o_ref.dtype)


def _band_bias(rel_bias):
    nkeys = BAND_KBLOCKS * BAND_QROWS
    tq = np.arange(BAND_QROWS)[:, None]
    kj = np.arange(nkeys)[None, :]
    qc = tq // CHUNK
    kc = kj // CHUNK
    visible = (kc >= qc) & (kc <= qc + BAND_PREV)
    shift = BAND_QROWS - 1
    m = np.arange(nkeys + shift)
    dist = m - shift - (BAND_KBLOCKS - 1) * BAND_QROWS
    ext = rel_bias.astype(F32)[:, np.clip(dist, -REL_PAST, CHUNK - 1) + REL_PAST]
    bias = jnp.stack([ext[:, shift - t:shift - t + nkeys] for t in range(BAND_QROWS)], axis=1)
    return jnp.where(jnp.asarray(visible)[None], bias, -jnp.inf)


def _band(qkv, rel_bias, batch, seq, d):
    t = qkv.shape[0]
    nq = seq // BAND_QROWS
    n_heads = d // HEAD_DIM
    nkeys = BAND_KBLOCKS * BAND_QROWS
    bias = _band_bias(rel_bias).reshape(n_heads // 2, 2 * BAND_QROWS, nkeys)

    def kv_spec(blk, col):
        back = BAND_KBLOCKS - 1 - blk
        return pl.BlockSpec((BAND_QROWS, d), lambda b, i: (b * nq + jnp.maximum(i - back, 0), col))

    in_specs = [pl.BlockSpec((BAND_QROWS, d), lambda b, i: (b * nq + i, 0))]
    in_specs += [kv_spec(blk, 1) for blk in range(BAND_KBLOCKS)]
    in_specs += [kv_spec(blk, 2) for blk in range(BAND_KBLOCKS)]
    in_specs += [pl.BlockSpec((n_heads // 2, 2 * BAND_QROWS, nkeys), lambda b, i: (0, 0, 0))]
    return pl.pallas_call(
        functools.partial(_band_kernel, q_scale=HEAD_DIM ** -0.5),
        grid=(batch, nq),
        in_specs=in_specs,
        out_specs=pl.BlockSpec((BAND_QROWS, d), lambda b, i: (b * nq + i, 0)),
        out_shape=jax.ShapeDtypeStruct((t, d), BF16),
        scratch_shapes=[pltpu.VMEM((nkeys, d), BF16), pltpu.VMEM((nkeys, d), BF16)],
        compiler_params=_cparams("parallel", "arbitrary"),
        name="band",
    )(*([qkv] * (1 + 2 * BAND_KBLOCKS)), bias)


def _rope_tables(seq):
    half = HEAD_DIM // 2
    inv = ROPE_THETA ** (-jnp.arange(half, dtype=F32) / half)
    ang = jnp.arange(seq, dtype=F32)[:, None] * inv[None, :]
    cos = jnp.cos(ang)
    sin = jnp.sin(ang)
    reps = LANES // HEAD_DIM
    cos_t = jnp.tile(jnp.concatenate([cos, cos], axis=1), (1, reps))
    sin_t = jnp.tile(jnp.concatenate([-sin, sin], axis=1), (1, reps))
    return cos_t, sin_t


def _ssd_dsa_mixer(x, norm, w_in, conv_w, conv_b, dt_bias, a_log, d_skip, out_norm, w_out, batch, seq):
    t, d = x.shape
    hw = d // 2
    n_ssd_heads = d // HEAD_DIM
    n_idx_heads = d // 128
    xbc = d + 2 * SSD_GROUPS * SSD_STATE
    sizes = (d, xbc, n_ssd_heads, hw, hw, hw, n_idx_heads * HEAD_DIM, HEAD_DIM, n_idx_heads)
    offs = np.concatenate([[0], np.cumsum(sizes)])
    seg = [w_in[:, offs[i]:offs[i + 1]] for i in range(len(sizes))]
    w_z, w_xbc, w_dt, w_q, w_k, w_v, w_qi, w_ki, w_wi = seg
    w_main = jnp.concatenate([w_z, w_xbc, w_q, w_k, w_v, w_qi], axis=1).astype(BF16)
    w_small = jnp.zeros((d, LANES), F32)
    w_small = w_small.at[:, KIDX_LANE:KIDX_LANE + HEAD_DIM].set(w_ki)
    w_small = w_small.at[:, DT_LANE:DT_LANE + n_ssd_heads].set(w_dt)
    w_small = w_small.at[:, WIDX_LANE:WIDX_LANE + n_idx_heads].set(w_wi)
    pm = _norm_matmul(x, norm, w_main, hw, BF16)
    ps = _norm_matmul(x, norm, w_small.astype(BF16), LANES, F32)
    y = _ssd(pm, ps, conv_w, conv_b, dt_bias, a_log, d_skip, out_norm, batch, seq, d)
    cos_t, sin_t = _rope_tables(seq)
    col_q = (d + xbc) // hw
    q, k, v, qi, ki = _dsa_prep(pm, ps, cos_t, sin_t, seq, hw, col_q)
    o = _dsa(q, k, v, qi, ki, ps, batch, seq, n_idx_heads)
    w_out_b = w_out.astype(BF16)
    return _proj_res(x, [y, o], [w_out_b[:d], w_out_b[d:]])


def _band_mixer(x, norm, w_qkv, rel_bias, w_out, batch, seq):
    t, d = x.shape
    qkv = _norm_matmul(x, norm, w_qkv.astype(BF16), d // 2, BF16)
    o = _band(qkv, rel_bias, batch, seq, d)
    return _proj_res(x, [o], [w_out.astype(BF16)])


def kernel(x, ffn1_norm, ffn1_w_in, ffn1_w_out, mix_norm, ab_w_in, ssd_conv_w, ssd_conv_b, ssd_dt_bias, ssd_a_log, ssd_d_skip, ssd_out_norm, ab_w_out, c_w_qkv, c_rel_bias, c_w_out, ffn2_norm, ffn2_w_in, ffn2_w_out, final_norm):
    batch, seq, d = x.shape
    depth = ffn1_norm.shape[0]
    xf = x.reshape(batch * seq, d)
    w1_in, w1_out = ffn1_w_in.astype(BF16), ffn1_w_out.astype(BF16)
    w2_in, w2_out = ffn2_w_in.astype(BF16), ffn2_w_out.astype(BF16)
    for layer in range(depth):
        xf = _ffn(xf, ffn1_norm[layer], w1_in, w1_out, layer)
        i = layer // 2
        if layer % 2 == 0:
            xf = _ssd_dsa_mixer(xf, mix_norm[layer], ab_w_in[i], ssd_conv_w[i], ssd_conv_b[i], ssd_dt_bias[i],
                                ssd_a_log[i], ssd_d_skip[i], ssd_out_norm[i], ab_w_out[i], batch, seq)
        else:
            xf = _band_mixer(xf, mix_norm[layer], c_w_qkv[i], c_rel_bias[i], c_w_out[i], batch, seq)
        fin = final_norm if layer == depth - 1 else None
        xf = _ffn(xf, ffn2_norm[layer], w2_in, w2_out, layer, fin)
    return xf.reshape(batch, seq, d)
```

```python
import functools

import jax
import jax.numpy as jnp
import numpy as np
from jax import lax
from jax.experimental import pallas as pl
from jax.experimental.pallas import tpu as pltpu

F32 = jnp.float32
BF16 = jnp.bfloat16
I32 = jnp.int32

EPS = 1e-5
ROPE_THETA = 10000.0
CHUNK = 64
HEAD_DIM = 64
SSD_GROUPS = 4
SSD_STATE = 128
SSD_CONV = 4
DSA_TOPK_MAX = 256
DSA_QBLOCK = 128
SEARCH_MAX_ITERS = 40
LOG_COUNT_MIN_KEYS = 1536.0
BAND_PREV = 8
REL_PAST = 256

LANES = 128
SUBLANES = 8
VMEM_LIMIT_BYTES = 56 * 1024 * 1024

KIDX_LANE = 0
DT_LANE = 64
WIDX_LANE = 96

NEG_BIG = -1e30
HIGHEST = lax.Precision.HIGHEST


def _cparams(*sem):
    return pltpu.CompilerParams(dimension_semantics=sem, vmem_limit_bytes=VMEM_LIMIT_BYTES)


def _nt(a, b, **kw):
    return lax.dot_general(a, b, (((1,), (1,)), ((), ())), preferred_element_type=F32, **kw)


def _tn(a, b, **kw):
    return lax.dot_general(a, b, (((0,), (0,)), ((), ())), preferred_element_type=F32, **kw)


def _dot(a, b, **kw):
    return jnp.dot(a, b, preferred_element_type=F32, **kw)


def _rms(x, gain):
    ms = jnp.mean(x * x, axis=-1, keepdims=True)
    return x * lax.rsqrt(ms + EPS) * gain


def _silu(x):
    return x * jax.nn.sigmoid(x)


def _tree(parts, op):
    while len(parts) > 1:
        parts = [op(parts[i], parts[i + 1]) for i in range(0, len(parts) - 1, 2)] + parts[len(parts) & ~1:]
    return parts[0]


def _norm_matmul_kernel(x_ref, g_ref, w_ref, o_ref, xn_ref):
    @pl.when(pl.program_id(1) == 0)
    def _():
        xn_ref[...] = _rms(x_ref[...], g_ref[...]).astype(BF16)

    o_ref[...] = _dot(xn_ref[...], w_ref[...]).astype(o_ref.dtype)


def _norm_matmul(x, gain, w, tn, out_dtype):
    t, d = x.shape
    n = w.shape[1]
    tm = min(1024, t)
    return pl.pallas_call(
        _norm_matmul_kernel,
        grid=(t // tm, n // tn),
        in_specs=[
            pl.BlockSpec((tm, d), lambda i, j: (i, 0)),
            pl.BlockSpec((1, d), lambda i, j: (0, 0)),
            pl.BlockSpec((d, tn), lambda i, j: (0, j)),
        ],
        out_specs=pl.BlockSpec((tm, tn), lambda i, j: (i, j)),
        out_shape=jax.ShapeDtypeStruct((t, n), out_dtype),
        scratch_shapes=[pltpu.VMEM((tm, d), BF16)],
        compiler_params=_cparams("parallel", "arbitrary"),
        name="norm_matmul",
    )(x, gain.reshape(1, d), w)


def _ffn_kernel(x_ref, g_ref, wg_ref, wu_ref, wo_ref, fg_ref, o_ref, xn_ref, *, final):
    j = pl.program_id(1)

    @pl.when(j == 0)
    def _():
        xn_ref[...] = _rms(x_ref[...], g_ref[...]).astype(BF16)
        o_ref[...] = jnp.zeros_like(o_ref)

    xn = xn_ref[...]
    g = _dot(xn, wg_ref[...])
    u = _dot(xn, wu_ref[...])
    a = (_silu(g) * u).astype(BF16)
    o_ref[...] += _dot(a, wo_ref[...])

    @pl.when(j == pl.num_programs(1) - 1)
    def _():
        y = x_ref[...] + 0.5 * o_ref[...]
        if final:
            y = _rms(y, fg_ref[...])
        o_ref[...] = y


def _ffn(x, gain, w_in, w_out, layer, final_gain=None):
    t, d = x.shape
    ff = w_out.shape[1]
    tm = min(512, t)
    tf = 512
    nf = ff // tf
    final = final_gain is not None
    fg = (final_gain if final else gain).reshape(1, d)
    return pl.pallas_call(
        functools.partial(_ffn_kernel, final=final),
        grid=(t // tm, nf),
        in_specs=[
            pl.BlockSpec((tm, d), lambda i, j: (i, 0)),
            pl.BlockSpec((1, d), lambda i, j: (0, 0)),
            pl.BlockSpec((None, d, tf), lambda i, j: (layer, 0, j)),
            pl.BlockSpec((None, d, tf), lambda i, j: (layer, 0, j + nf)),
            pl.BlockSpec((None, tf, d), lambda i, j: (layer, j, 0)),
            pl.BlockSpec((1, d), lambda i, j: (0, 0)),
        ],
        out_specs=pl.BlockSpec((tm, d), lambda i, j: (i, 0)),
        out_shape=jax.ShapeDtypeStruct((t, d), F32),
        scratch_shapes=[pltpu.VMEM((tm, d), BF16)],
        compiler_params=_cparams("parallel", "arbitrary"),
        name="ffn",
    )(x, gain.reshape(1, d), w_in, w_in, w_out, fg)


def _proj_res_kernel(*refs, n_lhs):
    x_ref = refs[0]
    a_refs = refs[1:1 + n_lhs]
    w_refs = refs[1 + n_lhs:1 + 2 * n_lhs]
    o_ref = refs[1 + 2 * n_lhs]
    y = x_ref[...]
    for a_ref, w_ref in zip(a_refs, w_refs):
        y = y + _dot(a_ref[...], w_ref[...])
    o_ref[...] = y


def _proj_res(x, lhs, ws):
    t, d = x.shape
    tm = min(1024, t)
    tn = d // 2
    in_specs = [pl.BlockSpec((tm, tn), lambda i, j: (i, j))]
    in_specs += [pl.BlockSpec((tm, a.shape[1]), lambda i, j: (i, 0)) for a in lhs]
    in_specs += [pl.BlockSpec((w.shape[0], tn), lambda i, j: (0, j)) for w in ws]
    return pl.pallas_call(
        functools.partial(_proj_res_kernel, n_lhs=len(lhs)),
        grid=(t // tm, d // tn),
        in_specs=in_specs,
        out_specs=pl.BlockSpec((tm, tn), lambda i, j: (i, j)),
        out_shape=jax.ShapeDtypeStruct((t, d), F32),
        compiler_params=_cparams("parallel", "arbitrary"),
        name="proj_res",
    )(x, *lhs, *ws)


def _rotate_slab(x, cos, sin_signed, first_half):
    fwd = pltpu.roll(x, HEAD_DIM // 2, 1)
    bwd = pltpu.roll(x, LANES - HEAD_DIM // 2, 1)
    return x * cos + jnp.where(first_half, bwd, fwd) * sin_signed


def _dsa_prep_kernel(q_ref, k_ref, v_ref, qi_ref, ps_ref, cos_ref, sin_ref,
                     qo_ref, ko_ref, vo_ref, qio_ref, kio_ref, *, q_scale):
    cos = cos_ref[...]
    sin = sin_ref[...]
    lane = lax.broadcasted_iota(I32, cos.shape, 1)
    first_half = (lane % HEAD_DIM) < HEAD_DIM // 2
    n_slabs = q_ref.shape[1] // LANES
    for c in range(n_slabs):
        sl = slice(c * LANES, (c + 1) * LANES)
        qo_ref[:, sl] = (_rotate_slab(q_ref[:, sl], cos, sin, first_half) * q_scale).astype(BF16)
        ko_ref[:, sl] = _rotate_slab(k_ref[:, sl], cos, sin, first_half).astype(BF16)
        qio_ref[:, sl] = _rotate_slab(qi_ref[:, sl], cos, sin, first_half).astype(BF16)
    vo_ref[...] = v_ref[...].astype(BF16)
    ki = _rotate_slab(ps_ref[...], cos, sin, first_half)
    ki_dup = jnp.where(lane < HEAD_DIM, ki, pltpu.roll(ki, HEAD_DIM, 1))
    kio_ref[...] = ki_dup.astype(BF16)


def _dsa_prep(pm, ps, cos, sin, seq, hw, col_q):
    t = pm.shape[0]
    tm = min(512, seq)
    npos = seq // tm

    def col(c):
        return pl.BlockSpec((tm, hw), lambda i: (i, c))

    pos_spec = pl.BlockSpec((tm, LANES), lambda i: (i % npos, 0))
    row_hw = pl.BlockSpec((tm, hw), lambda i: (i, 0))
    row_l = pl.BlockSpec((tm, LANES), lambda i: (i, 0))
    shp = jax.ShapeDtypeStruct((t, hw), BF16)
    return pl.pallas_call(
        functools.partial(_dsa_prep_kernel, q_scale=HEAD_DIM ** -0.5),
        grid=(t // tm,),
        in_specs=[col(col_q), col(col_q + 1), col(col_q + 2), col(col_q + 3), row_l, pos_spec, pos_spec],
        out_specs=[row_hw, row_hw, row_hw, row_hw, row_l],
        out_shape=[shp, shp, shp, shp, jax.ShapeDtypeStruct((t, LANES), BF16)],
        compiler_params=_cparams("parallel"),
        name="dsa_prep",
    )(pm, pm, pm, pm, ps, cos, sin)


def _ssd_kernel(z_ref, xr_ref, br_ref, cr_ref, ps_ref,
                wx_ref, wb_ref, wc_ref, bx_ref, bb_ref, bc_ref,
                dtb_ref, alog_ref, dsk_ref, gn_ref,
                ex_ref, lt_ref, sel0_ref, sel1_ref, up0_ref, up1_ref,
                o_ref,
                extx_ref, extb_ref, extc_ref, xs_ref, bm_ref, cm_ref, dt_ref, y_ref, st_ref,
                *, rows, n_pairs_per_group):
    r = pl.program_id(1)
    d = z_ref.shape[1]
    gw = SSD_GROUPS * SSD_STATE
    hist = SUBLANES

    @pl.when(r == 0)
    def _():
        extx_ref[0:hist, :] = jnp.zeros((hist, d), F32)
        extb_ref[0:hist, :] = jnp.zeros((hist, gw), F32)
        extc_ref[0:hist, :] = jnp.zeros((hist, gw), F32)
        st_ref[...] = jnp.zeros_like(st_ref)

    @pl.when(r > 0)
    def _():
        extx_ref[0:hist, :] = extx_ref[rows:rows + hist, :]
        extb_ref[0:hist, :] = extb_ref[rows:rows + hist, :]
        extc_ref[0:hist, :] = extc_ref[rows:rows + hist, :]

    def conv_silu(raw_ref, ext_ref, w_ref, b_ref, dst_ref):
        ext_ref[hist:hist + rows, :] = raw_ref[...]
        acc = b_ref[...] + w_ref[SSD_CONV - 1:SSD_CONV, :] * ext_ref[hist:hist + rows, :]
        for back in range(1, SSD_CONV):
            tap = SSD_CONV - 1 - back
            acc = acc + w_ref[tap:tap + 1, :] * ext_ref[hist - back:hist - back + rows, :]
        dst_ref[...] = _silu(acc)

    conv_silu(xr_ref, extx_ref, wx_ref, bx_ref, xs_ref)
    conv_silu(br_ref, extb_ref, wb_ref, bb_ref, bm_ref)
    conv_silu(cr_ref, extc_ref, wc_ref, bc_ref, cm_ref)

    lane = lax.broadcasted_iota(I32, (1, LANES), 1)
    n_heads = d // HEAD_DIM
    head_lane = (lane >= DT_LANE) & (lane < DT_LANE + n_heads)
    dt_all = jax.nn.softplus(ps_ref[...] + dtb_ref[...])
    dt_ref[...] = jnp.where(head_lane, dt_all, 0.0)
    a_vec = jnp.where(head_lane, -jnp.exp(alog_ref[...]), 0.0)

    ex = ex_ref[...]
    lt = lt_ref[...]
    up0 = up0_ref[...]
    up1 = up1_ref[...]
    t_idx = lax.broadcasted_iota(I32, (CHUNK, LANES), 0)
    s_idx = lax.broadcasted_iota(I32, (CHUNK, LANES), 1)
    tril2 = t_idx >= (s_idx % HEAD_DIM)
    low_half = s_idx < HEAD_DIM
    low_half2 = lax.broadcasted_iota(I32, (LANES, LANES), 1) < HEAD_DIM
    top_rows = lax.broadcasted_iota(I32, (LANES, LANES), 0) < HEAD_DIM
    blockdiag = low_half2 == top_rows

    def chunk_body(c, carry):
        rs = pl.ds(pl.multiple_of(c * CHUNK, CHUNK), CHUNK)
        dt = dt_ref[rs, :]
        ac = dt * a_vec
        a_cum = _dot(lt, ac, precision=HIGHEST)
        act0 = _nt(sel0_ref[...], ac, precision=HIGHEST)
        act1 = _nt(sel1_ref[...], ac, precision=HIGHEST)
        a_cum_t = _dot(act0, up0, precision=HIGHEST) + _dot(act1, up1, precision=HIGHEST)
        both = jnp.concatenate([a_cum, dt], axis=0)
        hi = both.astype(BF16)
        rem = both - hi.astype(F32)
        mid = rem.astype(BF16)
        lo = (rem - mid.astype(F32)).astype(BF16)
        both_e = (_dot(hi, ex) + _dot(mid, ex)) + _dot(lo, ex)
        col_all = both_e[0:CHUNK, :]
        dt_e = both_e[CHUNK:2 * CHUNK, :]
        last_e = col_all[CHUNK - 1:CHUNK, :]
        exp_a = jnp.exp(col_all)
        to_end = jnp.exp(last_e - col_all)
        exp_end = jnp.exp(last_e)
        xdt = xs_ref[rs, :] * dt_e
        xdt_b = xdt.astype(BF16)
        xw_b = (xdt * to_end).astype(BF16)
        gwid = d // SSD_GROUPS
        for g in range(SSD_GROUPS):
            gs = slice(g * SSD_STATE, (g + 1) * SSD_STATE)
            gd = slice(g * gwid, (g + 1) * gwid)
            bm_b = bm_ref[rs, gs].astype(BF16)
            cm_b = cm_ref[rs, gs].astype(BF16)
            cb2 = _nt(cm_b, jnp.concatenate([bm_b, bm_b], axis=0))
            st = st_ref[g]
            y_off = _dot(cm_b, st.astype(BF16)) * exp_a[:, gd]
            for jj in range(n_pairs_per_group):
                pidx = g * n_pairs_per_group + jj
                sl = slice(pidx * LANES, (pidx + 1) * LANES)
                seg = col_all[:, sl] - a_cum_t[pidx:pidx + 1, :]
                lmat = (cb2 * jnp.exp(jnp.where(tril2, seg, -jnp.inf))).astype(BF16)
                xp = xdt_b[:, sl]
                rhs = jnp.where(blockdiag, jnp.concatenate([xp, xp], axis=0), jnp.zeros((), BF16))
                y_ref[rs, sl] = _dot(lmat, rhs) + y_off[:, jj * LANES:(jj + 1) * LANES]
            st_ref[g] = st * exp_end[:, gd] + _tn(bm_b, xw_b[:, gd])
        return carry

    lax.fori_loop(0, rows // CHUNK, chunk_body, 0)

    y = (y_ref[...] + xs_ref[...] * dsk_ref[...]) * _silu(z_ref[...])
    gwid = d // SSD_GROUPS
    for g in range(SSD_GROUPS):
        gd = slice(g * gwid, (g + 1) * gwid)
        o_ref[:, gd] = _rms(y[:, gd], gn_ref[:, gd]).astype(o_ref.dtype)


def _ssd_constants(d):
    n_heads = d // HEAD_DIM
    n_pairs = n_heads // 2
    pr = max(SUBLANES, n_pairs)
    ex = np.zeros((LANES, d), np.float32)
    for h in range(n_heads):
        ex[DT_LANE + h, h * HEAD_DIM:(h + 1) * HEAD_DIM] = 1.0
    lt = np.tril(np.ones((CHUNK, CHUNK), np.float32))
    up = np.triu(np.ones((CHUNK, CHUNK), np.float32))
    up0 = np.concatenate([up, np.zeros_like(up)], axis=1)
    up1 = np.concatenate([np.zeros_like(up), up], axis=1)
    sel0 = np.zeros((pr, LANES), np.float32)
    sel1 = np.zeros((pr, LANES), np.float32)
    for j in range(n_pairs):
        sel0[j, DT_LANE + 2 * j] = 1.0
        sel1[j, DT_LANE + 2 * j + 1] = 1.0
    return [jnp.asarray(ex).astype(BF16)] + [jnp.asarray(a) for a in (lt, sel0, sel1, up0, up1)]


def _ssd(pm, ps, conv_w, conv_b, dt_bias, a_log, d_skip, out_norm, batch, seq, d):
    t = pm.shape[0]
    rows = min(256, seq)
    nr = seq // rows
    gw = SSD_GROUPS * SSD_STATE
    n_heads = d // HEAD_DIM
    gwid = d // SSD_GROUPS

    def lane_vec(v):
        return jnp.zeros((1, LANES), F32).at[0, DT_LANE:DT_LANE + n_heads].set(v)

    consts = _ssd_constants(d)
    wx, wb, wc = conv_w[:, :d], conv_w[:, d:d + gw], conv_w[:, d + gw:]
    cb = conv_b.reshape(1, -1)
    bx, bb, bc = cb[:, :d], cb[:, d:d + gw], cb[:, d + gw:]
    dsk = jnp.repeat(d_skip, HEAD_DIM).reshape(1, d)

    def rowblk(width, colblk):
        return pl.BlockSpec((rows, width), lambda b, r: (b * nr + r, colblk))

    def full(a):
        return pl.BlockSpec(a.shape, lambda b, r: (0,) * a.ndim)

    small = [wx, wb, wc, bx, bb, bc, lane_vec(dt_bias), lane_vec(a_log), dsk, out_norm.reshape(1, d)] + consts
    return pl.pallas_call(
        functools.partial(_ssd_kernel, rows=rows, n_pairs_per_group=n_heads // SSD_GROUPS // 2),
        grid=(batch, nr),
        in_specs=[rowblk(d, 0), rowblk(d, 1), rowblk(gw, 2 * d // gw), rowblk(gw, 2 * d // gw + 1),
                  rowblk(LANES, 0)] + [full(a) for a in small],
        out_specs=rowblk(d, 0),
        out_shape=jax.ShapeDtypeStruct((t, d), BF16),
        scratch_shapes=[
            pltpu.VMEM((rows + SUBLANES, d), F32), pltpu.VMEM((rows + SUBLANES, gw), F32),
            pltpu.VMEM((rows + SUBLANES, gw), F32),
            pltpu.VMEM((rows, d), F32), pltpu.VMEM((rows, gw), F32), pltpu.VMEM((rows, gw), F32),
            pltpu.VMEM((rows, LANES), F32), pltpu.VMEM((rows, d), F32),
            pltpu.VMEM((SSD_GROUPS, SSD_STATE, gwid), F32),
        ],
        compiler_params=_cparams("arbitrary", "arbitrary"),
        name="ssd",
    )(pm, pm, pm, pm, ps, *small)


def _order_key(score):
    bits = lax.bitcast_convert_type(score, I32)
    return bits ^ ((bits >> 31) & jnp.int32(0x7FFFFFFF))


_NEG_INF_KEY = int(np.array(-np.inf, np.float32).view(np.int32)) ^ 0x7FFFFFFF


def _dsa_kernel(qi_s, kj_s, q_ref, qi_ref, ps_ref, ki_ref, k_ref, v_ref, o_ref,
                keys_ref, thr_ref, jcut_ref, qst_ref, qis_ref, wb_ref, s_ref, m_ref, l_ref, acc_ref,
                *, topk, w_scale, tk, ts, idx_bits):
    step = pl.program_id(1)
    qblk = qi_s[step]
    kj = kj_s[step]
    start = qblk * DSA_QBLOCK
    hw = q_ref.shape[1]
    n_heads = hw // HEAD_DIM
    n_pairs = n_heads // 2
    lane_q = lax.broadcasted_iota(I32, (DSA_QBLOCK, LANES), 1)
    low = lane_q < HEAD_DIM

    @pl.when(kj == 0)
    def _():
        zero_b = jnp.zeros((), BF16)
        for p in range(n_pairs):
            sl = slice(p * LANES, (p + 1) * LANES)
            qs = q_ref[:, sl]
            qst_ref[p, 0:DSA_QBLOCK, :] = jnp.where(low, qs, zero_b)
            qst_ref[p, DSA_QBLOCK:2 * DSA_QBLOCK, :] = jnp.where(low, zero_b, qs)
            qis = qi_ref[:, sl]
            qis_ref[(2 * p) * DSA_QBLOCK:(2 * p + 1) * DSA_QBLOCK, :] = jnp.where(low, qis, zero_b)
            qis_ref[(2 * p + 1) * DSA_QBLOCK:(2 * p + 2) * DSA_QBLOCK, :] = jnp.where(low, zero_b, qis)
        w = ps_ref[...] * w_scale
        for h in range(n_heads):
            wb_ref[h] = jnp.broadcast_to(w[:, WIDX_LANE + h:WIDX_LANE + h + 1], (DSA_QBLOCK, LANES))
        m_ref[...] = jnp.full(m_ref.shape, NEG_BIG, F32)
        l_ref[...] = jnp.zeros_like(l_ref)
        acc_ref[...] = jnp.zeros_like(acc_ref)

        n_tiles = (start + DSA_QBLOCK + ts - 1) // ts
        row = lax.broadcasted_iota(I32, (DSA_QBLOCK, ts), 0)
        kcol = lax.broadcasted_iota(I32, (DSA_QBLOCK, ts), 1)
        vis_end = start + CHUNK + CHUNK * (row // CHUNK)

        def score_body(t, carry):
            mx, mn = carry
            base = pl.multiple_of(t * ts, ts)
            kt = ki_ref[pl.ds(base, ts), :]
            rel_all = _nt(qis_ref[...], kt)
            slabs = []
            for c in range(ts // LANES):
                acc = jnp.zeros((DSA_QBLOCK, LANES), F32)
                for h in range(n_heads):
                    rel = rel_all[h * DSA_QBLOCK:(h + 1) * DSA_QBLOCK, c * LANES:(c + 1) * LANES]
                    acc = acc + jnp.maximum(rel, 0.0) * wb_ref[h]
                slabs.append(acc)
            sc = jnp.concatenate(slabs, axis=1)
            sc = jnp.where(sc == 0.0, 0.0, sc)
            adm = kcol + base < vis_end
            lowest = jnp.where(adm, sc, jnp.inf)
            sc = jnp.where(adm, sc, -jnp.inf)
            keys_ref[:, pl.ds(base, ts)] = _order_key(sc)
            for c in range(ts // LANES):
                mx = jnp.maximum(mx, sc[:, c * LANES:(c + 1) * LANES])
                mn = jnp.minimum(mn, lowest[:, c * LANES:(c + 1) * LANES])
            return mx, mn

        mx, mn = lax.fori_loop(0, n_tiles, score_body,
                               (jnp.full((DSA_QBLOCK, LANES), -jnp.inf, F32),
                                jnp.full((DSA_QBLOCK, LANES), jnp.inf, F32)))
        row_max = jnp.broadcast_to(jnp.max(mx, axis=1, keepdims=True), (DSA_QBLOCK, LANES))
        row_min = jnp.broadcast_to(jnp.min(mn, axis=1, keepdims=True), (DSA_QBLOCK, LANES))

        def pad_body(t, carry):
            base = pl.multiple_of(t * ts, ts)
            keys_ref[:, pl.ds(base, ts)] = jnp.full((DSA_QBLOCK, ts), _NEG_INF_KEY, I32)
            return carry

        n_cover = ((start + DSA_QBLOCK + tk - 1) // tk) * (tk // ts)
        lax.fori_loop(n_tiles, n_cover, pad_body, 0)

        zeros_f = jnp.zeros((DSA_QBLOCK, LANES), F32)
        kf = float(topk)

        def count_ge(cand_key):
            def count_body(t, cnt):
                base = pl.multiple_of(t * ts, ts)
                kt = keys_ref[:, pl.ds(base, ts)]
                for c in range(ts // LANES):
                    cnt = cnt + jnp.where(kt[:, c * LANES:(c + 1) * LANES] >= cand_key, 1.0, 0.0)
                return cnt

            cnt = lax.fori_loop(0, n_tiles, count_body, zeros_f)
            return jnp.broadcast_to(jnp.sum(cnt, axis=1, keepdims=True), (DSA_QBLOCK, LANES))

        n_adm = vis_end[:, 0:LANES].astype(F32)
        few = n_adm < kf
        c_max = count_ge(_order_key(row_max))
        at_max = c_max >= kf
        lo0 = jnp.where(few, -jnp.inf, jnp.where(at_max, row_max, row_min))
        done0 = jnp.where(few | at_max | (n_adm == kf), 1.0, 0.0)
        long_row = n_adm > LOG_COUNT_MIN_KEYS

        def residual(c):
            return jnp.where(long_row, jnp.log((c + 0.5) * (1.0 / kf)), c - kf + 0.5)

        def search_step(st):
            lo, hi, c_lo, f_lo, f_hi, side, done = st
            cand = lo + (hi - lo) * (f_lo / (f_lo - f_hi))
            inside = (cand > lo) & (cand < hi)
            cand = jnp.where(inside, cand, 0.5 * lo + 0.5 * hi)
            stuck = jnp.logical_not((cand > lo) & (cand < hi))
            c = count_ge(_order_key(cand))
            live = jnp.logical_not(stuck) & (done < 0.5)
            up = live & (c >= kf)
            dn = live & (c < kf)
            f_hi = jnp.where(up & (side > 0.5), 0.5 * f_hi, f_hi)
            f_lo = jnp.where(dn & (side < -0.5), 0.5 * f_lo, f_lo)
            side = jnp.where(up, 1.0, jnp.where(dn, -1.0, side))
            lo = jnp.where(up, cand, lo)
            c_lo = jnp.where(up, c, c_lo)
            f_c = residual(c)
            f_lo = jnp.where(up, f_c, f_lo)
            hi = jnp.where(dn, cand, hi)
            f_hi = jnp.where(dn, f_c, f_hi)
            done = jnp.where(stuck | (c_lo == kf), 1.0, done)
            return (lo, hi, c_lo, f_lo, f_hi, side, done)

        def search_body(st):
            inner = search_step(search_step(st[2:]))
            return (st[0] + 2, (jnp.min(inner[-1]) < 0.5).astype(I32)) + inner

        state = (jnp.int32(0), (jnp.min(done0) < 0.5).astype(I32), lo0, row_max, n_adm,
                 residual(n_adm), residual(c_max), zeros_f, done0)
        state = lax.while_loop(lambda st: (st[0] < SEARCH_MAX_ITERS) & (st[1] > 0), search_body, state)
        thr_ref[...] = _order_key(state[2])

        @pl.when(state[1] > 0)
        def _():
            sign_bit = jnp.int32(-2 ** 31)

            def bisect_body(it, res):
                cand = res | jnp.left_shift(jnp.int32(1), 31 - it)
                return jnp.where(count_ge(cand ^ sign_bit) >= kf, cand, res)

            res = lax.fori_loop(0, 32, bisect_body, jnp.zeros((DSA_QBLOCK, LANES), I32))
            thr_ref[...] = res ^ sign_bit

        thr = thr_ref[...]
        jcut_ref[...] = jnp.full((DSA_QBLOCK, LANES), 2 ** idx_bits - 1, I32)
        settled = jnp.where((state[4] == kf) | few, 1.0, 0.0)

        @pl.when(jnp.min(settled) < 0.5)
        def _():
            def tally_body(t, carry):
                gt, ge = carry
                base = pl.multiple_of(t * ts, ts)
                kt = keys_ref[:, pl.ds(base, ts)]
                for c in range(ts // LANES):
                    ks = kt[:, c * LANES:(c + 1) * LANES]
                    gt = gt + jnp.where(ks > thr, 1.0, 0.0)
                    ge = ge + jnp.where(ks >= thr, 1.0, 0.0)
                return gt, ge

            gt, ge = lax.fori_loop(0, n_tiles, tally_body, (zeros_f, zeros_f))
            need = float(topk) - jnp.sum(gt, axis=1, keepdims=True)
            n_ge = jnp.sum(ge, axis=1, keepdims=True)

            @pl.when(jnp.max(n_ge) > float(topk))
            def _():
                def cut_body(it, cut):
                    cand = cut | jnp.left_shift(jnp.int32(1), idx_bits - 1 - it)

                    def count_body(t, cnt):
                        base = pl.multiple_of(t * ts, ts)
                        kt = keys_ref[:, pl.ds(base, ts)]
                        for c in range(ts // LANES):
                            idx = lane_q + (base + c * LANES)
                            hit = (kt[:, c * LANES:(c + 1) * LANES] == thr) & (idx < cand)
                            cnt = cnt + jnp.where(hit, 1.0, 0.0)
                        return cnt

                    cnt = lax.fori_loop(0, n_tiles, count_body, zeros_f)
                    return jnp.where(jnp.sum(cnt, axis=1, keepdims=True) <= need, cand, cut)

                jcut_ref[...] = lax.fori_loop(0, idx_bits, cut_body, jnp.zeros((DSA_QBLOCK, LANES), I32))

    kbase = pl.multiple_of(kj * tk, tk)
    key_t = keys_ref[:, pl.ds(kbase, tk)]
    thr = thr_ref[...]
    reps = tk // LANES
    thr_t = jnp.concatenate([thr] * reps, axis=1)
    cut_t = jnp.concatenate([jcut_ref[...]] * reps, axis=1)
    kidx = lax.broadcasted_iota(I32, (DSA_QBLOCK, tk), 1) + kbase
    mask = ((key_t > thr_t) | ((key_t == thr_t) & (kidx < cut_t))) & (key_t > jnp.int32(_NEG_INF_KEY))

    pen = jnp.where(mask, 0.0, NEG_BIG).astype(BF16)
    def qk(p):
        s_ref[p] = _nt(qst_ref[p], k_ref[:, p * LANES:(p + 1) * LANES]).astype(BF16)

    qk(0)

    def lane_slabs(a):
        return [a[:, c * LANES:(c + 1) * LANES] for c in range(reps)]

    for p in range(n_pairs):
        if p + 1 < n_pairs:
            qk(p + 1)
        pes, alphas = [], []
        for half in range(2):
            rows = slice(half * DSA_QBLOCK, (half + 1) * DSA_QBLOCK)
            s = s_ref[p, rows, :] + pen
            row_max = jnp.max(_tree(lane_slabs(s), jnp.maximum).astype(F32), axis=1, keepdims=True)
            m_old = m_ref[p, rows, :]
            m_new = jnp.maximum(m_old, row_max)
            alpha = jnp.exp(m_old - m_new)
            pe = jnp.exp(s - jnp.concatenate([m_new.astype(BF16)] * reps, axis=1))
            row_sum = jnp.sum(_tree(lane_slabs(pe), jnp.add).astype(F32), axis=1, keepdims=True)
            m_ref[p, rows, :] = m_new
            l_ref[p, rows, :] = alpha * l_ref[p, rows, :] + row_sum
            pes.append(pe)
            alphas.append(alpha)
        pv = _dot(jnp.concatenate(pes, axis=0), v_ref[:, p * LANES:(p + 1) * LANES])
        acc_ref[p] = jnp.concatenate(alphas, axis=0) * acc_ref[p] + pv

    last = (start + DSA_QBLOCK - 1) // tk

    @pl.when(kj == last)
    def _():
        for p in range(n_pairs):
            oa = acc_ref[p, 0:DSA_QBLOCK, :] / l_ref[p, 0:DSA_QBLOCK, :]
            ob = acc_ref[p, DSA_QBLOCK:2 * DSA_QBLOCK, :] / l_ref[p, DSA_QBLOCK:2 * DSA_QBLOCK, :]
            o_ref[:, p * LANES:(p + 1) * LANES] = jnp.where(low, oa, ob).astype(o_ref.dtype)


def _dsa(q, k, v, qi, ki, ps, batch, seq, n_idx_heads):
    t, hw = q.shape
    nq = seq // DSA_QBLOCK
    tk = min(1024, seq)
    ts = min(1024, seq)
    nkt = seq // tk
    topk = min(DSA_TOPK_MAX, seq // 4)
    n_heads = hw // HEAD_DIM
    qi_list, kj_list = [], []
    for i in range(nq):
        for j in range((i * DSA_QBLOCK + DSA_QBLOCK - 1) // tk + 1):
            qi_list.append(i)
            kj_list.append(j)
    qi_arr = jnp.asarray(np.array(qi_list, np.int32))
    kj_arr = jnp.asarray(np.array(kj_list, np.int32))
    nsteps = len(qi_list)

    qspec = pl.BlockSpec((DSA_QBLOCK, hw), lambda b, s, qs, ks: (b * nq + qs[s], 0))
    kspec = pl.BlockSpec((tk, hw), lambda b, s, qs, ks: (b * nkt + ks[s], 0))
    grid_spec = pltpu.PrefetchScalarGridSpec(
        num_scalar_prefetch=2,
        grid=(batch, nsteps),
        in_specs=[
            qspec, qspec,
            pl.BlockSpec((DSA_QBLOCK, LANES), lambda b, s, qs, ks: (b * nq + qs[s], 0)),
            pl.BlockSpec((seq, LANES), lambda b, s, qs, ks: (b, 0)),
            kspec, kspec,
        ],
        out_specs=qspec,
        scratch_shapes=[
            pltpu.VMEM((DSA_QBLOCK, seq), I32),
            pltpu.VMEM((DSA_QBLOCK, LANES), I32),
            pltpu.VMEM((DSA_QBLOCK, LANES), I32),
            pltpu.VMEM((n_heads // 2, 2 * DSA_QBLOCK, LANES), BF16),
            pltpu.VMEM((n_idx_heads * DSA_QBLOCK, LANES), BF16),
            pltpu.VMEM((n_idx_heads, DSA_QBLOCK, LANES), F32),
            pltpu.VMEM((n_heads // 2, 2 * DSA_QBLOCK, tk), BF16),
            pltpu.VMEM((n_heads // 2, 2 * DSA_QBLOCK, LANES), F32),
            pltpu.VMEM((n_heads // 2, 2 * DSA_QBLOCK, LANES), F32),
            pltpu.VMEM((n_heads // 2, 2 * DSA_QBLOCK, LANES), F32),
        ],
    )
    w_scale = float(n_idx_heads) ** -0.5 * float(HEAD_DIM) ** -0.5
    return pl.pallas_call(
        functools.partial(_dsa_kernel, topk=topk, w_scale=w_scale, tk=tk, ts=ts, idx_bits=seq.bit_length()),
        grid_spec=grid_spec,
        out_shape=jax.ShapeDtypeStruct((t, hw), BF16),
        compiler_params=_cparams("arbitrary", "arbitrary"),
        name="dsa",
    )(qi_arr, kj_arr, q, qi, ps, ki, k, v)


BAND_QROWS = 2 * CHUNK
BAND_KBLOCKS = (BAND_PREV * CHUNK) // BAND_QROWS + 1


def _band_kernel(*refs, q_scale):
    q_ref = refs[0]
    k_refs = refs[1:1 + BAND_KBLOCKS]
    v_refs = refs[1 + BAND_KBLOCKS:1 + 2 * BAND_KBLOCKS]
    bias_ref = refs[1 + 2 * BAND_KBLOCKS]
    o_ref = refs[2 + 2 * BAND_KBLOCKS]
    kbuf_ref, vbuf_ref = refs[3 + 2 * BAND_KBLOCKS:]
    i = pl.program_id(1)
    d = q_ref.shape[1]
    nkeys = BAND_KBLOCKS * BAND_QROWS
    for blk in range(BAND_KBLOCKS):
        kbuf_ref[blk * BAND_QROWS:(blk + 1) * BAND_QROWS, :] = k_refs[blk][...]
        vbuf_ref[blk * BAND_QROWS:(blk + 1) * BAND_QROWS, :] = v_refs[blk][...]
    kcol = lax.broadcasted_iota(I32, (2 * BAND_QROWS, nkeys), 1)
    valid = kcol >= (BAND_KBLOCKS - 1 - i) * BAND_QROWS
    low = lax.broadcasted_iota(I32, (BAND_QROWS, LANES), 1) < HEAD_DIM
    zero_b = jnp.zeros((), BF16)
    def logits(p):
        sl = slice(p * LANES, (p + 1) * LANES)
        qs = q_ref[:, sl] * q_scale
        qst = jnp.concatenate([jnp.where(low, qs, zero_b), jnp.where(low, zero_b, qs)], axis=0)
        s = _nt(qst, kbuf_ref[:, sl]) + bias_ref[p]
        return jnp.where(valid, s, -jnp.inf).astype(BF16)

    n_slabs = d // LANES
    s_next = logits(0)
    for p in range(n_slabs):
        sl = slice(p * LANES, (p + 1) * LANES)
        s = s_next
        if p + 1 < n_slabs:
            s_next = logits(p + 1)
        slabs = [s[:, c * LANES:(c + 1) * LANES] for c in range(nkeys // LANES)]
        row_max = jnp.max(_tree(slabs, jnp.maximum).astype(F32), axis=1, keepdims=True).astype(BF16)
        es = [jnp.exp(sl_c - row_max) for sl_c in slabs]
        denom = jnp.sum(_tree(es, jnp.add).astype(F32), axis=1, keepdims=True)
        pv = _dot(jnp.concatenate(es, axis=1), vbuf_ref[:, sl]) / denom
        o_ref[:, sl] = jnp.where(low, pv[0:BAND_QROWS], pv[BAND_QROWS:2 * BAND_QROWS]).astype(o_ref.dtype)


def _band_bias(rel_bias):
    nkeys = BAND_KBLOCKS * BAND_QROWS
    tq = np.arange(BAND_QROWS)[:, None]
    kj = np.arange(nkeys)[None, :]
    qc = tq // CHUNK
    kc = kj // CHUNK
    visible = (kc >= qc) & (kc <= qc + BAND_PREV)
    shift = BAND_QROWS - 1
    m = np.arange(nkeys + shift)
    dist = m - shift - (BAND_KBLOCKS - 1) * BAND_QROWS
    ext = rel_bias.astype(F32)[:, np.clip(dist, -REL_PAST, CHUNK - 1) + REL_PAST]
    bias = jnp.stack([ext[:, shift - t:shift - t + nkeys] for t in range(BAND_QROWS)], axis=1)
    return jnp.where(jnp.asarray(visible)[None], bias, -jnp.inf)


def _band(qkv, rel_bias, batch, seq, d):
    t = qkv.shape[0]
    nq = seq // BAND_QROWS
    n_heads = d // HEAD_DIM
    nkeys = BAND_KBLOCKS * BAND_QROWS
    bias = _band_bias(rel_bias).reshape(n_heads // 2, 2 * BAND_QROWS, nkeys)

    def kv_spec(blk, col):
        back = BAND_KBLOCKS - 1 - blk
        return pl.BlockSpec((BAND_QROWS, d), lambda b, i: (b * nq + jnp.maximum(i - back, 0), col))

    in_specs = [pl.BlockSpec((BAND_QROWS, d), lambda b, i: (b * nq + i, 0))]
    in_specs += [kv_spec(blk, 1) for blk in range(BAND_KBLOCKS)]
    in_specs += [kv_spec(blk, 2) for blk in range(BAND_KBLOCKS)]
    in_specs += [pl.BlockSpec((n_heads // 2, 2 * BAND_QROWS, nkeys), lambda b, i: (0, 0, 0))]
    return pl.pallas_call(
        functools.partial(_band_kernel, q_scale=HEAD_DIM ** -0.5),
        grid=(batch, nq),
        in_specs=in_specs,
        out_specs=pl.BlockSpec((BAND_QROWS, d), lambda b, i: (b * nq + i, 0)),
        out_shape=jax.ShapeDtypeStruct((t, d), BF16),
        scratch_shapes=[pltpu.VMEM((nkeys, d), BF16), pltpu.VMEM((nkeys, d), BF16)],
        compiler_params=_cparams("parallel", "arbitrary"),
        name="band",
    )(*([qkv] * (1 + 2 * BAND_KBLOCKS)), bias)


def _rope_tables(seq):
    half = HEAD_DIM // 2
    inv = ROPE_THETA ** (-jnp.arange(half, dtype=F32) / half)
    ang = jnp.arange(seq, dtype=F32)[:, None] * inv[None, :]
    cos = jnp.cos(ang)
    sin = jnp.sin(ang)
    reps = LANES // HEAD_DIM
    cos_t = jnp.tile(jnp.concatenate([cos, cos], axis=1), (1, reps))
    sin_t = jnp.tile(jnp.concatenate([-sin, sin], axis=1), (1, reps))
    return cos_t, sin_t


def _ssd_dsa_mixer(x, norm, w_in, conv_w, conv_b, dt_bias, a_log, d_skip, out_norm, w_out, batch, seq):
    t, d = x.shape
    hw = d // 2
    n_ssd_heads = d // HEAD_DIM
    n_idx_heads = d // 128
    xbc = d + 2 * SSD_GROUPS * SSD_STATE
    sizes = (d, xbc, n_ssd_heads, hw, hw, hw, n_idx_heads * HEAD_DIM, HEAD_DIM, n_idx_heads)
    offs = np.concatenate([[0], np.cumsum(sizes)])
    seg = [w_in[:, offs[i]:offs[i + 1]] for i in range(len(sizes))]
    w_z, w_xbc, w_dt, w_q, w_k, w_v, w_qi, w_ki, w_wi = seg
    w_main = jnp.concatenate([w_z, w_xbc, w_q, w_k, w_v, w_qi], axis=1).astype(BF16)
    w_small = jnp.zeros((d, LANES), F32)
    w_small = w_small.at[:, KIDX_LANE:KIDX_LANE + HEAD_DIM].set(w_ki)
    w_small = w_small.at[:, DT_LANE:DT_LANE + n_ssd_heads].set(w_dt)
    w_small = w_small.at[:, WIDX_LANE:WIDX_LANE + n_idx_heads].set(w_wi)
    pm = _norm_matmul(x, norm, w_main, hw, F32)
    ps = _norm_matmul(x, norm, w_small.astype(BF16), LANES, F32)
    y = _ssd(pm, ps, conv_w, conv_b, dt_bias, a_log, d_skip, out_norm, batch, seq, d)
    cos_t, sin_t = _rope_tables(seq)
    col_q = (d + xbc) // hw
    q, k, v, qi, ki = _dsa_prep(pm, ps, cos_t, sin_t, seq, hw, col_q)
    o = _dsa(q, k, v, qi, ki, ps, batch, seq, n_idx_heads)
    w_out_b = w_out.astype(BF16)
    return _proj_res(x, [y, o], [w_out_b[:d], w_out_b[d:]])


def _band_mixer(x, norm, w_qkv, rel_bias, w_out, batch, seq):
    t, d = x.shape
    qkv = _norm_matmul(x, norm, w_qkv.astype(BF16), d // 2, BF16)
    o = _band(qkv, rel_bias, batch, seq, d)
    return _proj_res(x, [o], [w_out.astype(BF16)])


def kernel(x, ffn1_norm, ffn1_w_in, ffn1_w_out, mix_norm, ab_w_in, ssd_conv_w, ssd_conv_b, ssd_dt_bias, ssd_a_log, ssd_d_skip, ssd_out_norm, ab_w_out, c_w_qkv, c_rel_bias, c_w_out, ffn2_norm, ffn2_w_in, ffn2_w_out, final_norm):
    batch, seq, d = x.shape
    depth = ffn1_norm.shape[0]
    xf = x.reshape(batch * seq, d)
    w1_in, w1_out = ffn1_w_in.astype(BF16), ffn1_w_out.astype(BF16)
    w2_in, w2_out = ffn2_w_in.astype(BF16), ffn2_w_out.astype(BF16)
    for layer in range(depth):
        xf = _ffn(xf, ffn1_norm[layer], w1_in, w1_out, layer)
        i = layer // 2
        if layer % 2 == 0:
            xf = _ssd_dsa_mixer(xf, mix_norm[layer], ab_w_in[i], ssd_conv_w[i], ssd_conv_b[i], ssd_dt_bias[i],
                                ssd_a_log[i], ssd_d_skip[i], ssd_out_norm[i], ab_w_out[i], batch, seq)
        else:
            xf = _band_mixer(xf, mix_norm[layer], c_w_qkv[i], c_rel_bias[i], c_w_out[i], batch, seq)
        fin = final_norm if layer == depth - 1 else None
        xf = _ffn(xf, ffn2_norm[layer], w2_in, w2_out, layer, fin)
    return xf.reshape(batch, seq, d)
```

```python
import functools

import jax
import jax.numpy as jnp
import numpy as np
from jax import lax
from jax.experimental import pallas as pl
from jax.experimental.pallas import tpu as pltpu

F32 = jnp.float32
BF16 = jnp.bfloat16
I32 = jnp.int32

EPS = 1e-5
ROPE_THETA = 10000.0
CHUNK = 64
HEAD_DIM = 64
SSD_GROUPS = 4
SSD_STATE = 128
SSD_CONV = 4
DSA_TOPK_MAX = 256
DSA_QBLOCK = 128
SEARCH_MAX_ITERS = 40
LOG_COUNT_MIN_KEYS = 1536.0
KV_RING = 3
BAND_PREV = 8
REL_PAST = 256

LANES = 128
SUBLANES = 8
VMEM_LIMIT_BYTES = 56 * 1024 * 1024

KIDX_LANE = 0
DT_LANE = 64
WIDX_LANE = 96

NEG_BIG = -1e30
HIGHEST = lax.Precision.HIGHEST


def _cparams(*sem):
    return pltpu.CompilerParams(dimension_semantics=sem, vmem_limit_bytes=VMEM_LIMIT_BYTES)


def _nt(a, b, **kw):
    return lax.dot_general(a, b, (((1,), (1,)), ((), ())), preferred_element_type=F32, **kw)


def _tn(a, b, **kw):
    return lax.dot_general(a, b, (((0,), (0,)), ((), ())), preferred_element_type=F32, **kw)


def _dot(a, b, **kw):
    return jnp.dot(a, b, preferred_element_type=F32, **kw)


def _rms(x, gain):
    ms = jnp.mean(x * x, axis=-1, keepdims=True)
    return x * lax.rsqrt(ms + EPS) * gain


def _silu(x):
    return x * jax.nn.sigmoid(x)


def _tree(parts, op):
    while len(parts) > 1:
        parts = [op(parts[i], parts[i + 1]) for i in range(0, len(parts) - 1, 2)] + parts[len(parts) & ~1:]
    return parts[0]


def _norm_matmul_kernel(x_ref, g_ref, w_ref, o_ref, xn_ref):
    @pl.when(pl.program_id(1) == 0)
    def _():
        xn_ref[...] = _rms(x_ref[...], g_ref[...]).astype(BF16)

    o_ref[...] = _dot(xn_ref[...], w_ref[...]).astype(o_ref.dtype)


def _norm_matmul(x, gain, w, tn, out_dtype):
    t, d = x.shape
    n = w.shape[1]
    tm = min(1024, t)
    return pl.pallas_call(
        _norm_matmul_kernel,
        grid=(t // tm, n // tn),
        in_specs=[
            pl.BlockSpec((tm, d), lambda i, j: (i, 0)),
            pl.BlockSpec((1, d), lambda i, j: (0, 0)),
            pl.BlockSpec((d, tn), lambda i, j: (0, j)),
        ],
        out_specs=pl.BlockSpec((tm, tn), lambda i, j: (i, j)),
        out_shape=jax.ShapeDtypeStruct((t, n), out_dtype),
        scratch_shapes=[pltpu.VMEM((tm, d), BF16)],
        compiler_params=_cparams("parallel", "arbitrary"),
        name="norm_matmul",
    )(x, gain.reshape(1, d), w)


def _ffn_kernel(x_ref, g_ref, wg_ref, wu_ref, wo_ref, fg_ref, o_ref, xn_ref, *, final):
    j = pl.program_id(1)

    @pl.when(j == 0)
    def _():
        xn_ref[...] = _rms(x_ref[...], g_ref[...]).astype(BF16)
        o_ref[...] = jnp.zeros_like(o_ref)

    xn = xn_ref[...]
    g = _dot(xn, wg_ref[...])
    u = _dot(xn, wu_ref[...])
    a = (_silu(g) * u).astype(BF16)
    o_ref[...] += _dot(a, wo_ref[...])

    @pl.when(j == pl.num_programs(1) - 1)
    def _():
        y = x_ref[...] + 0.5 * o_ref[...]
        if final:
            y = _rms(y, fg_ref[...])
        o_ref[...] = y


def _ffn(x, gain, w_in, w_out, layer, final_gain=None):
    t, d = x.shape
    ff = w_out.shape[1]
    tm = min(512, t)
    tf = 512
    nf = ff // tf
    final = final_gain is not None
    fg = (final_gain if final else gain).reshape(1, d)
    return pl.pallas_call(
        functools.partial(_ffn_kernel, final=final),
        grid=(t // tm, nf),
        in_specs=[
            pl.BlockSpec((tm, d), lambda i, j: (i, 0)),
            pl.BlockSpec((1, d), lambda i, j: (0, 0)),
            pl.BlockSpec((None, d, tf), lambda i, j: (layer, 0, j)),
            pl.BlockSpec((None, d, tf), lambda i, j: (layer, 0, j + nf)),
            pl.BlockSpec((None, tf, d), lambda i, j: (layer, j, 0)),
            pl.BlockSpec((1, d), lambda i, j: (0, 0)),
        ],
        out_specs=pl.BlockSpec((tm, d), lambda i, j: (i, 0)),
        out_shape=jax.ShapeDtypeStruct((t, d), F32),
        scratch_shapes=[pltpu.VMEM((tm, d), BF16)],
        compiler_params=_cparams("parallel", "arbitrary"),
        name="ffn",
    )(x, gain.reshape(1, d), w_in, w_in, w_out, fg)


def _proj_res_kernel(*refs, n_lhs):
    x_ref = refs[0]
    a_refs = refs[1:1 + n_lhs]
    w_refs = refs[1 + n_lhs:1 + 2 * n_lhs]
    o_ref = refs[1 + 2 * n_lhs]
    y = x_ref[...]
    for a_ref, w_ref in zip(a_refs, w_refs):
        y = y + _dot(a_ref[...], w_ref[...])
    o_ref[...] = y


def _proj_res(x, lhs, ws):
    t, d = x.shape
    tm = min(1024, t)
    tn = d // 2
    in_specs = [pl.BlockSpec((tm, tn), lambda i, j: (i, j))]
    in_specs += [pl.BlockSpec((tm, a.shape[1]), lambda i, j: (i, 0)) for a in lhs]
    in_specs += [pl.BlockSpec((w.shape[0], tn), lambda i, j: (0, j)) for w in ws]
    return pl.pallas_call(
        functools.partial(_proj_res_kernel, n_lhs=len(lhs)),
        grid=(t // tm, d // tn),
        in_specs=in_specs,
        out_specs=pl.BlockSpec((tm, tn), lambda i, j: (i, j)),
        out_shape=jax.ShapeDtypeStruct((t, d), F32),
        compiler_params=_cparams("parallel", "arbitrary"),
        name="proj_res",
    )(x, *lhs, *ws)


def _rotate_slab(x, cos, sin_signed, first_half):
    fwd = pltpu.roll(x, HEAD_DIM // 2, 1)
    bwd = pltpu.roll(x, LANES - HEAD_DIM // 2, 1)
    return x * cos + jnp.where(first_half, bwd, fwd) * sin_signed


def _dsa_prep_kernel(q_ref, k_ref, v_ref, qi_ref, ps_ref, cos_ref, sin_ref,
                     qo_ref, ko_ref, vo_ref, qio_ref, kio_ref, *, q_scale):
    cos = cos_ref[...]
    sin = sin_ref[...]
    lane = lax.broadcasted_iota(I32, cos.shape, 1)
    first_half = (lane % HEAD_DIM) < HEAD_DIM // 2
    n_slabs = q_ref.shape[1] // LANES
    for c in range(n_slabs):
        sl = slice(c * LANES, (c + 1) * LANES)
        qo_ref[:, sl] = (_rotate_slab(q_ref[:, sl], cos, sin, first_half) * q_scale).astype(BF16)
        ko_ref[:, sl] = _rotate_slab(k_ref[:, sl], cos, sin, first_half).astype(BF16)
        qio_ref[:, sl] = _rotate_slab(qi_ref[:, sl], cos, sin, first_half).astype(BF16)
    vo_ref[...] = v_ref[...].astype(BF16)
    ki = _rotate_slab(ps_ref[...], cos, sin, first_half)
    ki_dup = jnp.where(lane < HEAD_DIM, ki, pltpu.roll(ki, HEAD_DIM, 1))
    kio_ref[...] = ki_dup.astype(BF16)


def _dsa_prep(pm, ps, cos, sin, seq, hw, col_q):
    t = pm.shape[0]
    tm = min(512, seq)
    npos = seq // tm

    def col(c):
        return pl.BlockSpec((tm, hw), lambda i: (i, c))

    pos_spec = pl.BlockSpec((tm, LANES), lambda i: (i % npos, 0))
    row_hw = pl.BlockSpec((tm, hw), lambda i: (i, 0))
    row_l = pl.BlockSpec((tm, LANES), lambda i: (i, 0))
    shp = jax.ShapeDtypeStruct((t, hw), BF16)
    return pl.pallas_call(
        functools.partial(_dsa_prep_kernel, q_scale=HEAD_DIM ** -0.5),
        grid=(t // tm,),
        in_specs=[col(col_q), col(col_q + 1), col(col_q + 2), col(col_q + 3), row_l, pos_spec, pos_spec],
        out_specs=[row_hw, row_hw, row_hw, row_hw, row_l],
        out_shape=[shp, shp, shp, shp, jax.ShapeDtypeStruct((t, LANES), BF16)],
        compiler_params=_cparams("parallel"),
        name="dsa_prep",
    )(pm, pm, pm, pm, ps, cos, sin)


def _ssd_kernel(z_ref, xr_ref, br_ref, cr_ref, ps_ref,
                wx_ref, wb_ref, wc_ref, bx_ref, bb_ref, bc_ref,
                dtb_ref, alog_ref, dsk_ref, gn_ref,
                ex_ref, lt_ref, sel0_ref, sel1_ref, up0_ref, up1_ref,
                o_ref,
                extx_ref, extb_ref, extc_ref, xs_ref, bm_ref, cm_ref, dt_ref, y_ref, st_ref,
                *, rows, n_pairs_per_group):
    r = pl.program_id(1)
    d = z_ref.shape[1]
    gw = SSD_GROUPS * SSD_STATE
    hist = SUBLANES

    @pl.when(r == 0)
    def _():
        extx_ref[0:hist, :] = jnp.zeros((hist, d), F32)
        extb_ref[0:hist, :] = jnp.zeros((hist, gw), F32)
        extc_ref[0:hist, :] = jnp.zeros((hist, gw), F32)
        st_ref[...] = jnp.zeros_like(st_ref)

    @pl.when(r > 0)
    def _():
        extx_ref[0:hist, :] = extx_ref[rows:rows + hist, :]
        extb_ref[0:hist, :] = extb_ref[rows:rows + hist, :]
        extc_ref[0:hist, :] = extc_ref[rows:rows + hist, :]

    def conv_silu(raw_ref, ext_ref, w_ref, b_ref, dst_ref):
        ext_ref[hist:hist + rows, :] = raw_ref[...]
        acc = b_ref[...] + w_ref[SSD_CONV - 1:SSD_CONV, :] * ext_ref[hist:hist + rows, :]
        for back in range(1, SSD_CONV):
            tap = SSD_CONV - 1 - back
            acc = acc + w_ref[tap:tap + 1, :] * ext_ref[hist - back:hist - back + rows, :]
        dst_ref[...] = _silu(acc)

    conv_silu(xr_ref, extx_ref, wx_ref, bx_ref, xs_ref)
    conv_silu(br_ref, extb_ref, wb_ref, bb_ref, bm_ref)
    conv_silu(cr_ref, extc_ref, wc_ref, bc_ref, cm_ref)

    lane = lax.broadcasted_iota(I32, (1, LANES), 1)
    n_heads = d // HEAD_DIM
    head_lane = (lane >= DT_LANE) & (lane < DT_LANE + n_heads)
    dt_all = jax.nn.softplus(ps_ref[...] + dtb_ref[...])
    dt_ref[...] = jnp.where(head_lane, dt_all, 0.0)
    a_vec = jnp.where(head_lane, -jnp.exp(alog_ref[...]), 0.0)

    ex = ex_ref[...]
    lt = lt_ref[...]
    up0 = up0_ref[...]
    up1 = up1_ref[...]
    t_idx = lax.broadcasted_iota(I32, (CHUNK, LANES), 0)
    s_idx = lax.broadcasted_iota(I32, (CHUNK, LANES), 1)
    tril2 = t_idx >= (s_idx % HEAD_DIM)
    low_half = s_idx < HEAD_DIM
    low_half2 = lax.broadcasted_iota(I32, (LANES, LANES), 1) < HEAD_DIM
    top_rows = lax.broadcasted_iota(I32, (LANES, LANES), 0) < HEAD_DIM
    blockdiag = low_half2 == top_rows

    def chunk_body(c, carry):
        rs = pl.ds(pl.multiple_of(c * CHUNK, CHUNK), CHUNK)
        dt = dt_ref[rs, :]
        ac = dt * a_vec
        a_cum = _dot(lt, ac, precision=HIGHEST)
        act0 = _nt(sel0_ref[...], ac, precision=HIGHEST)
        act1 = _nt(sel1_ref[...], ac, precision=HIGHEST)
        a_cum_t = _dot(act0, up0, precision=HIGHEST) + _dot(act1, up1, precision=HIGHEST)
        both = jnp.concatenate([a_cum, dt], axis=0)
        hi = both.astype(BF16)
        rem = both - hi.astype(F32)
        mid = rem.astype(BF16)
        lo = (rem - mid.astype(F32)).astype(BF16)
        both_e = (_dot(hi, ex) + _dot(mid, ex)) + _dot(lo, ex)
        col_all = both_e[0:CHUNK, :]
        dt_e = both_e[CHUNK:2 * CHUNK, :]
        last_e = col_all[CHUNK - 1:CHUNK, :]
        exp_a = jnp.exp(col_all)
        to_end = jnp.exp(last_e - col_all)
        exp_end = jnp.exp(last_e)
        xdt = xs_ref[rs, :] * dt_e
        xdt_b = xdt.astype(BF16)
        xw_b = (xdt * to_end).astype(BF16)
        gwid = d // SSD_GROUPS
        for g in range(SSD_GROUPS):
            gs = slice(g * SSD_STATE, (g + 1) * SSD_STATE)
            gd = slice(g * gwid, (g + 1) * gwid)
            bm_b = bm_ref[rs, gs].astype(BF16)
            cm_b = cm_ref[rs, gs].astype(BF16)
            cb2 = _nt(cm_b, jnp.concatenate([bm_b, bm_b], axis=0))
            st = st_ref[g]
            y_off = _dot(cm_b, st.astype(BF16)) * exp_a[:, gd]
            for jj in range(n_pairs_per_group):
                pidx = g * n_pairs_per_group + jj
                sl = slice(pidx * LANES, (pidx + 1) * LANES)
                seg = col_all[:, sl] - a_cum_t[pidx:pidx + 1, :]
                lmat = (cb2 * jnp.exp(jnp.where(tril2, seg, -jnp.inf))).astype(BF16)
                xp = xdt_b[:, sl]
                rhs = jnp.where(blockdiag, jnp.concatenate([xp, xp], axis=0), jnp.zeros((), BF16))
                y_ref[rs, sl] = _dot(lmat, rhs) + y_off[:, jj * LANES:(jj + 1) * LANES]
            st_ref[g] = st * exp_end[:, gd] + _tn(bm_b, xw_b[:, gd])
        return carry

    lax.fori_loop(0, rows // CHUNK, chunk_body, 0)

    y = (y_ref[...] + xs_ref[...] * dsk_ref[...]) * _silu(z_ref[...])
    gwid = d // SSD_GROUPS
    for g in range(SSD_GROUPS):
        gd = slice(g * gwid, (g + 1) * gwid)
        o_ref[:, gd] = _rms(y[:, gd], gn_ref[:, gd]).astype(o_ref.dtype)


def _ssd_constants(d):
    n_heads = d // HEAD_DIM
    n_pairs = n_heads // 2
    pr = max(SUBLANES, n_pairs)
    ex = np.zeros((LANES, d), np.float32)
    for h in range(n_heads):
        ex[DT_LANE + h, h * HEAD_DIM:(h + 1) * HEAD_DIM] = 1.0
    lt = np.tril(np.ones((CHUNK, CHUNK), np.float32))
    up = np.triu(np.ones((CHUNK, CHUNK), np.float32))
    up0 = np.concatenate([up, np.zeros_like(up)], axis=1)
    up1 = np.concatenate([np.zeros_like(up), up], axis=1)
    sel0 = np.zeros((pr, LANES), np.float32)
    sel1 = np.zeros((pr, LANES), np.float32)
    for j in range(n_pairs):
        sel0[j, DT_LANE + 2 * j] = 1.0
        sel1[j, DT_LANE + 2 * j + 1] = 1.0
    return [jnp.asarray(ex).astype(BF16)] + [jnp.asarray(a) for a in (lt, sel0, sel1, up0, up1)]


def _ssd(pm, ps, conv_w, conv_b, dt_bias, a_log, d_skip, out_norm, batch, seq, d):
    t = pm.shape[0]
    rows = min(256, seq)
    nr = seq // rows
    gw = SSD_GROUPS * SSD_STATE
    n_heads = d // HEAD_DIM
    gwid = d // SSD_GROUPS

    def lane_vec(v):
        return jnp.zeros((1, LANES), F32).at[0, DT_LANE:DT_LANE + n_heads].set(v)

    consts = _ssd_constants(d)
    wx, wb, wc = conv_w[:, :d], conv_w[:, d:d + gw], conv_w[:, d + gw:]
    cb = conv_b.reshape(1, -1)
    bx, bb, bc = cb[:, :d], cb[:, d:d + gw], cb[:, d + gw:]
    dsk = jnp.repeat(d_skip, HEAD_DIM).reshape(1, d)

    def rowblk(width, colblk):
        return pl.BlockSpec((rows, width), lambda b, r: (b * nr + r, colblk))

    def full(a):
        return pl.BlockSpec(a.shape, lambda b, r: (0,) * a.ndim)

    small = [wx, wb, wc, bx, bb, bc, lane_vec(dt_bias), lane_vec(a_log), dsk, out_norm.reshape(1, d)] + consts
    return pl.pallas_call(
        functools.partial(_ssd_kernel, rows=rows, n_pairs_per_group=n_heads // SSD_GROUPS // 2),
        grid=(batch, nr),
        in_specs=[rowblk(d, 0), rowblk(d, 1), rowblk(gw, 2 * d // gw), rowblk(gw, 2 * d // gw + 1),
                  rowblk(LANES, 0)] + [full(a) for a in small],
        out_specs=rowblk(d, 0),
        out_shape=jax.ShapeDtypeStruct((t, d), BF16),
        scratch_shapes=[
            pltpu.VMEM((rows + SUBLANES, d), F32), pltpu.VMEM((rows + SUBLANES, gw), F32),
            pltpu.VMEM((rows + SUBLANES, gw), F32),
            pltpu.VMEM((rows, d), F32), pltpu.VMEM((rows, gw), F32), pltpu.VMEM((rows, gw), F32),
            pltpu.VMEM((rows, LANES), F32), pltpu.VMEM((rows, d), F32),
            pltpu.VMEM((SSD_GROUPS, SSD_STATE, gwid), F32),
        ],
        compiler_params=_cparams("arbitrary", "arbitrary"),
        name="ssd",
    )(pm, pm, pm, pm, ps, *small)


def _order_key(score):
    bits = lax.bitcast_convert_type(score, I32)
    return bits ^ ((bits >> 31) & jnp.int32(0x7FFFFFFF))


_NEG_INF_KEY = int(np.array(-np.inf, np.float32).view(np.int32)) ^ 0x7FFFFFFF


def _dsa_kernel(qi_s, kj_s, q_ref, qi_ref, ps_ref, ki_ref, k_hbm, v_hbm, o_ref,
                keys_ref, thr_ref, jcut_ref, qst_ref, qis_ref, wb_ref, s_ref, m_ref, l_ref, acc_ref,
                kbuf_ref, vbuf_ref, kv_sem,
                *, topk, w_scale, tk, ts, idx_bits, nkt):
    step = pl.program_id(1)
    nsteps = pl.num_programs(1)
    qblk = qi_s[step]
    kj = kj_s[step]

    g = pl.program_id(0) * nsteps + step
    total = pl.num_programs(0) * nsteps

    def kv_copies(gg):
        bb = lax.div(gg, nsteps)
        row0 = pl.multiple_of((bb * nkt + kj_s[gg - bb * nsteps]) * tk, tk)
        slot = lax.rem(gg, KV_RING)
        return (pltpu.make_async_copy(k_hbm.at[pl.ds(row0, tk), :], kbuf_ref.at[slot], kv_sem.at[0, slot]),
                pltpu.make_async_copy(v_hbm.at[pl.ds(row0, tk), :], vbuf_ref.at[slot], kv_sem.at[1, slot]))

    def kv_start(gg):
        for cp in kv_copies(gg):
            cp.start()

    @pl.when(g == 0)
    def _():
        kv_start(g)

    @pl.when((g == 0) & (total > 1))
    def _():
        kv_start(g + 1)

    @pl.when(g + 2 < total)
    def _():
        kv_start(g + 2)
    start = qblk * DSA_QBLOCK
    hw = q_ref.shape[1]
    n_heads = hw // HEAD_DIM
    n_pairs = n_heads // 2
    lane_q = lax.broadcasted_iota(I32, (DSA_QBLOCK, LANES), 1)
    low = lane_q < HEAD_DIM

    @pl.when(kj == 0)
    def _():
        zero_b = jnp.zeros((), BF16)
        for p in range(n_pairs):
            sl = slice(p * LANES, (p + 1) * LANES)
            qs = q_ref[:, sl]
            qst_ref[p, 0:DSA_QBLOCK, :] = jnp.where(low, qs, zero_b)
            qst_ref[p, DSA_QBLOCK:2 * DSA_QBLOCK, :] = jnp.where(low, zero_b, qs)
            qis = qi_ref[:, sl]
            qis_ref[(2 * p) * DSA_QBLOCK:(2 * p + 1) * DSA_QBLOCK, :] = jnp.where(low, qis, zero_b)
            qis_ref[(2 * p + 1) * DSA_QBLOCK:(2 * p + 2) * DSA_QBLOCK, :] = jnp.where(low, zero_b, qis)
        w = ps_ref[...] * w_scale
        for h in range(n_heads):
            wb_ref[h] = jnp.broadcast_to(w[:, WIDX_LANE + h:WIDX_LANE + h + 1], (DSA_QBLOCK, LANES))
        m_ref[...] = jnp.full(m_ref.shape, NEG_BIG, F32)
        l_ref[...] = jnp.zeros_like(l_ref)
        acc_ref[...] = jnp.zeros_like(acc_ref)

        n_tiles = (start + DSA_QBLOCK + ts - 1) // ts
        row = lax.broadcasted_iota(I32, (DSA_QBLOCK, ts), 0)
        kcol = lax.broadcasted_iota(I32, (DSA_QBLOCK, ts), 1)
        vis_end = start + CHUNK + CHUNK * (row // CHUNK)

        def score_body(t, carry):
            mx, mn = carry
            base = pl.multiple_of(t * ts, ts)
            kt = ki_ref[pl.ds(base, ts), :]
            rel_all = _nt(qis_ref[...], kt)
            slabs = []
            for c in range(ts // LANES):
                acc = jnp.zeros((DSA_QBLOCK, LANES), F32)
                for h in range(n_heads):
                    rel = rel_all[h * DSA_QBLOCK:(h + 1) * DSA_QBLOCK, c * LANES:(c + 1) * LANES]
                    acc = acc + jnp.maximum(rel, 0.0) * wb_ref[h]
                slabs.append(acc)
            sc = jnp.concatenate(slabs, axis=1)
            sc = jnp.where(sc == 0.0, 0.0, sc)
            adm = kcol + base < vis_end
            lowest = jnp.where(adm, sc, jnp.inf)
            sc = jnp.where(adm, sc, -jnp.inf)
            keys_ref[:, pl.ds(base, ts)] = _order_key(sc)
            for c in range(ts // LANES):
                mx = jnp.maximum(mx, sc[:, c * LANES:(c + 1) * LANES])
                mn = jnp.minimum(mn, lowest[:, c * LANES:(c + 1) * LANES])
            return mx, mn

        mx, mn = lax.fori_loop(0, n_tiles, score_body,
                               (jnp.full((DSA_QBLOCK, LANES), -jnp.inf, F32),
                                jnp.full((DSA_QBLOCK, LANES), jnp.inf, F32)))
        row_max = jnp.broadcast_to(jnp.max(mx, axis=1, keepdims=True), (DSA_QBLOCK, LANES))
        row_min = jnp.broadcast_to(jnp.min(mn, axis=1, keepdims=True), (DSA_QBLOCK, LANES))

        def pad_body(t, carry):
            base = pl.multiple_of(t * ts, ts)
            keys_ref[:, pl.ds(base, ts)] = jnp.full((DSA_QBLOCK, ts), _NEG_INF_KEY, I32)
            return carry

        n_cover = ((start + DSA_QBLOCK + tk - 1) // tk) * (tk // ts)
        lax.fori_loop(n_tiles, n_cover, pad_body, 0)

        zeros_f = jnp.zeros((DSA_QBLOCK, LANES), F32)
        kf = float(topk)

        def count_ge(cand_key):
            def count_body(t, cnt):
                base = pl.multiple_of(t * ts, ts)
                kt = keys_ref[:, pl.ds(base, ts)]
                for c in range(ts // LANES):
                    cnt = cnt + jnp.where(kt[:, c * LANES:(c + 1) * LANES] >= cand_key, 1.0, 0.0)
                return cnt

            cnt = lax.fori_loop(0, n_tiles, count_body, zeros_f)
            return jnp.broadcast_to(jnp.sum(cnt, axis=1, keepdims=True), (DSA_QBLOCK, LANES))

        n_adm = vis_end[:, 0:LANES].astype(F32)
        few = n_adm < kf
        c_max = count_ge(_order_key(row_max))
        at_max = c_max >= kf
        lo0 = jnp.where(few, -jnp.inf, jnp.where(at_max, row_max, row_min))
        done0 = jnp.where(few | at_max | (n_adm == kf), 1.0, 0.0)
        long_row = n_adm > LOG_COUNT_MIN_KEYS

        def residual(c):
            return jnp.where(long_row, jnp.log((c + 0.5) * (1.0 / kf)), c - kf + 0.5)

        def search_step(st):
            lo, hi, c_lo, f_lo, f_hi, side, done = st
            cand = lo + (hi - lo) * (f_lo / (f_lo - f_hi))
            inside = (cand > lo) & (cand < hi)
            cand = jnp.where(inside, cand, 0.5 * lo + 0.5 * hi)
            stuck = jnp.logical_not((cand > lo) & (cand < hi))
            c = count_ge(_order_key(cand))
            live = jnp.logical_not(stuck) & (done < 0.5)
            up = live & (c >= kf)
            dn = live & (c < kf)
            f_hi = jnp.where(up & (side > 0.5), 0.5 * f_hi, f_hi)
            f_lo = jnp.where(dn & (side < -0.5), 0.5 * f_lo, f_lo)
            side = jnp.where(up, 1.0, jnp.where(dn, -1.0, side))
            lo = jnp.where(up, cand, lo)
            c_lo = jnp.where(up, c, c_lo)
            f_c = residual(c)
            f_lo = jnp.where(up, f_c, f_lo)
            hi = jnp.where(dn, cand, hi)
            f_hi = jnp.where(dn, f_c, f_hi)
            done = jnp.where(stuck | (c_lo == kf), 1.0, done)
            return (lo, hi, c_lo, f_lo, f_hi, side, done)

        def search_body(st):
            inner = search_step(search_step(st[2:]))
            return (st[0] + 2, (jnp.min(inner[-1]) < 0.5).astype(I32)) + inner

        state = (jnp.int32(0), (jnp.min(done0) < 0.5).astype(I32), lo0, row_max, n_adm,
                 residual(n_adm), residual(c_max), zeros_f, done0)
        state = lax.while_loop(lambda st: (st[0] < SEARCH_MAX_ITERS) & (st[1] > 0), search_body, state)
        thr_ref[...] = _order_key(state[2])

        @pl.when(state[1] > 0)
        def _():
            sign_bit = jnp.int32(-2 ** 31)

            def bisect_body(it, res):
                cand = res | jnp.left_shift(jnp.int32(1), 31 - it)
                return jnp.where(count_ge(cand ^ sign_bit) >= kf, cand, res)

            res = lax.fori_loop(0, 32, bisect_body, jnp.zeros((DSA_QBLOCK, LANES), I32))
            thr_ref[...] = res ^ sign_bit

        thr = thr_ref[...]
        jcut_ref[...] = jnp.full((DSA_QBLOCK, LANES), 2 ** idx_bits - 1, I32)
        settled = jnp.where((state[4] == kf) | few, 1.0, 0.0)

        @pl.when(jnp.min(settled) < 0.5)
        def _():
            def tally_body(t, carry):
                gt, ge = carry
                base = pl.multiple_of(t * ts, ts)
                kt = keys_ref[:, pl.ds(base, ts)]
                for c in range(ts // LANES):
                    ks = kt[:, c * LANES:(c + 1) * LANES]
                    gt = gt + jnp.where(ks > thr, 1.0, 0.0)
                    ge = ge + jnp.where(ks >= thr, 1.0, 0.0)
                return gt, ge

            gt, ge = lax.fori_loop(0, n_tiles, tally_body, (zeros_f, zeros_f))
            need = float(topk) - jnp.sum(gt, axis=1, keepdims=True)
            n_ge = jnp.sum(ge, axis=1, keepdims=True)

            @pl.when(jnp.max(n_ge) > float(topk))
            def _():
                def cut_body(it, cut):
                    cand = cut | jnp.left_shift(jnp.int32(1), idx_bits - 1 - it)

                    def count_body(t, cnt):
                        base = pl.multiple_of(t * ts, ts)
                        kt = keys_ref[:, pl.ds(base, ts)]
                        for c in range(ts // LANES):
                            idx = lane_q + (base + c * LANES)
                            hit = (kt[:, c * LANES:(c + 1) * LANES] == thr) & (idx < cand)
                            cnt = cnt + jnp.where(hit, 1.0, 0.0)
                        return cnt

                    cnt = lax.fori_loop(0, n_tiles, count_body, zeros_f)
                    return jnp.where(jnp.sum(cnt, axis=1, keepdims=True) <= need, cand, cut)

                jcut_ref[...] = lax.fori_loop(0, idx_bits, cut_body, jnp.zeros((DSA_QBLOCK, LANES), I32))

    kbase = pl.multiple_of(kj * tk, tk)
    key_t = keys_ref[:, pl.ds(kbase, tk)]
    thr = thr_ref[...]
    reps = tk // LANES
    thr_t = jnp.concatenate([thr] * reps, axis=1)
    cut_t = jnp.concatenate([jcut_ref[...]] * reps, axis=1)
    kidx = lax.broadcasted_iota(I32, (DSA_QBLOCK, tk), 1) + kbase
    mask = ((key_t > thr_t) | ((key_t == thr_t) & (kidx < cut_t))) & (key_t > jnp.int32(_NEG_INF_KEY))

    pen = jnp.where(mask, 0.0, NEG_BIG).astype(BF16)
    for cp in kv_copies(g):
        cp.wait()
    slot = lax.rem(g, KV_RING)

    def qk(p):
        s_ref[p] = _nt(qst_ref[p], kbuf_ref[slot, :, p * LANES:(p + 1) * LANES]).astype(BF16)

    qk(0)

    def lane_slabs(a):
        return [a[:, c * LANES:(c + 1) * LANES] for c in range(reps)]

    for p in range(n_pairs):
        if p + 1 < n_pairs:
            qk(p + 1)
        pes, alphas = [], []
        for half in range(2):
            rows = slice(half * DSA_QBLOCK, (half + 1) * DSA_QBLOCK)
            s = s_ref[p, rows, :] + pen
            row_max = jnp.max(_tree(lane_slabs(s), jnp.maximum).astype(F32), axis=1, keepdims=True)
            m_old = m_ref[p, rows, :]
            m_new = jnp.maximum(m_old, row_max)
            alpha = jnp.exp(m_old - m_new)
            pe = jnp.exp(s - jnp.concatenate([m_new.astype(BF16)] * reps, axis=1))
            row_sum = jnp.sum(_tree(lane_slabs(pe), jnp.add).astype(F32), axis=1, keepdims=True)
            m_ref[p, rows, :] = m_new
            l_ref[p, rows, :] = alpha * l_ref[p, rows, :] + row_sum
            pes.append(pe)
            alphas.append(alpha)
        pv = _dot(jnp.concatenate(pes, axis=0), vbuf_ref[slot, :, p * LANES:(p + 1) * LANES])
        acc_ref[p] = jnp.concatenate(alphas, axis=0) * acc_ref[p] + pv

    last = (start + DSA_QBLOCK - 1) // tk

    @pl.when(kj == last)
    def _():
        for p in range(n_pairs):
            oa = acc_ref[p, 0:DSA_QBLOCK, :] / l_ref[p, 0:DSA_QBLOCK, :]
            ob = acc_ref[p, DSA_QBLOCK:2 * DSA_QBLOCK, :] / l_ref[p, DSA_QBLOCK:2 * DSA_QBLOCK, :]
            o_ref[:, p * LANES:(p + 1) * LANES] = jnp.where(low, oa, ob).astype(o_ref.dtype)


def _dsa(q, k, v, qi, ki, ps, batch, seq, n_idx_heads):
    t, hw = q.shape
    nq = seq // DSA_QBLOCK
    tk = min(1024, seq)
    ts = min(1024, seq)
    nkt = seq // tk
    topk = min(DSA_TOPK_MAX, seq // 4)
    n_heads = hw // HEAD_DIM
    qi_list, kj_list = [], []
    for i in range(nq):
        for j in range((i * DSA_QBLOCK + DSA_QBLOCK - 1) // tk + 1):
            qi_list.append(i)
            kj_list.append(j)
    qi_arr = jnp.asarray(np.array(qi_list, np.int32))
    kj_arr = jnp.asarray(np.array(kj_list, np.int32))
    nsteps = len(qi_list)

    qspec = pl.BlockSpec((DSA_QBLOCK, hw), lambda b, s, qs, ks: (b * nq + qs[s], 0))
    kspec = pl.BlockSpec((tk, hw), lambda b, s, qs, ks: (b * nkt + ks[s], 0))
    grid_spec = pltpu.PrefetchScalarGridSpec(
        num_scalar_prefetch=2,
        grid=(batch, nsteps),
        in_specs=[
            qspec, qspec,
            pl.BlockSpec((DSA_QBLOCK, LANES), lambda b, s, qs, ks: (b * nq + qs[s], 0)),
            pl.BlockSpec((seq, LANES), lambda b, s, qs, ks: (b, 0)),
            pl.BlockSpec(memory_space=pl.ANY), pl.BlockSpec(memory_space=pl.ANY),
        ],
        out_specs=qspec,
        scratch_shapes=[
            pltpu.VMEM((DSA_QBLOCK, seq), I32),
            pltpu.VMEM((DSA_QBLOCK, LANES), I32),
            pltpu.VMEM((DSA_QBLOCK, LANES), I32),
            pltpu.VMEM((n_heads // 2, 2 * DSA_QBLOCK, LANES), BF16),
            pltpu.VMEM((n_idx_heads * DSA_QBLOCK, LANES), BF16),
            pltpu.VMEM((n_idx_heads, DSA_QBLOCK, LANES), F32),
            pltpu.VMEM((n_heads // 2, 2 * DSA_QBLOCK, tk), BF16),
            pltpu.VMEM((n_heads // 2, 2 * DSA_QBLOCK, LANES), F32),
            pltpu.VMEM((n_heads // 2, 2 * DSA_QBLOCK, LANES), F32),
            pltpu.VMEM((n_heads // 2, 2 * DSA_QBLOCK, LANES), F32),
            pltpu.VMEM((KV_RING, tk, hw), BF16),
            pltpu.VMEM((KV_RING, tk, hw), BF16),
            pltpu.SemaphoreType.DMA((2, KV_RING)),
        ],
    )
    w_scale = float(n_idx_heads) ** -0.5 * float(HEAD_DIM) ** -0.5
    return pl.pallas_call(
        functools.partial(_dsa_kernel, topk=topk, w_scale=w_scale, tk=tk, ts=ts, idx_bits=seq.bit_length(), nkt=nkt),
        grid_spec=grid_spec,
        out_shape=jax.ShapeDtypeStruct((t, hw), BF16),
        compiler_params=_cparams("arbitrary", "arbitrary"),
        name="dsa",
    )(qi_arr, kj_arr, q, qi, ps, ki, k, v)


BAND_QROWS = 2 * CHUNK
BAND_KBLOCKS = (BAND_PREV * CHUNK) // BAND_QROWS + 1


def _band_kernel(*refs, q_scale):
    q_ref = refs[0]
    k_refs = refs[1:1 + BAND_KBLOCKS]
    v_refs = refs[1 + BAND_KBLOCKS:1 + 2 * BAND_KBLOCKS]
    bias_ref = refs[1 + 2 * BAND_KBLOCKS]
    o_ref = refs[2 + 2 * BAND_KBLOCKS]
    kbuf_ref, vbuf_ref = refs[3 + 2 * BAND_KBLOCKS:]
    i = pl.program_id(1)
    d = q_ref.shape[1]
    nkeys = BAND_KBLOCKS * BAND_QROWS
    for blk in range(BAND_KBLOCKS):
        kbuf_ref[blk * BAND_QROWS:(blk + 1) * BAND_QROWS, :] = k_refs[blk][...]
        vbuf_ref[blk * BAND_QROWS:(blk + 1) * BAND_QROWS, :] = v_refs[blk][...]
    kcol = lax.broadcasted_iota(I32, (2 * BAND_QROWS, nkeys), 1)
    valid = kcol >= (BAND_KBLOCKS - 1 - i) * BAND_QROWS
    low = lax.broadcasted_iota(I32, (BAND_QROWS, LANES), 1) < HEAD_DIM
    zero_b = jnp.zeros((), BF16)
    def logits(p):
        sl = slice(p * LANES, (p + 1) * LANES)
        qs = q_ref[:, sl] * q_scale
        qst = jnp.concatenate([jnp.where(low, qs, zero_b), jnp.where(low, zero_b, qs)], axis=0)
        s = _nt(qst, kbuf_ref[:, sl]) + bias_ref[p]
        return jnp.where(valid, s, -jnp.inf).astype(BF16)

    n_slabs = d // LANES
    s_next = logits(0)
    for p in range(n_slabs):
        sl = slice(p * LANES, (p + 1) * LANES)
        s = s_next
        if p + 1 < n_slabs:
            s_next = logits(p + 1)
        slabs = [s[:, c * LANES:(c + 1) * LANES] for c in range(nkeys // LANES)]
        row_max = jnp.max(_tree(slabs, jnp.maximum).astype(F32), axis=1, keepdims=True).astype(BF16)
        es = [jnp.exp(sl_c - row_max) for sl_c in slabs]
        denom = jnp.sum(_tree(es, jnp.add).astype(F32), axis=1, keepdims=True)
        pv = _dot(jnp.concatenate(es, axis=1), vbuf_ref[:, sl]) / denom
        o_ref[:, sl] = jnp.where(low, pv[0:BAND_QROWS], pv[BAND_QROWS:2 * BAND_QROWS]).astype(o_ref.dtype)


def _band_bias(rel_bias):
    nkeys = BAND_KBLOCKS * BAND_QROWS
    tq = np.arange(BAND_QROWS)[:, None]
    kj = np.arange(nkeys)[None, :]
    qc = tq // CHUNK
    kc = kj // CHUNK
    visible = (kc >= qc) & (kc <= qc + BAND_PREV)
    shift = BAND_QROWS - 1
    m = np.arange(nkeys + shift)
    dist = m - shift - (BAND_KBLOCKS - 1) * BAND_QROWS
    ext = rel_bias.astype(F32)[:, np.clip(dist, -REL_PAST, CHUNK - 1) + REL_PAST]
    bias = jnp.stack([ext[:, shift - t:shift - t + nkeys] for t in range(BAND_QROWS)], axis=1)
    return jnp.where(jnp.asarray(visible)[None], bias, -jnp.inf)


def _band(qkv, rel_bias, batch, seq, d):
    t = qkv.shape[0]
    nq = seq // BAND_QROWS
    n_heads = d // HEAD_DIM
    nkeys = BAND_KBLOCKS * BAND_QROWS
    bias = _band_bias(rel_bias).reshape(n_heads // 2, 2 * BAND_QROWS, nkeys)

    def kv_spec(blk, col):
        back = BAND_KBLOCKS - 1 - blk
        return pl.BlockSpec((BAND_QROWS, d), lambda b, i: (b * nq + jnp.maximum(i - back, 0), col))

    in_specs = [pl.BlockSpec((BAND_QROWS, d), lambda b, i: (b * nq + i, 0))]
    in_specs += [kv_spec(blk, 1) for blk in range(BAND_KBLOCKS)]
    in_specs += [kv_spec(blk, 2) for blk in range(BAND_KBLOCKS)]
    in_specs += [pl.BlockSpec((n_heads // 2, 2 * BAND_QROWS, nkeys), lambda b, i: (0, 0, 0))]
    return pl.pallas_call(
        functools.partial(_band_kernel, q_scale=HEAD_DIM ** -0.5),
        grid=(batch, nq),
        in_specs=in_specs,
        out_specs=pl.BlockSpec((BAND_QROWS, d), lambda b, i: (b * nq + i, 0)),
        out_shape=jax.ShapeDtypeStruct((t, d), BF16),
        scratch_shapes=[pltpu.VMEM((nkeys, d), BF16), pltpu.VMEM((nkeys, d), BF16)],
        compiler_params=_cparams("parallel", "arbitrary"),
        name="band",
    )(*([qkv] * (1 + 2 * BAND_KBLOCKS)), bias)


def _rope_tables(seq):
    half = HEAD_DIM // 2
    inv = ROPE_THETA ** (-jnp.arange(half, dtype=F32) / half)
    ang = jnp.arange(seq, dtype=F32)[:, None] * inv[None, :]
    cos = jnp.cos(ang)
    sin = jnp.sin(ang)
    reps = LANES // HEAD_DIM
    cos_t = jnp.tile(jnp.concatenate([cos, cos], axis=1), (1, reps))
    sin_t = jnp.tile(jnp.concatenate([-sin, sin], axis=1), (1, reps))
    return cos_t, sin_t


def _ssd_dsa_mixer(x, norm, w_in, conv_w, conv_b, dt_bias, a_log, d_skip, out_norm, w_out, batch, seq):
    t, d = x.shape
    hw = d // 2
    n_ssd_heads = d // HEAD_DIM
    n_idx_heads = d // 128
    xbc = d + 2 * SSD_GROUPS * SSD_STATE
    sizes = (d, xbc, n_ssd_heads, hw, hw, hw, n_idx_heads * HEAD_DIM, HEAD_DIM, n_idx_heads)
    offs = np.concatenate([[0], np.cumsum(sizes)])
    seg = [w_in[:, offs[i]:offs[i + 1]] for i in range(len(sizes))]
    w_z, w_xbc, w_dt, w_q, w_k, w_v, w_qi, w_ki, w_wi = seg
    w_main = jnp.concatenate([w_z, w_xbc, w_q, w_k, w_v, w_qi], axis=1).astype(BF16)
    w_small = jnp.zeros((d, LANES), F32)
    w_small = w_small.at[:, KIDX_LANE:KIDX_LANE + HEAD_DIM].set(w_ki)
    w_small = w_small.at[:, DT_LANE:DT_LANE + n_ssd_heads].set(w_dt)
    w_small = w_small.at[:, WIDX_LANE:WIDX_LANE + n_idx_heads].set(w_wi)
    pm = _norm_matmul(x, norm, w_main, hw, F32)
    ps = _norm_matmul(x, norm, w_small.astype(BF16), LANES, F32)
    y = _ssd(pm, ps, conv_w, conv_b, dt_bias, a_log, d_skip, out_norm, batch, seq, d)
    cos_t, sin_t = _rope_tables(seq)
    col_q = (d + xbc) // hw
    q, k, v, qi, ki = _dsa_prep(pm, ps, cos_t, sin_t, seq, hw, col_q)
    o = _dsa(q, k, v, qi, ki, ps, batch, seq, n_idx_heads)
    w_out_b = w_out.astype(BF16)
    return _proj_res(x, [y, o], [w_out_b[:d], w_out_b[d:]])


def _band_mixer(x, norm, w_qkv, rel_bias, w_out, batch, seq):
    t, d = x.shape
    qkv = _norm_matmul(x, norm, w_qkv.astype(BF16), d // 2, BF16)
    o = _band(qkv, rel_bias, batch, seq, d)
    return _proj_res(x, [o], [w_out.astype(BF16)])


def kernel(x, ffn1_norm, ffn1_w_in, ffn1_w_out, mix_norm, ab_w_in, ssd_conv_w, ssd_conv_b, ssd_dt_bias, ssd_a_log, ssd_d_skip, ssd_out_norm, ab_w_out, c_w_qkv, c_rel_bias, c_w_out, ffn2_norm, ffn2_w_in, ffn2_w_out, final_norm):
    batch, seq, d = x.shape
    depth = ffn1_norm.shape[0]
    xf = x.reshape(batch * seq, d)
    w1_in, w1_out = ffn1_w_in.astype(BF16), ffn1_w_out.astype(BF16)
    w2_in, w2_out = ffn2_w_in.astype(BF16), ffn2_w_out.astype(BF16)
    for layer in range(depth):
        xf = _ffn(xf, ffn1_norm[layer], w1_in, w1_out, layer)
        i = layer // 2
        if layer % 2 == 0:
            xf = _ssd_dsa_mixer(xf, mix_norm[layer], ab_w_in[i], ssd_conv_w[i], ssd_conv_b[i], ssd_dt_bias[i],
                                ssd_a_log[i], ssd_d_skip[i], ssd_out_norm[i], ab_w_out[i], batch, seq)
        else:
            xf = _band_mixer(xf, mix_norm[layer], c_w_qkv[i], c_rel_bias[i], c_w_out[i], batch, seq)
        fin = final_norm if layer == depth - 1 else None
        xf = _ffn(xf, ffn2_norm[layer], w2_in, w2_out, layer, fin)
    return xf.reshape(batch, seq, d)
```
